```python
import jax, jax.numpy as jnp
from jax import lax
import numpy as np

D_MODEL = 1024
BATCH = 32
SEQ = 2048
DEPTH = 2

CHUNK = 64
N_META = 16
EPS = 1e-6
ROPE_BASE = 10000.0

RET_HEADS = D_MODEL // 256
RET_DK = 128
RET_DV = 128
HG_HEADS = D_MODEL // 256
HG_DK = 128
HG_DV = 128
RW_HEADS = D_MODEL // 128
RW_HD = 64
RW_DECAY_LORA = 64
RW_AAA_LORA = 64
RW_GATE_LORA = 128
RW_LNX_EPS = 64e-5
GDN_HEADS = D_MODEL // 256
GDN_DK = 128
GDN_DV = 128
GDN_CONV = 4
D_FF = 128 * ((8 * D_MODEL // 3 + 127) // 128)
FFN_CONV = 3

RET_KW = RET_HEADS * RET_DK
RET_VW = RET_HEADS * RET_DV
HG_KW = HG_HEADS * HG_DK
HG_VW = HG_HEADS * HG_DV
RW_W = RW_HEADS * RW_HD
GDN_KW = GDN_HEADS * GDN_DK
GDN_VW = GDN_HEADS * GDN_DV

EVEN_SPLITS = (RET_KW, RET_KW, RET_VW, RET_VW, HG_KW, HG_KW, HG_VW, HG_VW)
EVEN_IN = sum(EVEN_SPLITS)
EVEN_MIX = RET_VW + HG_VW
RW_SPLITS = (RW_W, RW_W, RW_W, RW_DECAY_LORA, RW_AAA_LORA, RW_GATE_LORA)
RW_IN = sum(RW_SPLITS)
GDN_CONV_CH = 2 * GDN_KW + GDN_VW
GDN_REST = (GDN_VW, GDN_HEADS, GDN_HEADS)
GDN_IN = GDN_CONV_CH + sum(GDN_REST)
ODD_IN = RW_IN + GDN_IN
ODD_MIX = RW_W + GDN_VW

N_EVEN = (DEPTH + 1) // 2
N_ODD = DEPTH // 2

kernel_name = 'hybrid_retention_hgrn2_rwkv7_gdn_convffn'

F32 = jnp.float32


def split_cols(p, sizes):
    return jnp.split(p, [int(c) for c in np.cumsum(sizes)[:-1]], axis=-1)


def rmsnorm(x, g):
    xf = x.astype(F32)
    y = xf * lax.rsqrt(jnp.mean(xf * xf, axis=-1, keepdims=True) + EPS)
    return (y * g.astype(F32)).astype(x.dtype)


def head_layernorm(x, eps):
    xf = x.astype(F32)
    xc = xf - jnp.mean(xf, axis=-1, keepdims=True)
    return xc * lax.rsqrt(jnp.mean(xc * xc, axis=-1, keepdims=True) + eps)


def l2norm(x):
    xf = x.astype(F32)
    return xf * lax.rsqrt(jnp.sum(xf * xf, axis=-1, keepdims=True) + EPS)


def token_shift(t):
    return jnp.pad(t[:, :-1], ((0, 0), (1, 0), (0, 0)))


def causal_dwconv(t, w):
    k_taps = w.shape[0]
    l = t.shape[1]
    tp = jnp.pad(t, ((0, 0), (k_taps - 1, 0), (0, 0)))
    y = tp[:, 0:l] * w[0]
    for j in range(1, k_taps):
        y = y + tp[:, j:j + l] * w[j]
    return y


def rotary(t, pos):
    half = t.shape[-1] // 2
    inv = ROPE_BASE ** (-jnp.arange(half, dtype=F32) / half)
    ang = pos.astype(F32)[:, None] * inv[None, :]
    cos = jnp.cos(ang)[None, :, None, :]
    sin = jnp.sin(ang)[None, :, None, :]
    t1 = t[..., :half].astype(F32)
    t2 = t[..., half:].astype(F32)
    return jnp.concatenate([t1 * cos - t2 * sin, t1 * sin + t2 * cos], axis=-1)


def to_chunks(t):
    b, l, h, d = t.shape
    pad = (-l) % CHUNK
    t = jnp.pad(t.astype(F32), ((0, 0), (pad, 0), (0, 0), (0, 0)))
    n = (l + pad) // CHUNK
    return t.reshape(b, n, CHUNK, h, d).transpose(1, 0, 3, 2, 4)


def from_chunks(t, l):
    n, b, h, c, d = t.shape
    return t.transpose(1, 0, 3, 2, 4).reshape(b, n * c, h, d)[:, n * c - l:]


def retention_chunkwise(q, k, v):
    b, l, h, dk = q.shape
    dv = v.shape[-1]
    log_gamma = jnp.log1p(-jnp.exp2(-5.0 - jnp.arange(h, dtype=F32)))
    idx = jnp.arange(CHUNK, dtype=F32)
    diff = idx[:, None] - idx[None, :]
    intra = jnp.where(diff >= 0, jnp.exp(log_gamma[:, None, None] * jnp.maximum(diff, 0.0)), 0.0)
    q_dec = jnp.exp(log_gamma[:, None] * (idx + 1.0))[..., None]
    k_dec = jnp.exp(log_gamma[:, None] * (CHUNK - 1.0 - idx))[..., None]
    s_dec = jnp.exp(log_gamma * CHUNK)[:, None, None]

    def step(s, inp):
        qi, ki, vi = inp
        att = jnp.einsum('bhid,bhjd->bhij', qi, ki) * intra
        o = jnp.einsum('bhij,bhjv->bhiv', att, vi) + jnp.einsum('bhid,bhdv->bhiv', qi * q_dec, s)
        s = s * s_dec + jnp.einsum('bhjd,bhjv->bhdv', ki * k_dec, vi)
        return s, o

    s0 = jnp.zeros((b, h, dk, dv), F32)
    _, o = lax.scan(step, s0, (to_chunks(q), to_chunks(k), to_chunks(v)))
    return from_chunks(o, l)


def hgrn2_chunkwise(q, k, v, log_f):
    b, l, h, dk = q.shape
    dv = v.shape[-1]
    causal = jnp.tril(jnp.ones((CHUNK, CHUNK), bool))[:, :, None]

    def step(s, inp):
        qi, ki, vi, lfi = inp
        cb = jnp.cumsum(lfi, axis=2)
        rel = jnp.exp(jnp.where(causal, cb[:, :, :, None, :] - cb[:, :, None, :, :], -jnp.inf))
        att = jnp.einsum('bhid,bhjd,bhijd->bhij', qi, ki, rel)
        o = jnp.einsum('bhij,bhjv->bhiv', att, vi) + jnp.einsum('bhid,bhdv->bhiv', qi * jnp.exp(cb), s)
        last = cb[:, :, -1:, :]
        s = s * jnp.exp(last)[:, :, 0, :, None] + jnp.einsum('bhjd,bhjv->bhdv', ki * jnp.exp(last - cb), vi)
        return s, o

    s0 = jnp.zeros((b, h, dk, dv), F32)
    _, o = lax.scan(step, s0, (to_chunks(q), to_chunks(k), to_chunks(v), to_chunks(log_f)))
    return from_chunks(o, l)


def gated_delta_chunkwise(q, k, v, g, beta):
    b, l, h, dk = q.shape
    dv = v.shape[-1]
    qc, kc, vc = to_chunks(q), to_chunks(k), to_chunks(v)
    gc = to_chunks(g[..., None])[..., 0]
    bc = to_chunks(beta[..., None])[..., 0]
    cg = jnp.cumsum(gc, axis=-1)
    incl = jnp.tril(jnp.ones((CHUNK, CHUNK), bool))
    strict = jnp.tril(jnp.ones((CHUNK, CHUNK), bool), -1)
    decay_in = jnp.exp(jnp.where(incl, cg[..., :, None] - cg[..., None, :], -jnp.inf))
    kk = jnp.einsum('nbhid,nbhjd->nbhij', kc, kc)
    m = jnp.eye(CHUNK, dtype=F32) + jnp.where(strict, bc[..., :, None] * kk * decay_in, 0.0)
    rhs = jnp.concatenate([bc[..., None] * vc, bc[..., None] * kc * jnp.exp(cg)[..., None]], axis=-1)
    sol = lax.linalg.triangular_solve(m, rhs, left_side=True, lower=True, unit_diagonal=True)
    u, w = sol[..., :dv], sol[..., dv:]
    qk = jnp.einsum('nbhid,nbhjd->nbhij', qc, kc) * decay_in
    q_in = qc * jnp.exp(cg)[..., None]
    k_out = kc * jnp.exp(cg[..., -1:] - cg)[..., None]
    s_dec = jnp.exp(cg[..., -1])[..., None, None]

    def step(s, inp):
        q_i, k_i, u_i, w_i, qk_i, sd_i = inp
        v_new = u_i - jnp.einsum('bhcd,bhdv->bhcv', w_i, s)
        o = jnp.einsum('bhcd,bhdv->bhcv', q_i, s) + jnp.einsum('bhij,bhjv->bhiv', qk_i, v_new)
        s = s * sd_i + jnp.einsum('bhcd,bhcv->bhdv', k_i, v_new)
        return s, o

    s0 = jnp.zeros((b, h, dk, dv), F32)
    _, o = lax.scan(step, s0, (q_in, k_out, u, w, qk, s_dec))
    return from_chunks(o, l)


def rwkv7_scan(r, w, k, v, a, bb):
    bsz, l, h, d = r.shape

    def step(s, inp):
        rt, wt, kt, vt, at, bt = inp
        sa = jnp.einsum('bhvk,bhk->bhv', s, at)
        s = s * wt[:, :, None, :] + sa[..., None] * bt[:, :, None, :] + vt[..., None] * kt[:, :, None, :]
        return s, jnp.einsum('bhvk,bhk->bhv', s, rt)

    tm = lambda t: jnp.swapaxes(t.astype(F32), 0, 1)
    s0 = jnp.zeros((bsz, h, d, d), F32)
    _, y = lax.scan(step, s0, tuple(tm(t) for t in (r, w, k, v, a, bb)))
    return jnp.swapaxes(y, 0, 1)


def even_mixer(u, w_in, w_out, lb, hg_gain, pos):
    bsz, l, _ = u.shape
    p = u @ w_in
    qa, ka, va, ga, qb, fb, ib, gb = split_cols(p, EVEN_SPLITS)
    hs = lambda t, h: t.reshape(bsz, l, h, -1)
    qa = rotary(hs(qa, RET_HEADS), pos)
    ka = rotary(hs(ka, RET_HEADS), pos) * (RET_DK ** -0.5)
    oa = retention_chunkwise(qa, ka, hs(va, RET_HEADS))
    oa = head_layernorm(oa, EPS) * jax.nn.silu(hs(ga, RET_HEADS).astype(F32))
    fb32 = fb.astype(F32)
    lb32 = lb.astype(F32)
    log_f = jnp.log(lb32 + (1.0 - lb32) * jax.nn.sigmoid(fb32))
    kb = (1.0 - lb32) * jax.nn.sigmoid(-fb32)
    ob = hgrn2_chunkwise(hs(qb, HG_HEADS), hs(kb, HG_HEADS), hs(ib, HG_HEADS), hs(log_f, HG_HEADS))
    ob = rmsnorm(ob, hg_gain.reshape(HG_HEADS, HG_DV)) * jax.nn.silu(hs(gb, HG_HEADS).astype(F32))
    y = jnp.concatenate([oa.reshape(bsz, l, RET_VW), ob.reshape(bsz, l, HG_VW)], axis=-1).astype(u.dtype)
    return y @ w_out


def odd_mixer(u, w_in, w_out, rw_mu, rw_w0, rw_w2, rw_a0, rw_a2, rw_g2, rw_kk_scale, rw_ka_scale,
              rw_rk, rw_lnx_w, rw_lnx_b, gdn_conv_w, gdn_a_log, gdn_dt_bias, gdn_norm_gain):
    bsz, l, _ = u.shape
    p = u @ w_in
    pc, pd = p[..., :RW_IN], p[..., RW_IN:]
    pc = pc + (token_shift(pc) - pc) * rw_mu
    r, k, v, w_lo, a_lo, g_lo = split_cols(pc, RW_SPLITS)
    rh = lambda t: t.reshape(bsz, l, RW_HEADS, RW_HD)
    log_w = -jax.nn.softplus(-(rw_w0 + jnp.tanh(w_lo) @ rw_w2).astype(F32)) - 0.5
    decay = jnp.exp(-jnp.exp(log_w))
    a = jax.nn.sigmoid((rw_a0 + a_lo @ rw_a2).astype(F32))
    gate = jax.nn.sigmoid(g_lo) @ rw_g2
    kk = l2norm(rh(k * rw_kk_scale))
    k = k.astype(F32) * (1.0 + (a - 1.0) * rw_ka_scale.astype(F32))
    r_h, k_h, v_h, a_h = rh(r.astype(F32)), rh(k), rh(v.astype(F32)), rh(a)
    y = rwkv7_scan(r_h, rh(decay), k_h, v_h, -kk, kk * a_h)
    y = head_layernorm(y, RW_LNX_EPS) * rw_lnx_w.reshape(RW_HEADS, RW_HD) + rw_lnx_b.reshape(RW_HEADS, RW_HD)
    y = y + jnp.sum(r_h * k_h * rw_rk, axis=-1, keepdims=True) * v_h
    y_c = y.reshape(bsz, l, RW_W) * gate
    qkv = jax.nn.silu(causal_dwconv(pd[..., :GDN_CONV_CH], gdn_conv_w))
    qd, kd, vd = split_cols(qkv, (GDN_KW, GDN_KW, GDN_VW))
    og, al, be = split_cols(pd[..., GDN_CONV_CH:], GDN_REST)
    gh = lambda t: t.reshape(bsz, l, GDN_HEADS, -1)
    qd = l2norm(gh(qd)) * (GDN_DK ** -0.5)
    kd = l2norm(gh(kd))
    g_log = -jnp.exp(gdn_a_log.astype(F32)) * jax.nn.softplus((al + gdn_dt_bias).astype(F32))
    beta = jax.nn.sigmoid(be.astype(F32))
    o = gated_delta_chunkwise(qd, kd, gh(vd), g_log, beta)
    o = rmsnorm(o, gdn_norm_gain) * jax.nn.silu(gh(og).astype(F32))
    y_d = o.reshape(bsz, l, GDN_VW)
    y = jnp.concatenate([y_c, y_d], axis=-1).astype(u.dtype)
    return y @ w_out


def conv_ffn(h, w_up, conv_w, conv_b, w_down):
    z = causal_dwconv(h @ w_up, conv_w) + conv_b
    gate, val = z[..., :D_FF], z[..., D_FF:]
    return (jax.nn.silu(gate) * val) @ w_down


def setup_inputs(seed: int = 0) -> dict:
    key = jax.random.key(seed)
    ks = iter(jax.random.split(key, 40))
    nrm = lambda shape, scale: jax.random.normal(next(ks), shape, F32) * scale
    dt = jnp.exp(jax.random.uniform(next(ks), (N_ODD, GDN_HEADS), F32, np.log(1e-3), np.log(1e-1)))
    return {
        'x': nrm((BATCH, SEQ, D_MODEL), 1.0),
        'meta_tokens': nrm((N_META, D_MODEL), 1.0),
        'norm_gains': 1.0 + nrm((DEPTH, 4, D_MODEL), 0.02),
        'w_in_even': nrm((N_EVEN, D_MODEL, EVEN_IN), D_MODEL ** -0.5),
        'w_out_even': nrm((N_EVEN, EVEN_MIX, D_MODEL), EVEN_MIX ** -0.5),
        'hg_lb_logits': nrm((DEPTH + 1, HG_KW), 0.5),
        'hg_norm_gain': 1.0 + nrm((N_EVEN, HG_VW), 0.02),
        'w_in_odd': nrm((N_ODD, D_MODEL, ODD_IN), D_MODEL ** -0.5),
        'w_out_odd': nrm((N_ODD, ODD_MIX, D_MODEL), ODD_MIX ** -0.5),
        'rw_mu': jax.random.uniform(next(ks), (N_ODD, RW_IN), F32),
        'rw_w0': -1.0 + nrm((N_ODD, RW_W), 1.0),
        'rw_w2': nrm((N_ODD, RW_DECAY_LORA, RW_W), 0.5 * RW_DECAY_LORA ** -0.5),
        'rw_a0': nrm((N_ODD, RW_W), 0.5),
        'rw_a2': nrm((N_ODD, RW_AAA_LORA, RW_W), 0.5 * RW_AAA_LORA ** -0.5),
        'rw_g2': nrm((N_ODD, RW_GATE_LORA, RW_W), RW_GATE_LORA ** -0.5),
        'rw_kk_scale': 0.85 + nrm((N_ODD, RW_W), 0.1),
        'rw_ka_scale': 1.0 + nrm((N_ODD, RW_W), 0.1),
        'rw_rk': nrm((N_ODD, RW_HEADS, RW_HD), 0.1),
        'rw_lnx_w': 1.0 + nrm((N_ODD, RW_W), 0.02),
        'rw_lnx_b': nrm((N_ODD, RW_W), 0.02),
        'gdn_conv_w': nrm((N_ODD, GDN_CONV, GDN_CONV_CH), GDN_CONV ** -0.5),
        'gdn_a_log': jnp.log(jax.random.uniform(next(ks), (N_ODD, GDN_HEADS), F32, 1.0, 16.0)),
        'gdn_dt_bias': dt + jnp.log(-jnp.expm1(-dt)),
        'gdn_norm_gain': 1.0 + nrm((N_ODD, GDN_DV), 0.02),
        'ffn_w_up': nrm((DEPTH, D_MODEL, 2 * D_FF), D_MODEL ** -0.5),
        'ffn_conv_w': nrm((DEPTH, FFN_CONV, 2 * D_FF), FFN_CONV ** -0.5),
        'ffn_conv_b': nrm((DEPTH, 2 * D_FF), 0.02),
        'ffn_w_down': nrm((DEPTH, D_FF, D_MODEL), D_FF ** -0.5),
    }


def reference(x, meta_tokens, norm_gains, w_in_even, w_out_even, hg_lb_logits, hg_norm_gain,
              w_in_odd, w_out_odd, rw_mu, rw_w0, rw_w2, rw_a0, rw_a2, rw_g2, rw_kk_scale, rw_ka_scale,
              rw_rk, rw_lnx_w, rw_lnx_b, gdn_conv_w, gdn_a_log, gdn_dt_bias, gdn_norm_gain,
              ffn_w_up, ffn_conv_w, ffn_conv_b, ffn_w_down):
    bsz = x.shape[0]
    meta = jnp.broadcast_to(meta_tokens[None].astype(x.dtype), (bsz, N_META, D_MODEL))
    h = jnp.concatenate([meta, x], axis=1)
    l = h.shape[1]
    pos = jnp.arange(l, dtype=jnp.int32)
    lb_all = jnp.cumsum(jax.nn.softmax(hg_lb_logits.astype(F32), axis=0), axis=0)
    for layer in range(DEPTH):
        g = norm_gains[layer]
        u = rmsnorm(h, g[0])
        i = layer // 2
        if layer % 2 == 0:
            m = even_mixer(u, w_in_even[i], w_out_even[i], lb_all[layer], hg_norm_gain[i], pos)
        else:
            m = odd_mixer(u, w_in_odd[i], w_out_odd[i], rw_mu[i], rw_w0[i], rw_w2[i], rw_a0[i], rw_a2[i],
                          rw_g2[i], rw_kk_scale[i], rw_ka_scale[i], rw_rk[i], rw_lnx_w[i], rw_lnx_b[i],
                          gdn_conv_w[i], gdn_a_log[i], gdn_dt_bias[i], gdn_norm_gain[i])
        h = h + rmsnorm(m, g[1])
        f = conv_ffn(rmsnorm(h, g[2]), ffn_w_up[layer], ffn_conv_w[layer], ffn_conv_b[layer], ffn_w_down[layer])
        h = h + rmsnorm(f, g[3])
    return h[:, N_META:]
```

```python
import functools
import math

import numpy as np
import jax
import jax.numpy as jnp
from jax import lax
from jax.experimental import pallas as pl
from jax.experimental.pallas import tpu as pltpu

F32 = jnp.float32
BF = jnp.bfloat16

D_MODEL = 1024
CHUNK = 64
N_META = 16
EPS = 1e-6
ROPE_BASE = 10000.0

RET_HEADS = D_MODEL // 256
RET_DK = 128
HG_HEADS = D_MODEL // 256
HG_DK = 128
RW_HEADS = D_MODEL // 128
RW_HD = 64
RW_DECAY_LORA = 64
RW_AAA_LORA = 64
RW_GATE_LORA = 128
RW_LNX_EPS = 64e-5
GDN_HEADS = D_MODEL // 256
GDN_DK = 128
GDN_CONV = 4
D_FF = 128 * ((8 * D_MODEL // 3 + 127) // 128)
FFN_CONV = 3

RET_W = RET_HEADS * RET_DK
HG_W = HG_HEADS * HG_DK
EVEN_IN = 4 * RET_W + 4 * HG_W
RW_W = RW_HEADS * RW_HD
RW_IN = 3 * RW_W + RW_DECAY_LORA + RW_AAA_LORA + RW_GATE_LORA
GDN_W = GDN_HEADS * GDN_DK
GDN_CONV_CH = 3 * GDN_W
GDN_IN = GDN_CONV_CH + GDN_W + 2 * GDN_HEADS
LANES = 128
GDN_IN_PAD = LANES * ((GDN_IN + LANES - 1) // LANES)
ODD_IN_PAD = RW_IN + GDN_IN_PAD
HG_LEVELS = (32, 16, 8, 4, 2, 1)
VMEM_LIMIT = 56 * 1024 * 1024
FFN_TILE = 256


def _dot(a, b):
    return jnp.dot(a, b, preferred_element_type=F32)


def _mm(a, b):
    return _dot(a.astype(BF), b.astype(BF))


def _mm_nt(a, b):
    return lax.dot_general(a.astype(BF), b.astype(BF), (((1,), (1,)), ((), ())), preferred_element_type=F32)


def _mm_tn(a, b):
    return lax.dot_general(a.astype(BF), b.astype(BF), (((0,), (0,)), ((), ())), preferred_element_type=F32)


def _split2(x):
    hi = x.astype(BF)
    lo = (x - hi.astype(F32)).astype(BF)
    return hi, lo


def _split3(x):
    hi = x.astype(BF)
    r1 = x - hi.astype(F32)
    mid = r1.astype(BF)
    lo = (r1 - mid.astype(F32)).astype(BF)
    return hi, mid, lo


def _cmm3(c, x):
    hi, mid, lo = _split3(x)
    return _dot(c, hi) + _dot(c, mid) + _dot(c, lo)


def _mmc2(x, c):
    hi, lo = _split2(x)
    return _dot(hi, c) + _dot(lo, c)


def _mm3(a, b):
    ah, al = _split2(a)
    bh, bl = _split2(b)
    return _dot(ah, bh) + _dot(ah, bl) + _dot(al, bh)


def _tri_inv(l):
    n = l.shape[0]
    eye = (lax.broadcasted_iota(jnp.int32, (n, n), 0) == lax.broadcasted_iota(jnp.int32, (n, n), 1)).astype(F32)
    p = eye + l
    lp = l
    span = 2
    while span < n:
        lp = _mm3(lp, lp)
        p = p + _mm3(lp, p)
        span *= 2
    return p


def _sigmoid(x):
    return 1.0 / (1.0 + jnp.exp(-x))


def _silu(x):
    return x * _sigmoid(x)


def _softplus(x):
    return jnp.maximum(x, 0.0) + jnp.log1p(jnp.exp(-jnp.abs(x)))


def _rms(x, g):
    return x * lax.rsqrt(jnp.mean(x * x, axis=-1, keepdims=True) + EPS) * g


def _norm_proj_body(h_ref, g_ref, w_ref, o_ref):
    u = _rms(h_ref[...], g_ref[...])
    o_ref[...] = _dot(u.astype(BF), w_ref[...])


def _norm_proj(h, gain, w, tm):
    b, lp, d = h.shape
    n = w.shape[1]
    return pl.pallas_call(
        _norm_proj_body,
        grid=(b, lp // tm),
        in_specs=[
            pl.BlockSpec((None, tm, d), lambda i, j: (i, j, 0)),
            pl.BlockSpec((1, d), lambda i, j: (0, 0)),
            pl.BlockSpec((d, n), lambda i, j: (0, 0), pipeline_mode=pl.Buffered(1)),
        ],
        out_specs=pl.BlockSpec((None, tm, n), lambda i, j: (i, j, 0)),
        out_shape=jax.ShapeDtypeStruct((b, lp, n), F32),
        compiler_params=pltpu.CompilerParams(
            dimension_semantics=("arbitrary", "arbitrary"), vmem_limit_bytes=VMEM_LIMIT),
        name="norm_proj",
    )(h, gain.reshape(1, d), w)


def _out_res_body(y_ref, w_ref, g_ref, h_ref, o_ref, *, tm, pad):
    m = _dot(y_ref[...].astype(BF), w_ref[...])
    t = pl.program_id(1) * tm + lax.broadcasted_iota(jnp.int32, m.shape, 0)
    o_ref[...] = jnp.where(t >= pad, h_ref[...] + _rms(m, g_ref[...]), 0.0)


def _out_res(y, w, gain, h, tm, pad):
    b, lp, d = h.shape
    k = y.shape[2]
    return pl.pallas_call(
        functools.partial(_out_res_body, tm=tm, pad=pad),
        grid=(b, lp // tm),
        in_specs=[
            pl.BlockSpec((None, tm, k), lambda i, j: (i, j, 0)),
            pl.BlockSpec((k, d), lambda i, j: (0, 0), pipeline_mode=pl.Buffered(1)),
            pl.BlockSpec((1, d), lambda i, j: (0, 0)),
            pl.BlockSpec((None, tm, d), lambda i, j: (i, j, 0)),
        ],
        out_specs=pl.BlockSpec((None, tm, d), lambda i, j: (i, j, 0)),
        out_shape=jax.ShapeDtypeStruct((b, lp, d), F32),
        compiler_params=pltpu.CompilerParams(
            dimension_semantics=("arbitrary", "arbitrary"), vmem_limit_bytes=VMEM_LIMIT),
        name="out_res",
    )(y, w, gain.reshape(1, d), h)


def _shift_rows(a, carry_ref, cols, n_back, row):
    out = []
    for s in range(1, n_back + 1):
        sh = pltpu.roll(a, s, 0)
        for r in range(s):
            sh = jnp.where(row == r, carry_ref[8 - s + r:9 - s + r, cols], sh)
        out.append(sh)
    return out


def _ffn_body(h_ref, g2_ref, wup_ref, cw_ref, cb_ref, wdn_ref, g3_ref, o_ref, carry_ref, *, tm, pad):
    @pl.when(pl.program_id(1) == 0)
    def _():
        carry_ref[...] = jnp.zeros_like(carry_ref)

    x = h_ref[...]
    u = _rms(x, g2_ref[...]).astype(BF)
    row = lax.broadcasted_iota(jnp.int32, (tm, FFN_TILE), 0)
    acc = jnp.zeros((tm, D_MODEL), F32)
    for c in range(D_FF // FFN_TILE):
        z = []
        for part in range(2):
            cols = slice(part * D_FF + c * FFN_TILE, part * D_FF + (c + 1) * FFN_TILE)
            a = _dot(u, wup_ref[:, cols])
            am1, am2 = _shift_rows(a, carry_ref, cols, FFN_CONV - 1, row)
            carry_ref[:, cols] = a[tm - 8:tm, :]
            z.append(am2 * cw_ref[0:1, cols] + am1 * cw_ref[1:2, cols] + a * cw_ref[2:3, cols] + cb_ref[:, cols])
        act = _silu(z[0]) * z[1]
        acc = acc + _dot(act.astype(BF), wdn_ref[c * FFN_TILE:(c + 1) * FFN_TILE, :])
    t = pl.program_id(1) * tm + lax.broadcasted_iota(jnp.int32, acc.shape, 0)
    o_ref[...] = jnp.where(t >= pad, x + _rms(acc, g3_ref[...]), 0.0)


def _ffn(h, g2, w_up, conv_w, conv_b, w_down, g3, tm, pad):
    b, lp, d = h.shape
    ff2 = w_up.shape[1]
    const = lambda i, j: (0, 0)
    return pl.pallas_call(
        functools.partial(_ffn_body, tm=tm, pad=pad),
        grid=(b, lp // tm),
        in_specs=[
            pl.BlockSpec((None, tm, d), lambda i, j: (i, j, 0)),
            pl.BlockSpec((1, d), const),
            pl.BlockSpec((d, ff2), const, pipeline_mode=pl.Buffered(1)),
            pl.BlockSpec((FFN_CONV, ff2), const),
            pl.BlockSpec((1, ff2), const),
            pl.BlockSpec((ff2 // 2, d), const, pipeline_mode=pl.Buffered(1)),
            pl.BlockSpec((1, d), const),
        ],
        out_specs=pl.BlockSpec((None, tm, d), lambda i, j: (i, j, 0)),
        out_shape=jax.ShapeDtypeStruct((b, lp, d), F32),
        scratch_shapes=[pltpu.VMEM((8, ff2), F32)],
        compiler_params=pltpu.CompilerParams(
            dimension_semantics=("arbitrary", "arbitrary"), vmem_limit_bytes=VMEM_LIMIT),
        name="conv_ffn",
    )(h, g2.reshape(1, d), w_up, conv_w, conv_b.reshape(1, ff2), w_down, g3.reshape(1, d))


def _hg_consts():
    c = CHUNK
    t = np.arange(c)
    m = np.zeros((2 + len(HG_LEVELS), c, c), np.float32)
    m[0] = t[None, :] <= t[:, None]
    m[1] = t[None, :] > t[:, None]
    masks = np.zeros((len(HG_LEVELS) + 1, c, c), np.float32)
    ii, jj = t[:, None], t[None, :]
    for li, s in enumerate(HG_LEVELS):
        for i in range(c):
            mid = (i // (2 * s)) * (2 * s) + s - 1
            if i % (2 * s) >= s:
                m[2 + li, i, mid + 1:i + 1] = 1.0
            else:
                m[2 + li, i, i + 1:mid + 1] = 1.0
        masks[li] = (ii // (2 * s) == jj // (2 * s)) & (ii % (2 * s) >= s) & (jj % (2 * s) < s)
    masks[-1] = ii == jj
    return m.reshape(-1, c), masks


def _even_body(p_ref, cos_ref, sin_ref, lbl_ref, hgain_ref, mall_ref, masks_ref, y_ref, sret_ref, shg_ref, *,
               layer):
    @pl.when(pl.program_id(1) == 0)
    def _():
        sret_ref[...] = jnp.zeros_like(sret_ref)
        shg_ref[...] = jnp.zeros_like(shg_ref)

    c = CHUNK
    cosv = cos_ref[...]
    sinv = sin_ref[...]
    ii = lax.broadcasted_iota(jnp.int32, (c, c), 0)
    jj = lax.broadcasted_iota(jnp.int32, (c, c), 1)
    diff = (ii - jj).astype(F32)
    rowi = lax.broadcasted_iota(jnp.int32, (c, RET_DK), 0).astype(F32)

    for h in range(RET_HEADS):
        lg = math.log1p(-(2.0 ** (-5.0 - h)))
        q = p_ref[:, h * 128:(h + 1) * 128]
        k = p_ref[:, RET_W + h * 128:RET_W + (h + 1) * 128]
        v = p_ref[:, 2 * RET_W + h * 128:2 * RET_W + (h + 1) * 128]
        g = p_ref[:, 3 * RET_W + h * 128:3 * RET_W + (h + 1) * 128]
        q = q * cosv + pltpu.roll(q, RET_DK // 2, 1) * sinv
        k = (k * cosv + pltpu.roll(k, RET_DK // 2, 1) * sinv) * (RET_DK ** -0.5)
        intra = jnp.where(diff >= 0, jnp.exp(lg * jnp.maximum(diff, 0.0)), 0.0)
        q_dec = jnp.exp(lg * (rowi + 1.0))
        k_dec = jnp.exp(lg * (c - 1.0 - rowi))
        s = sret_ref[h]
        att = _mm_nt(q, k) * intra
        o = _mm(att, v) + _mm(q * q_dec, s)
        sret_ref[h] = s * math.exp(lg * c) + _mm_tn(k * k_dec, v)
        xc = o - jnp.mean(o, axis=-1, keepdims=True)
        oa = xc * lax.rsqrt(jnp.mean(xc * xc, axis=-1, keepdims=True) + EPS)
        y_ref[:, h * 128:(h + 1) * 128] = oa * _silu(g)

    lgt = lbl_ref[...]
    mx = jnp.max(lgt, axis=0, keepdims=True)
    ex = jnp.exp(lgt - mx)
    sm = ex / jnp.sum(ex, axis=0, keepdims=True)
    lb_all = jnp.sum(sm[0:layer + 1, :], axis=0, keepdims=True)
    base = 4 * RET_W
    mall = mall_ref[...]
    for h in range(HG_HEADS):
        lb = lb_all[:, h * 128:(h + 1) * 128]
        q = p_ref[:, base + h * 128:base + (h + 1) * 128]
        fb = p_ref[:, base + HG_W + h * 128:base + HG_W + (h + 1) * 128]
        v = p_ref[:, base + 2 * HG_W + h * 128:base + 2 * HG_W + (h + 1) * 128]
        g = p_ref[:, base + 3 * HG_W + h * 128:base + 3 * HG_W + (h + 1) * 128]
        lf = jnp.log(lb + (1.0 - lb) * _sigmoid(fb))
        k = (1.0 - lb) * _sigmoid(-fb)
        x = jnp.exp(_cmm3(mall, lf))
        att = masks_ref[len(HG_LEVELS)] * _mm_nt(q, k)
        for li in range(len(HG_LEVELS)):
            xs = x[(2 + li) * c:(3 + li) * c, :]
            att = att + masks_ref[li] * _mm_nt(q * xs, k * xs)
        st = shg_ref[h]
        o = _mm(att, v) + _mm_nt(q * x[0:c, :], st)
        shg_ref[h] = st * x[c - 1:c, :] + _mm_tn(v, k * x[c:2 * c, :])
        ob = _rms(o, hgain_ref[:, h * 128:(h + 1) * 128])
        y_ref[:, RET_W + h * 128:RET_W + (h + 1) * 128] = ob * _silu(g)


def _even_mixer(p, cos_t, sin_t, lb_logits, hg_gain, layer):
    b, lp, n = p.shape
    nc = lp // CHUNK
    mall, masks = _hg_consts()
    const2 = lambda i, j: (0, 0)
    return pl.pallas_call(
        functools.partial(_even_body, layer=layer),
        grid=(b, nc),
        in_specs=[
            pl.BlockSpec((None, CHUNK, n), lambda i, j: (i, j, 0)),
            pl.BlockSpec((CHUNK, RET_DK), lambda i, j: (j, 0)),
            pl.BlockSpec((CHUNK, RET_DK), lambda i, j: (j, 0)),
            pl.BlockSpec(lb_logits.shape, const2),
            pl.BlockSpec((1, HG_W), const2),
            pl.BlockSpec(mall.shape, const2),
            pl.BlockSpec(masks.shape, lambda i, j: (0, 0, 0)),
        ],
        out_specs=pl.BlockSpec((None, CHUNK, RET_W + HG_W), lambda i, j: (i, j, 0)),
        out_shape=jax.ShapeDtypeStruct((b, lp, RET_W + HG_W), F32),
        scratch_shapes=[pltpu.VMEM((RET_HEADS, RET_DK, RET_DK), F32), pltpu.VMEM((HG_HEADS, HG_DK, HG_DK), F32)],
        compiler_params=pltpu.CompilerParams(
            dimension_semantics=("arbitrary", "arbitrary"), vmem_limit_bytes=VMEM_LIMIT),
        name="even_mixer",
    )(p, cos_t, sin_t, lb_logits, hg_gain.reshape(1, HG_W), jnp.asarray(mall, BF), jnp.asarray(masks))


def _odd_body(p_ref, mu_ref, w0_ref, a0_ref, wa2_ref, g2_ref, kks_ref, kas_ref, rk_ref, lnw_ref, lnb_ref,
              cw_ref, alog_ref, dtb_ref, ggain_ref, tril_ref, bones_ref, y_ref,
              hrw_ref, sgd_ref, cpc_ref, ccv_ref):
    @pl.when(pl.program_id(1) == 0)
    def _():
        hrw_ref[...] = jnp.zeros_like(hrw_ref)
        sgd_ref[...] = jnp.zeros_like(sgd_ref)
        cpc_ref[...] = jnp.zeros_like(cpc_ref)
        ccv_ref[...] = jnp.zeros_like(ccv_ref)

    c = CHUNK
    tril = tril_ref[...]
    bones = bones_ref[...]
    ii = lax.broadcasted_iota(jnp.int32, (c, c), 0)
    jj = lax.broadcasted_iota(jnp.int32, (c, c), 1)
    strict = ii > jj
    incl = ii >= jj

    def seg_sum(x):
        return jnp.concatenate(
            [_mmc2(x[:, p * 128:(p + 1) * 128], bones) for p in range(x.shape[1] // 128)], axis=1)

    pc = p_ref[:, 0:RW_IN]
    row = lax.broadcasted_iota(jnp.int32, (c, RW_IN), 0)
    prev = jnp.where(row == 0, cpc_ref[7:8, :], pltpu.roll(pc, 1, 0))
    cpc_ref[...] = pc[c - 8:c, :]
    pcs = pc + (prev - pc) * mu_ref[...]
    r = pcs[:, 0:RW_W]
    k = pcs[:, RW_W:2 * RW_W]
    v = pcs[:, 2 * RW_W:3 * RW_W]
    lo = pcs[:, 3 * RW_W:3 * RW_W + 128]
    glo = pcs[:, 3 * RW_W + 128:RW_IN]
    lane = lax.broadcasted_iota(jnp.int32, (c, 128), 1)
    wa = _mm(jnp.where(lane < RW_DECAY_LORA, jnp.tanh(lo), lo), wa2_ref[...])
    log_w = -_softplus(-(w0_ref[...] + wa[:, 0:RW_W])) - 0.5
    lw = -jnp.exp(log_w)
    a = _sigmoid(a0_ref[...] + wa[:, RW_W:2 * RW_W])
    gate = _mm(_sigmoid(glo), g2_ref[...])
    kks = k * kks_ref[...]
    kkn = kks * lax.rsqrt(seg_sum(kks * kks) + EPS)
    k2 = k * (1.0 + (a - 1.0) * kas_ref[...])
    alpha = -kkn
    beta = kkn * a
    c_inc = _cmm3(tril, lw)
    c_last = c_inc[c - 1:c, :]
    e_inc = jnp.exp(c_inc)
    e_neg = jnp.exp(-c_inc)
    e_rest = jnp.exp(c_last - c_inc)
    ah = alpha * jnp.exp(c_inc - lw)
    rh = r * e_inc
    bt = beta * e_neg
    kt = k2 * e_neg
    bg = beta * e_rest
    kg = k2 * e_rest
    gc = jnp.exp(c_last)
    ys = []
    for h in range(RW_HEADS):
        sl = slice(h * RW_HD, (h + 1) * RW_HD)
        ht = hrw_ref[h]
        ah_h, rh_h, bt_h, kt_h, v_h = ah[:, sl], rh[:, sl], bt[:, sl], kt[:, sl], v[:, sl]
        l_ab = jnp.where(strict, _mm_nt(ah_h, bt_h), 0.0)
        l_ak = jnp.where(strict, _mm_nt(ah_h, kt_h), 0.0)
        u = _mm3(_tri_inv(l_ab), _mm_nt(ah_h, ht) + _mm(l_ak, v_h))
        a_rb = jnp.where(incl, _mm_nt(rh_h, bt_h), 0.0)
        a_rk = jnp.where(incl, _mm_nt(rh_h, kt_h), 0.0)
        ys.append(_mm_nt(rh_h, ht) + _mm(a_rb, u) + _mm(a_rk, v_h))
        hrw_ref[h] = ht * gc[:, sl] + _mm_tn(u, bg[:, sl]) + _mm_tn(v_h, kg[:, sl])
    y = jnp.concatenate(ys, axis=1)
    inv_hd = 1.0 / RW_HD
    yc = y - seg_sum(y) * inv_hd
    yn = yc * lax.rsqrt(seg_sum(yc * yc) * inv_hd + RW_LNX_EPS) * lnw_ref[...] + lnb_ref[...]
    y_ref[:, 0:RW_W] = (yn + seg_sum(r * k2 * rk_ref[...]) * v) * gate

    base = RW_IN
    x = p_ref[:, base:base + GDN_CONV_CH]
    rowc = lax.broadcasted_iota(jnp.int32, (c, GDN_CONV_CH), 0)
    xm1, xm2, xm3 = _shift_rows(x, ccv_ref, slice(None), GDN_CONV - 1, rowc)
    ccv_ref[...] = x[c - 8:c, :]
    qkv = _silu(xm3 * cw_ref[0:1, :] + xm2 * cw_ref[1:2, :] + xm1 * cw_ref[2:3, :] + x * cw_ref[3:4, :])
    sc = p_ref[:, base + GDN_CONV_CH + GDN_W:base + GDN_IN_PAD]
    g_all = -jnp.exp(alog_ref[...]) * _softplus(sc + dtb_ref[...])
    b_all = _sigmoid(sc)
    for h in range(GDN_HEADS):
        q = qkv[:, h * 128:(h + 1) * 128]
        kd = qkv[:, GDN_W + h * 128:GDN_W + (h + 1) * 128]
        vd = qkv[:, 2 * GDN_W + h * 128:2 * GDN_W + (h + 1) * 128]
        og = p_ref[:, base + GDN_CONV_CH + h * 128:base + GDN_CONV_CH + (h + 1) * 128]
        q = q * lax.rsqrt(jnp.sum(q * q, axis=-1, keepdims=True) + EPS) * (GDN_DK ** -0.5)
        kd = kd * lax.rsqrt(jnp.sum(kd * kd, axis=-1, keepdims=True) + EPS)
        g_b = jnp.broadcast_to(g_all[:, h:h + 1], (c, 128))
        b_b = jnp.broadcast_to(b_all[:, GDN_HEADS + h:GDN_HEADS + h + 1], (c, 128))
        cg = _cmm3(tril, g_b)
        dm = _cmm3(tril, jnp.where(strict, g_b[:, 0:c], 0.0))
        decay = jnp.where(incl, jnp.exp(dm), 0.0)
        l_m = -jnp.where(strict, b_b[:, 0:c] * _mm_nt(kd, kd) * decay, 0.0)
        eg = jnp.exp(cg)
        sol = _mm3(_tri_inv(l_m), jnp.concatenate([b_b * vd, b_b * kd * eg], axis=1))
        u = sol[:, 0:128]
        w = sol[:, 128:256]
        qk = _mm_nt(q, kd) * decay
        s = sgd_ref[h]
        v_new = u - _mm(w, s)
        o = _mm(q * eg, s) + _mm(qk, v_new)
        cl = cg[c - 1:c, :]
        sgd_ref[h] = s * jnp.exp(cl) + _mm_tn(kd * jnp.exp(cl - cg), v_new)
        y_ref[:, RW_W + h * 128:RW_W + (h + 1) * 128] = _rms(o, ggain_ref[...]) * _silu(og)


def _odd_mixer(p, mu, w0, a0, wa2, g2, kks, kas, rk, lnw, lnb, conv_w, alog, dtb, ggain):
    b, lp, n = p.shape
    nc = lp // CHUNK
    t = np.arange(CHUNK)
    tril = jnp.asarray(t[None, :] <= t[:, None], BF)
    l = np.arange(128)
    bones = jnp.asarray(l[:, None] // RW_HD == l[None, :] // RW_HD, BF)
    row = lambda a: a.reshape(1, -1)
    consts = [row(mu), row(w0), row(a0), wa2, g2, row(kks), row(kas), row(rk), row(lnw), row(lnb),
              conv_w, row(alog), row(dtb), row(ggain), tril, bones]
    const2 = lambda i, j: (0, 0)
    return pl.pallas_call(
        _odd_body,
        grid=(b, nc),
        in_specs=[pl.BlockSpec((None, CHUNK, n), lambda i, j: (i, j, 0))]
        + [pl.BlockSpec(a.shape, const2) for a in consts],
        out_specs=pl.BlockSpec((None, CHUNK, RW_W + GDN_W), lambda i, j: (i, j, 0)),
        out_shape=jax.ShapeDtypeStruct((b, lp, RW_W + GDN_W), F32),
        scratch_shapes=[
            pltpu.VMEM((RW_HEADS, RW_HD, RW_HD), F32),
            pltpu.VMEM((GDN_HEADS, GDN_DK, GDN_DK), F32),
            pltpu.VMEM((8, RW_IN), F32),
            pltpu.VMEM((8, GDN_CONV_CH), F32),
        ],
        compiler_params=pltpu.CompilerParams(
            dimension_semantics=("arbitrary", "arbitrary"), vmem_limit_bytes=VMEM_LIMIT),
        name="odd_mixer",
    )(p, *consts)


def _row_tile(lp):
    best = 8
    for tm in range(8, min(lp, 704) + 1, 8):
        if lp % tm == 0:
            best = tm
    return best


def kernel(x, meta_tokens, norm_gains, w_in_even, w_out_even, hg_lb_logits, hg_norm_gain, w_in_odd, w_out_odd, rw_mu, rw_w0, rw_w2, rw_a0, rw_a2, rw_g2, rw_kk_scale, rw_ka_scale, rw_rk, rw_lnx_w, rw_lnx_b, gdn_conv_w, gdn_a_log, gdn_dt_bias, gdn_norm_gain, ffn_w_up, ffn_conv_w, ffn_conv_b, ffn_w_down):
    bsz, seq, d = x.shape
    depth = norm_gains.shape[0]
    l = N_META + seq
    pad = (-l) % CHUNK
    lp = l + pad
    tm = _row_tile(lp)
    meta = jnp.broadcast_to(meta_tokens[None].astype(x.dtype), (bsz, N_META, d))
    h = jnp.concatenate([jnp.zeros((bsz, pad, d), x.dtype), meta, x], axis=1)

    half = RET_DK // 2
    pos = (jnp.arange(lp, dtype=jnp.int32) - pad).astype(F32)
    inv = ROPE_BASE ** (-jnp.arange(half, dtype=F32) / half)
    ang = pos[:, None] * inv[None, :]
    cos_t = jnp.concatenate([jnp.cos(ang), jnp.cos(ang)], axis=1)
    sin_t = jnp.concatenate([-jnp.sin(ang), jnp.sin(ang)], axis=1)

    for layer in range(depth):
        g = norm_gains[layer]
        i = layer // 2
        if layer % 2 == 0:
            p = _norm_proj(h, g[0], w_in_even[i].astype(BF), tm)
            y = _even_mixer(p, cos_t, sin_t, hg_lb_logits, hg_norm_gain[i], layer)
            w_out = w_out_even[i]
        else:
            w_in = jnp.pad(w_in_odd[i], ((0, 0), (0, ODD_IN_PAD - w_in_odd.shape[2])))
            p = _norm_proj(h, g[0], w_in.astype(BF), tm)
            wa2 = jnp.zeros((RW_DECAY_LORA + RW_AAA_LORA, 2 * RW_W), F32)
            wa2 = wa2.at[:RW_DECAY_LORA, :RW_W].set(rw_w2[i]).at[RW_DECAY_LORA:, RW_W:].set(rw_a2[i])
            lane_pad = lambda a: jnp.pad(a, (0, LANES - a.shape[0]))
            y = _odd_mixer(p, rw_mu[i], rw_w0[i], rw_a0[i], wa2.astype(BF), rw_g2[i].astype(BF),
                           rw_kk_scale[i], rw_ka_scale[i], rw_rk[i].reshape(-1), rw_lnx_w[i], rw_lnx_b[i],
                           gdn_conv_w[i], lane_pad(gdn_a_log[i]), lane_pad(gdn_dt_bias[i]), gdn_norm_gain[i])
            w_out = w_out_odd[i]
        h = _out_res(y, w_out.astype(BF), g[1], h, tm, pad)
        h = _ffn(h, g[2], ffn_w_up[layer].astype(BF), ffn_conv_w[layer], ffn_conv_b[layer],
                 ffn_w_down[layer].astype(BF), g[3], tm, pad)
    return h[:, pad + N_META:]
```

```python
import functools
import math

import numpy as np
import jax
import jax.numpy as jnp
from jax import lax
from jax.experimental import pallas as pl
from jax.experimental.pallas import tpu as pltpu

F32 = jnp.float32
BF = jnp.bfloat16

D_MODEL = 1024
CHUNK = 64
N_META = 16
EPS = 1e-6
ROPE_BASE = 10000.0

RET_HEADS = D_MODEL // 256
RET_DK = 128
HG_HEADS = D_MODEL // 256
HG_DK = 128
RW_HEADS = D_MODEL // 128
RW_HD = 64
RW_DECAY_LORA = 64
RW_AAA_LORA = 64
RW_GATE_LORA = 128
RW_LNX_EPS = 64e-5
GDN_HEADS = D_MODEL // 256
GDN_DK = 128
GDN_CONV = 4
D_FF = 128 * ((8 * D_MODEL // 3 + 127) // 128)
FFN_CONV = 3

RET_W = RET_HEADS * RET_DK
HG_W = HG_HEADS * HG_DK
EVEN_IN = 4 * RET_W + 4 * HG_W
RW_W = RW_HEADS * RW_HD
RW_IN = 3 * RW_W + RW_DECAY_LORA + RW_AAA_LORA + RW_GATE_LORA
GDN_W = GDN_HEADS * GDN_DK
GDN_CONV_CH = 3 * GDN_W
GDN_IN = GDN_CONV_CH + GDN_W + 2 * GDN_HEADS
LANES = 128
GDN_IN_PAD = LANES * ((GDN_IN + LANES - 1) // LANES)
ODD_IN_PAD = RW_IN + GDN_IN_PAD
HG_LEVELS = (32, 16, 8, 4, 2, 1)
VMEM_LIMIT = 56 * 1024 * 1024
FFN_TILE = 256


def _dot(a, b):
    return jnp.dot(a, b, preferred_element_type=F32)


def _mm(a, b):
    return _dot(a.astype(BF), b.astype(BF))


def _mm_nt(a, b):
    return lax.dot_general(a.astype(BF), b.astype(BF), (((1,), (1,)), ((), ())), preferred_element_type=F32)


def _mm_tn(a, b):
    return lax.dot_general(a.astype(BF), b.astype(BF), (((0,), (0,)), ((), ())), preferred_element_type=F32)


def _split2(x):
    hi = x.astype(BF)
    lo = (x - hi.astype(F32)).astype(BF)
    return hi, lo


def _split3(x):
    hi = x.astype(BF)
    r1 = x - hi.astype(F32)
    mid = r1.astype(BF)
    lo = (r1 - mid.astype(F32)).astype(BF)
    return hi, mid, lo


def _cmm3(c, x):
    hi, mid, lo = _split3(x)
    return _dot(c, hi) + _dot(c, mid) + _dot(c, lo)


def _mmc2(x, c):
    hi, lo = _split2(x)
    return _dot(hi, c) + _dot(lo, c)


def _tri_inv_many(ls):
    n = ls[0].shape[0]
    eye = (lax.broadcasted_iota(jnp.int32, (n, n), 0) == lax.broadcasted_iota(jnp.int32, (n, n), 1)).astype(F32)
    ps = [eye + l for l in ls]
    lps = list(ls)
    span = 2
    while span < n:
        lps = [_mm(lp, lp) for lp in lps]
        ps = [p + _mm(lp, p) for lp, p in zip(lps, ps)]
        span *= 2
    return ps


def _sigmoid(x):
    return 1.0 / (1.0 + jnp.exp(-x))


def _silu(x):
    return x * _sigmoid(x)


def _softplus(x):
    return jnp.maximum(x, 0.0) + jnp.log1p(jnp.exp(-jnp.abs(x)))


def _rms(x, g):
    return x * lax.rsqrt(jnp.mean(x * x, axis=-1, keepdims=True) + EPS) * g


def _norm_proj_body(h_ref, g_ref, w_ref, o_ref):
    u = _rms(h_ref[...], g_ref[...])
    o_ref[...] = _dot(u.astype(BF), w_ref[...])


def _norm_proj(h, gain, w, tm):
    b, lp, d = h.shape
    n = w.shape[1]
    return pl.pallas_call(
        _norm_proj_body,
        grid=(b, lp // tm),
        in_specs=[
            pl.BlockSpec((None, tm, d), lambda i, j: (i, j, 0)),
            pl.BlockSpec((1, d), lambda i, j: (0, 0)),
            pl.BlockSpec((d, n), lambda i, j: (0, 0), pipeline_mode=pl.Buffered(1)),
        ],
        out_specs=pl.BlockSpec((None, tm, n), lambda i, j: (i, j, 0)),
        out_shape=jax.ShapeDtypeStruct((b, lp, n), F32),
        compiler_params=pltpu.CompilerParams(
            dimension_semantics=("arbitrary", "arbitrary"), vmem_limit_bytes=VMEM_LIMIT),
        name="norm_proj",
    )(h, gain.reshape(1, d), w)


def _out_res_body(y_ref, w_ref, g_ref, h_ref, o_ref, *, tm, pad):
    m = _dot(y_ref[...].astype(BF), w_ref[...])
    t = pl.program_id(1) * tm + lax.broadcasted_iota(jnp.int32, m.shape, 0)
    o_ref[...] = jnp.where(t >= pad, h_ref[...] + _rms(m, g_ref[...]), 0.0)


def _out_res(y, w, gain, h, tm, pad):
    b, lp, d = h.shape
    k = y.shape[2]
    return pl.pallas_call(
        functools.partial(_out_res_body, tm=tm, pad=pad),
        grid=(b, lp // tm),
        in_specs=[
            pl.BlockSpec((None, tm, k), lambda i, j: (i, j, 0)),
            pl.BlockSpec((k, d), lambda i, j: (0, 0), pipeline_mode=pl.Buffered(1)),
            pl.BlockSpec((1, d), lambda i, j: (0, 0)),
            pl.BlockSpec((None, tm, d), lambda i, j: (i, j, 0)),
        ],
        out_specs=pl.BlockSpec((None, tm, d), lambda i, j: (i, j, 0)),
        out_shape=jax.ShapeDtypeStruct((b, lp, d), F32),
        compiler_params=pltpu.CompilerParams(
            dimension_semantics=("arbitrary", "arbitrary"), vmem_limit_bytes=VMEM_LIMIT),
        name="out_res",
    )(y, w, gain.reshape(1, d), h)


def _shift_rows(a, carry_ref, cols, n_back, row):
    out = []
    for s in range(1, n_back + 1):
        sh = pltpu.roll(a, s, 0)
        for r in range(s):
            sh = jnp.where(row == r, carry_ref[8 - s + r:9 - s + r, cols], sh)
        out.append(sh)
    return out


def _ffn_body(h_ref, g2_ref, wup_ref, cw_ref, cb_ref, wdn_ref, g3_ref, o_ref, carry_ref, *, tm, pad):
    @pl.when(pl.program_id(1) == 0)
    def _():
        carry_ref[...] = jnp.zeros_like(carry_ref)

    x = h_ref[...]
    u = _rms(x, g2_ref[...]).astype(BF)
    row = lax.broadcasted_iota(jnp.int32, (tm, FFN_TILE), 0)
    acc = jnp.zeros((tm, D_MODEL), F32)
    for c in range(D_FF // FFN_TILE):
        z = []
        for part in range(2):
            cols = slice(part * D_FF + c * FFN_TILE, part * D_FF + (c + 1) * FFN_TILE)
            a = _dot(u, wup_ref[:, cols])
            am1, am2 = _shift_rows(a, carry_ref, cols, FFN_CONV - 1, row)
            carry_ref[:, cols] = a[tm - 8:tm, :]
            z.append(am2 * cw_ref[0:1, cols] + am1 * cw_ref[1:2, cols] + a * cw_ref[2:3, cols] + cb_ref[:, cols])
        act = _silu(z[0]) * z[1]
        acc = acc + _dot(act.astype(BF), wdn_ref[c * FFN_TILE:(c + 1) * FFN_TILE, :])
    t = pl.program_id(1) * tm + lax.broadcasted_iota(jnp.int32, acc.shape, 0)
    o_ref[...] = jnp.where(t >= pad, x + _rms(acc, g3_ref[...]), 0.0)


def _ffn(h, g2, w_up, conv_w, conv_b, w_down, g3, tm, pad):
    b, lp, d = h.shape
    ff2 = w_up.shape[1]
    const = lambda i, j: (0, 0)
    return pl.pallas_call(
        functools.partial(_ffn_body, tm=tm, pad=pad),
        grid=(b, lp // tm),
        in_specs=[
            pl.BlockSpec((None, tm, d), lambda i, j: (i, j, 0)),
            pl.BlockSpec((1, d), const),
            pl.BlockSpec((d, ff2), const, pipeline_mode=pl.Buffered(1)),
            pl.BlockSpec((FFN_CONV, ff2), const),
            pl.BlockSpec((1, ff2), const),
            pl.BlockSpec((ff2 // 2, d), const, pipeline_mode=pl.Buffered(1)),
            pl.BlockSpec((1, d), const),
        ],
        out_specs=pl.BlockSpec((None, tm, d), lambda i, j: (i, j, 0)),
        out_shape=jax.ShapeDtypeStruct((b, lp, d), F32),
        scratch_shapes=[pltpu.VMEM((8, ff2), F32)],
        compiler_params=pltpu.CompilerParams(
            dimension_semantics=("arbitrary", "arbitrary"), vmem_limit_bytes=VMEM_LIMIT),
        name="conv_ffn",
    )(h, g2.reshape(1, d), w_up, conv_w, conv_b.reshape(1, ff2), w_down, g3.reshape(1, d))


def _hg_consts():
    c = CHUNK
    t = np.arange(c)
    m = np.zeros((2 + len(HG_LEVELS), c, c), np.float32)
    m[0] = t[None, :] <= t[:, None]
    m[1] = t[None, :] > t[:, None]
    masks = np.zeros((len(HG_LEVELS) + 1, c, c), np.float32)
    ii, jj = t[:, None], t[None, :]
    for li, s in enumerate(HG_LEVELS):
        for i in range(c):
            mid = (i // (2 * s)) * (2 * s) + s - 1
            if i % (2 * s) >= s:
                m[2 + li, i, mid + 1:i + 1] = 1.0
            else:
                m[2 + li, i, i + 1:mid + 1] = 1.0
        masks[li] = (ii // (2 * s) == jj // (2 * s)) & (ii % (2 * s) >= s) & (jj % (2 * s) < s)
    masks[-1] = ii == jj
    return m.reshape(-1, c), masks


def _even_body(p_ref, cos_ref, sin_ref, lbl_ref, hgain_ref, mall_ref, masks_ref, y_ref, sret_ref, shg_ref, *,
               layer):
    @pl.when(pl.program_id(1) == 0)
    def _():
        sret_ref[...] = jnp.zeros_like(sret_ref)
        shg_ref[...] = jnp.zeros_like(shg_ref)

    c = CHUNK
    cosv = cos_ref[...]
    sinv = sin_ref[...]
    ii = lax.broadcasted_iota(jnp.int32, (c, c), 0)
    jj = lax.broadcasted_iota(jnp.int32, (c, c), 1)
    diff = (ii - jj).astype(F32)
    rowi = lax.broadcasted_iota(jnp.int32, (c, RET_DK), 0).astype(F32)

    for h in range(RET_HEADS):
        lg = math.log1p(-(2.0 ** (-5.0 - h)))
        q = p_ref[:, h * 128:(h + 1) * 128]
        k = p_ref[:, RET_W + h * 128:RET_W + (h + 1) * 128]
        v = p_ref[:, 2 * RET_W + h * 128:2 * RET_W + (h + 1) * 128]
        g = p_ref[:, 3 * RET_W + h * 128:3 * RET_W + (h + 1) * 128]
        q = q * cosv + pltpu.roll(q, RET_DK // 2, 1) * sinv
        k = (k * cosv + pltpu.roll(k, RET_DK // 2, 1) * sinv) * (RET_DK ** -0.5)
        intra = jnp.where(diff >= 0, jnp.exp(lg * jnp.maximum(diff, 0.0)), 0.0)
        q_dec = jnp.exp(lg * (rowi + 1.0))
        k_dec = jnp.exp(lg * (c - 1.0 - rowi))
        s = sret_ref[h]
        att = _mm_nt(q, k) * intra
        o = _mm(att, v) + _mm(q * q_dec, s)
        sret_ref[h] = s * math.exp(lg * c) + _mm_tn(k * k_dec, v)
        xc = o - jnp.mean(o, axis=-1, keepdims=True)
        oa = xc * lax.rsqrt(jnp.mean(xc * xc, axis=-1, keepdims=True) + EPS)
        y_ref[:, h * 128:(h + 1) * 128] = oa * _silu(g)

    lgt = lbl_ref[...]
    mx = jnp.max(lgt, axis=0, keepdims=True)
    ex = jnp.exp(lgt - mx)
    sm = ex / jnp.sum(ex, axis=0, keepdims=True)
    lb_all = jnp.sum(sm[0:layer + 1, :], axis=0, keepdims=True)
    base = 4 * RET_W
    mall = mall_ref[...]
    for h in range(HG_HEADS):
        lb = lb_all[:, h * 128:(h + 1) * 128]
        q = p_ref[:, base + h * 128:base + (h + 1) * 128]
        fb = p_ref[:, base + HG_W + h * 128:base + HG_W + (h + 1) * 128]
        v = p_ref[:, base + 2 * HG_W + h * 128:base + 2 * HG_W + (h + 1) * 128]
        g = p_ref[:, base + 3 * HG_W + h * 128:base + 3 * HG_W + (h + 1) * 128]
        lf = jnp.log(lb + (1.0 - lb) * _sigmoid(fb))
        k = (1.0 - lb) * _sigmoid(-fb)
        x = jnp.exp(_cmm3(mall, lf))
        att = masks_ref[len(HG_LEVELS)] * _mm_nt(q, k)
        for li in range(len(HG_LEVELS)):
            xs = x[(2 + li) * c:(3 + li) * c, :]
            att = att + masks_ref[li] * _mm_nt(q * xs, k * xs)
        st = shg_ref[h]
        o = _mm(att, v) + _mm_nt(q * x[0:c, :], st)
        shg_ref[h] = st * x[c - 1:c, :] + _mm_tn(v, k * x[c:2 * c, :])
        ob = _rms(o, hgain_ref[:, h * 128:(h + 1) * 128])
        y_ref[:, RET_W + h * 128:RET_W + (h + 1) * 128] = ob * _silu(g)


def _even_mixer(p, cos_t, sin_t, lb_logits, hg_gain, layer):
    b, lp, n = p.shape
    nc = lp // CHUNK
    mall, masks = _hg_consts()
    const2 = lambda i, j: (0, 0)
    return pl.pallas_call(
        functools.partial(_even_body, layer=layer),
        grid=(b, nc),
        in_specs=[
            pl.BlockSpec((None, CHUNK, n), lambda i, j: (i, j, 0)),
            pl.BlockSpec((CHUNK, RET_DK), lambda i, j: (j, 0)),
            pl.BlockSpec((CHUNK, RET_DK), lambda i, j: (j, 0)),
            pl.BlockSpec(lb_logits.shape, const2),
            pl.BlockSpec((1, HG_W), const2),
            pl.BlockSpec(mall.shape, const2),
            pl.BlockSpec(masks.shape, lambda i, j: (0, 0, 0)),
        ],
        out_specs=pl.BlockSpec((None, CHUNK, RET_W + HG_W), lambda i, j: (i, j, 0)),
        out_shape=jax.ShapeDtypeStruct((b, lp, RET_W + HG_W), F32),
        scratch_shapes=[pltpu.VMEM((RET_HEADS, RET_DK, RET_DK), F32), pltpu.VMEM((HG_HEADS, HG_DK, HG_DK), F32)],
        compiler_params=pltpu.CompilerParams(
            dimension_semantics=("arbitrary", "arbitrary"), vmem_limit_bytes=VMEM_LIMIT),
        name="even_mixer",
    )(p, cos_t, sin_t, lb_logits, hg_gain.reshape(1, HG_W), jnp.asarray(mall, BF), jnp.asarray(masks))


def _odd_body(p_ref, mu_ref, w0_ref, a0_ref, wa2_ref, g2_ref, kks_ref, kas_ref, rk_ref, lnw_ref, lnb_ref,
              cw_ref, alog_ref, dtb_ref, ggain_ref, tril_ref, bones_ref, y_ref,
              hrw_ref, sgd_ref, cpc_ref, ccv_ref):
    @pl.when(pl.program_id(1) == 0)
    def _():
        hrw_ref[...] = jnp.zeros_like(hrw_ref)
        sgd_ref[...] = jnp.zeros_like(sgd_ref)
        cpc_ref[...] = jnp.zeros_like(cpc_ref)
        ccv_ref[...] = jnp.zeros_like(ccv_ref)

    c = CHUNK
    tril = tril_ref[...]
    bones = bones_ref[...]
    ii = lax.broadcasted_iota(jnp.int32, (c, c), 0)
    jj = lax.broadcasted_iota(jnp.int32, (c, c), 1)
    strict = ii > jj
    incl = ii >= jj

    def seg_sum(x):
        return jnp.concatenate(
            [_mmc2(x[:, p * 128:(p + 1) * 128], bones) for p in range(x.shape[1] // 128)], axis=1)

    pc = p_ref[:, 0:RW_IN]
    row = lax.broadcasted_iota(jnp.int32, (c, RW_IN), 0)
    prev = jnp.where(row == 0, cpc_ref[7:8, :], pltpu.roll(pc, 1, 0))
    cpc_ref[...] = pc[c - 8:c, :]
    pcs = pc + (prev - pc) * mu_ref[...]
    r = pcs[:, 0:RW_W]
    k = pcs[:, RW_W:2 * RW_W]
    v = pcs[:, 2 * RW_W:3 * RW_W]
    lo = pcs[:, 3 * RW_W:3 * RW_W + 128]
    glo = pcs[:, 3 * RW_W + 128:RW_IN]
    lane = lax.broadcasted_iota(jnp.int32, (c, 128), 1)
    wa = _mm(jnp.where(lane < RW_DECAY_LORA, jnp.tanh(lo), lo), wa2_ref[...])
    log_w = -_softplus(-(w0_ref[...] + wa[:, 0:RW_W])) - 0.5
    lw = -jnp.exp(log_w)
    a = _sigmoid(a0_ref[...] + wa[:, RW_W:2 * RW_W])
    gate = _mm(_sigmoid(glo), g2_ref[...])
    kks = k * kks_ref[...]
    kkn = kks * lax.rsqrt(seg_sum(kks * kks) + EPS)
    k2 = k * (1.0 + (a - 1.0) * kas_ref[...])
    alpha = -kkn
    beta = kkn * a
    c_inc = _cmm3(tril, lw)
    c_last = c_inc[c - 1:c, :]
    e_inc = jnp.exp(c_inc)
    e_neg = jnp.exp(-c_inc)
    e_rest = jnp.exp(c_last - c_inc)
    ah = alpha * jnp.exp(c_inc - lw)
    rh = r * e_inc
    bt = beta * e_neg
    kt = k2 * e_neg
    bg = beta * e_rest
    kg = k2 * e_rest
    gc = jnp.exp(c_last)
    heads = [slice(h * RW_HD, (h + 1) * RW_HD) for h in range(RW_HEADS)]
    ar = [jnp.concatenate([ah[:, sl], rh[:, sl]], axis=0).astype(BF) for sl in heads]
    grams = [_mm_nt(ar[h], jnp.concatenate([bt[:, sl], kt[:, sl]], axis=0)) for h, sl in enumerate(heads)]
    ii2 = lax.broadcasted_iota(jnp.int32, (c, 2 * c), 0)
    lane2 = lax.broadcasted_iota(jnp.int32, (c, 2 * c), 1)
    jj2 = jnp.bitwise_and(lane2, c - 1)
    rw_ls = [jnp.where(strict, gm[0:c, 0:c], 0.0) for gm in grams]
    ak = [jnp.where((ii2 > jj2) & (lane2 >= c), gm[0:c, :], 0.0) for gm in grams]
    rbk = [jnp.where(ii2 >= jj2, gm[c:2 * c, :], 0.0) for gm in grams]

    base = RW_IN
    x = p_ref[:, base:base + GDN_CONV_CH]
    rowc = lax.broadcasted_iota(jnp.int32, (c, GDN_CONV_CH), 0)
    xm1, xm2, xm3 = _shift_rows(x, ccv_ref, slice(None), GDN_CONV - 1, rowc)
    ccv_ref[...] = x[c - 8:c, :]
    qkv = _silu(xm3 * cw_ref[0:1, :] + xm2 * cw_ref[1:2, :] + xm1 * cw_ref[2:3, :] + x * cw_ref[3:4, :])
    sc = p_ref[:, base + GDN_CONV_CH + GDN_W:base + GDN_IN_PAD]
    g_all = -jnp.exp(alog_ref[...]) * _softplus(sc + dtb_ref[...])
    b_all = _sigmoid(sc)
    g_b = jnp.concatenate([jnp.broadcast_to(g_all[:, h:h + 1], (c, 128)) for h in range(GDN_HEADS)], axis=1)
    cg_all = _cmm3(tril, g_b)
    eye = ii == jj
    gd = []
    for h in range(GDN_HEADS):
        hs = slice(h * 128, (h + 1) * 128)
        q = qkv[:, hs]
        kd = qkv[:, GDN_W + h * 128:GDN_W + (h + 1) * 128]
        vd = qkv[:, 2 * GDN_W + h * 128:2 * GDN_W + (h + 1) * 128]
        q = q * lax.rsqrt(jnp.sum(q * q, axis=-1, keepdims=True) + EPS) * (GDN_DK ** -0.5)
        kd = kd * lax.rsqrt(jnp.sum(kd * kd, axis=-1, keepdims=True) + EPS)
        b_b = jnp.broadcast_to(b_all[:, GDN_HEADS + h:GDN_HEADS + h + 1], (c, 128))
        cg = cg_all[:, hs]
        cg_row = jnp.sum(jnp.where(eye, cg[:, 0:c], 0.0), axis=0, keepdims=True)
        decay = jnp.exp(jnp.where(incl, cg[:, 0:c] - cg_row, -jnp.inf))
        l_m = -jnp.where(strict, b_b[:, 0:c] * _mm_nt(kd, kd) * decay, 0.0)
        eg = jnp.exp(cg)
        cl = cg[c - 1:c, :]
        gd.append(dict(l=l_m, rhs=jnp.concatenate([b_b * vd, b_b * kd * eg], axis=1), qk=_mm_nt(q, kd) * decay,
                       q_in=q * eg, k_out=kd * jnp.exp(cl - cg), sd=jnp.exp(cl)))

    tinv = _tri_inv_many(rw_ls + [d["l"] for d in gd])

    hts = [hrw_ref[h] for h in range(RW_HEADS)]
    hr = [_mm_nt(ar[h], hts[h]) for h in range(RW_HEADS)]
    vs = [v[:, sl] for sl in heads]
    us = [_mm(tinv[h], hr[h][0:c, :] + _mm(ak[h], jnp.concatenate([vs[h], vs[h]], axis=0)))
          for h in range(RW_HEADS)]
    uv = [jnp.concatenate([us[h], vs[h]], axis=0).astype(BF) for h in range(RW_HEADS)]
    y = jnp.concatenate([hr[h][c:2 * c, :] + _mm(rbk[h], uv[h]) for h in range(RW_HEADS)], axis=1)
    for h, sl in enumerate(heads):
        hrw_ref[h] = hts[h] * gc[:, sl] + _mm_tn(uv[h], jnp.concatenate([bg[:, sl], kg[:, sl]], axis=0))
    inv_hd = 1.0 / RW_HD
    yc = y - seg_sum(y) * inv_hd
    yn = yc * lax.rsqrt(seg_sum(yc * yc) * inv_hd + RW_LNX_EPS) * lnw_ref[...] + lnb_ref[...]
    y_ref[:, 0:RW_W] = (yn + seg_sum(r * k2 * rk_ref[...]) * v) * gate

    sols = [_mm(tinv[RW_HEADS + h], gd[h]["rhs"]) for h in range(GDN_HEADS)]
    ss = [sgd_ref[h] for h in range(GDN_HEADS)]
    v_new = [sols[h][:, 0:128] - _mm(sols[h][:, 128:256], ss[h]) for h in range(GDN_HEADS)]
    for h in range(GDN_HEADS):
        d = gd[h]
        o = _mm(d["q_in"], ss[h]) + _mm(d["qk"], v_new[h])
        sgd_ref[h] = ss[h] * d["sd"] + _mm_tn(d["k_out"], v_new[h])
        og = p_ref[:, base + GDN_CONV_CH + h * 128:base + GDN_CONV_CH + (h + 1) * 128]
        y_ref[:, RW_W + h * 128:RW_W + (h + 1) * 128] = _rms(o, ggain_ref[...]) * _silu(og)


def _odd_mixer(p, mu, w0, a0, wa2, g2, kks, kas, rk, lnw, lnb, conv_w, alog, dtb, ggain):
    b, lp, n = p.shape
    nc = lp // CHUNK
    t = np.arange(CHUNK)
    tril = jnp.asarray(t[None, :] <= t[:, None], BF)
    l = np.arange(128)
    bones = jnp.asarray(l[:, None] // RW_HD == l[None, :] // RW_HD, BF)
    row = lambda a: a.reshape(1, -1)
    consts = [row(mu), row(w0), row(a0), wa2, g2, row(kks), row(kas), row(rk), row(lnw), row(lnb),
              conv_w, row(alog), row(dtb), row(ggain), tril, bones]
    const2 = lambda i, j: (0, 0)
    return pl.pallas_call(
        _odd_body,
        grid=(b, nc),
        in_specs=[pl.BlockSpec((None, CHUNK, n), lambda i, j: (i, j, 0))]
        + [pl.BlockSpec(a.shape, const2) for a in consts],
        out_specs=pl.BlockSpec((None, CHUNK, RW_W + GDN_W), lambda i, j: (i, j, 0)),
        out_shape=jax.ShapeDtypeStruct((b, lp, RW_W + GDN_W), F32),
        scratch_shapes=[
            pltpu.VMEM((RW_HEADS, RW_HD, RW_HD), F32),
            pltpu.VMEM((GDN_HEADS, GDN_DK, GDN_DK), F32),
            pltpu.VMEM((8, RW_IN), F32),
            pltpu.VMEM((8, GDN_CONV_CH), F32),
        ],
        compiler_params=pltpu.CompilerParams(
            dimension_semantics=("arbitrary", "arbitrary"), vmem_limit_bytes=VMEM_LIMIT),
        name="odd_mixer",
    )(p, *consts)


def _row_tile(lp):
    best = 8
    for tm in range(8, min(lp, 704) + 1, 8):
        if lp % tm == 0:
            best = tm
    return best


def kernel(x, meta_tokens, norm_gains, w_in_even, w_out_even, hg_lb_logits, hg_norm_gain, w_in_odd, w_out_odd, rw_mu, rw_w0, rw_w2, rw_a0, rw_a2, rw_g2, rw_kk_scale, rw_ka_scale, rw_rk, rw_lnx_w, rw_lnx_b, gdn_conv_w, gdn_a_log, gdn_dt_bias, gdn_norm_gain, ffn_w_up, ffn_conv_w, ffn_conv_b, ffn_w_down):
    bsz, seq, d = x.shape
    depth = norm_gains.shape[0]
    l = N_META + seq
    pad = (-l) % CHUNK
    lp = l + pad
    tm = _row_tile(lp)
    meta = jnp.broadcast_to(meta_tokens[None].astype(x.dtype), (bsz, N_META, d))
    h = jnp.concatenate([jnp.zeros((bsz, pad, d), x.dtype), meta, x], axis=1)

    half = RET_DK // 2
    pos = (jnp.arange(lp, dtype=jnp.int32) - pad).astype(F32)
    inv = ROPE_BASE ** (-jnp.arange(half, dtype=F32) / half)
    ang = pos[:, None] * inv[None, :]
    cos_t = jnp.concatenate([jnp.cos(ang), jnp.cos(ang)], axis=1)
    sin_t = jnp.concatenate([-jnp.sin(ang), jnp.sin(ang)], axis=1)

    for layer in range(depth):
        g = norm_gains[layer]
        i = layer // 2
        if layer % 2 == 0:
            p = _norm_proj(h, g[0], w_in_even[i].astype(BF), tm)
            y = _even_mixer(p, cos_t, sin_t, hg_lb_logits, hg_norm_gain[i], layer)
            w_out = w_out_even[i]
        else:
            w_in = jnp.pad(w_in_odd[i], ((0, 0), (0, ODD_IN_PAD - w_in_odd.shape[2])))
            p = _norm_proj(h, g[0], w_in.astype(BF), tm)
            wa2 = jnp.zeros((RW_DECAY_LORA + RW_AAA_LORA, 2 * RW_W), F32)
            wa2 = wa2.at[:RW_DECAY_LORA, :RW_W].set(rw_w2[i]).at[RW_DECAY_LORA:, RW_W:].set(rw_a2[i])
            lane_pad = lambda a: jnp.pad(a, (0, LANES - a.shape[0]))
            y = _odd_mixer(p, rw_mu[i], rw_w0[i], rw_a0[i], wa2.astype(BF), rw_g2[i].astype(BF),
                           rw_kk_scale[i], rw_ka_scale[i], rw_rk[i].reshape(-1), rw_lnx_w[i], rw_lnx_b[i],
                           gdn_conv_w[i], lane_pad(gdn_a_log[i]), lane_pad(gdn_dt_bias[i]), gdn_norm_gain[i])
            w_out = w_out_odd[i]
        h = _out_res(y, w_out.astype(BF), g[1], h, tm, pad)
        h = _ffn(h, g[2], ffn_w_up[layer].astype(BF), ffn_conv_w[layer], ffn_conv_b[layer],
                 ffn_w_down[layer].astype(BF), g[3], tm, pad)
    return h[:, pad + N_META:]
```

```python
import functools
import math

import numpy as np
import jax
import jax.numpy as jnp
from jax import lax
from jax.experimental import pallas as pl
from jax.experimental.pallas import tpu as pltpu

F32 = jnp.float32
BF = jnp.bfloat16

D_MODEL = 1024
CHUNK = 64
N_META = 16
EPS = 1e-6
ROPE_BASE = 10000.0

RET_HEADS = D_MODEL // 256
RET_DK = 128
HG_HEADS = D_MODEL // 256
HG_DK = 128
RW_HEADS = D_MODEL // 128
RW_HD = 64
RW_DECAY_LORA = 64
RW_AAA_LORA = 64
RW_GATE_LORA = 128
RW_LNX_EPS = 64e-5
GDN_HEADS = D_MODEL // 256
GDN_DK = 128
GDN_CONV = 4
D_FF = 128 * ((8 * D_MODEL // 3 + 127) // 128)
FFN_CONV = 3

RET_W = RET_HEADS * RET_DK
HG_W = HG_HEADS * HG_DK
EVEN_IN = 4 * RET_W + 4 * HG_W
RW_W = RW_HEADS * RW_HD
RW_IN = 3 * RW_W + RW_DECAY_LORA + RW_AAA_LORA + RW_GATE_LORA
GDN_W = GDN_HEADS * GDN_DK
GDN_CONV_CH = 3 * GDN_W
GDN_IN = GDN_CONV_CH + GDN_W + 2 * GDN_HEADS
LANES = 128
GDN_IN_PAD = LANES * ((GDN_IN + LANES - 1) // LANES)
ODD_IN_PAD = RW_IN + GDN_IN_PAD
HG_LEVELS = (32, 16, 8, 4, 2, 1)
VMEM_LIMIT = 56 * 1024 * 1024
FFN_TILE = 256


def _dot(a, b):
    return jnp.dot(a, b, preferred_element_type=F32)


def _mm(a, b):
    return _dot(a.astype(BF), b.astype(BF))


def _mm_nt(a, b):
    return lax.dot_general(a.astype(BF), b.astype(BF), (((1,), (1,)), ((), ())), preferred_element_type=F32)


def _mm_tn(a, b):
    return lax.dot_general(a.astype(BF), b.astype(BF), (((0,), (0,)), ((), ())), preferred_element_type=F32)


def _split2(x):
    hi = x.astype(BF)
    lo = (x - hi.astype(F32)).astype(BF)
    return hi, lo


def _split3(x):
    hi = x.astype(BF)
    r1 = x - hi.astype(F32)
    mid = r1.astype(BF)
    lo = (r1 - mid.astype(F32)).astype(BF)
    return hi, mid, lo


def _cmm3(c, x):
    hi, mid, lo = _split3(x)
    return _dot(c, hi) + _dot(c, mid) + _dot(c, lo)


def _mmc2(x, c):
    hi, lo = _split2(x)
    return _dot(hi, c) + _dot(lo, c)


def _tri_inv_many(ls):
    n = ls[0].shape[0]
    eye = (lax.broadcasted_iota(jnp.int32, (n, n), 0) == lax.broadcasted_iota(jnp.int32, (n, n), 1)).astype(F32)
    ps = [eye + l for l in ls]
    lps = list(ls)
    span = 2
    while span < n:
        lps = [_mm(lp, lp) for lp in lps]
        ps = [p + _mm(lp, p) for lp, p in zip(lps, ps)]
        span *= 2
    return ps


def _sigmoid(x):
    return 1.0 / (1.0 + jnp.exp(-x))


def _silu(x):
    return x * _sigmoid(x)


def _softplus(x):
    return jnp.maximum(x, 0.0) + jnp.log1p(jnp.exp(-jnp.abs(x)))


def _rms(x, g):
    return x * lax.rsqrt(jnp.mean(x * x, axis=-1, keepdims=True) + EPS) * g


def _norm_proj_body(h_ref, g_ref, w_ref, o_ref):
    u = _rms(h_ref[...], g_ref[...])
    o_ref[...] = _dot(u.astype(BF), w_ref[...])


def _norm_proj(h, gain, w, tm):
    b, lp, d = h.shape
    n = w.shape[1]
    return pl.pallas_call(
        _norm_proj_body,
        grid=(b, lp // tm),
        in_specs=[
            pl.BlockSpec((None, tm, d), lambda i, j: (i, j, 0)),
            pl.BlockSpec((1, d), lambda i, j: (0, 0)),
            pl.BlockSpec((d, n), lambda i, j: (0, 0), pipeline_mode=pl.Buffered(1)),
        ],
        out_specs=pl.BlockSpec((None, tm, n), lambda i, j: (i, j, 0)),
        out_shape=jax.ShapeDtypeStruct((b, lp, n), F32),
        compiler_params=pltpu.CompilerParams(
            dimension_semantics=("arbitrary", "arbitrary"), vmem_limit_bytes=VMEM_LIMIT),
        name="norm_proj",
    )(h, gain.reshape(1, d), w)


def _out_res_body(y_ref, w_ref, g_ref, h_ref, o_ref, *, tm, pad):
    m = _dot(y_ref[...].astype(BF), w_ref[...])
    t = pl.program_id(1) * tm + lax.broadcasted_iota(jnp.int32, m.shape, 0)
    o_ref[...] = jnp.where(t >= pad, h_ref[...] + _rms(m, g_ref[...]), 0.0)


def _out_res(y, w, gain, h, tm, pad):
    b, lp, d = h.shape
    k = y.shape[2]
    return pl.pallas_call(
        functools.partial(_out_res_body, tm=tm, pad=pad),
        grid=(b, lp // tm),
        in_specs=[
            pl.BlockSpec((None, tm, k), lambda i, j: (i, j, 0)),
            pl.BlockSpec((k, d), lambda i, j: (0, 0), pipeline_mode=pl.Buffered(1)),
            pl.BlockSpec((1, d), lambda i, j: (0, 0)),
            pl.BlockSpec((None, tm, d), lambda i, j: (i, j, 0)),
        ],
        out_specs=pl.BlockSpec((None, tm, d), lambda i, j: (i, j, 0)),
        out_shape=jax.ShapeDtypeStruct((b, lp, d), F32),
        compiler_params=pltpu.CompilerParams(
            dimension_semantics=("arbitrary", "arbitrary"), vmem_limit_bytes=VMEM_LIMIT),
        name="out_res",
    )(y, w, gain.reshape(1, d), h)


def _shift_rows(a, carry_ref, cols, n_back, row):
    out = []
    for s in range(1, n_back + 1):
        sh = pltpu.roll(a, s, 0)
        for r in range(s):
            sh = jnp.where(row == r, carry_ref[8 - s + r:9 - s + r, cols], sh)
        out.append(sh)
    return out


def _ffn_body(h_ref, g2_ref, wup_ref, cw_ref, cb_ref, wdn_ref, g3_ref, o_ref, carry_ref, *, tm, pad):
    @pl.when(pl.program_id(1) == 0)
    def _():
        carry_ref[...] = jnp.zeros_like(carry_ref)

    x = h_ref[...]
    u = _rms(x, g2_ref[...]).astype(BF)
    row = lax.broadcasted_iota(jnp.int32, (tm, FFN_TILE), 0)
    n_tiles = D_FF // FFN_TILE

    def up(c):
        return [_dot(u, wup_ref[:, part * D_FF + c * FFN_TILE:part * D_FF + (c + 1) * FFN_TILE])
                for part in range(2)]

    acc = jnp.zeros((tm, D_MODEL), F32)
    pending = [up(c) for c in range(min(2, n_tiles))]
    for c in range(n_tiles):
        cur = pending.pop(0)
        if c + 2 < n_tiles:
            pending.append(up(c + 2))
        z = []
        for part in range(2):
            cols = slice(part * D_FF + c * FFN_TILE, part * D_FF + (c + 1) * FFN_TILE)
            a = cur[part]
            am1, am2 = _shift_rows(a, carry_ref, cols, FFN_CONV - 1, row)
            carry_ref[:, cols] = a[tm - 8:tm, :]
            z.append(am2 * cw_ref[0:1, cols] + am1 * cw_ref[1:2, cols] + a * cw_ref[2:3, cols] + cb_ref[:, cols])
        act = _silu(z[0]) * z[1]
        acc = acc + _dot(act.astype(BF), wdn_ref[c * FFN_TILE:(c + 1) * FFN_TILE, :])
    t = pl.program_id(1) * tm + lax.broadcasted_iota(jnp.int32, acc.shape, 0)
    o_ref[...] = jnp.where(t >= pad, x + _rms(acc, g3_ref[...]), 0.0)


def _ffn(h, g2, w_up, conv_w, conv_b, w_down, g3, tm, pad):
    b, lp, d = h.shape
    ff2 = w_up.shape[1]
    const = lambda i, j: (0, 0)
    return pl.pallas_call(
        functools.partial(_ffn_body, tm=tm, pad=pad),
        grid=(b, lp // tm),
        in_specs=[
            pl.BlockSpec((None, tm, d), lambda i, j: (i, j, 0)),
            pl.BlockSpec((1, d), const),
            pl.BlockSpec((d, ff2), const, pipeline_mode=pl.Buffered(1)),
            pl.BlockSpec((FFN_CONV, ff2), const),
            pl.BlockSpec((1, ff2), const),
            pl.BlockSpec((ff2 // 2, d), const, pipeline_mode=pl.Buffered(1)),
            pl.BlockSpec((1, d), const),
        ],
        out_specs=pl.BlockSpec((None, tm, d), lambda i, j: (i, j, 0)),
        out_shape=jax.ShapeDtypeStruct((b, lp, d), F32),
        scratch_shapes=[pltpu.VMEM((8, ff2), F32)],
        compiler_params=pltpu.CompilerParams(
            dimension_semantics=("arbitrary", "arbitrary"), vmem_limit_bytes=VMEM_LIMIT),
        name="conv_ffn",
    )(h, g2.reshape(1, d), w_up, conv_w, conv_b.reshape(1, ff2), w_down, g3.reshape(1, d))


def _hg_consts():
    c = CHUNK
    t = np.arange(c)
    m = np.zeros((2 + len(HG_LEVELS), c, c), np.float32)
    m[0] = t[None, :] <= t[:, None]
    m[1] = t[None, :] > t[:, None]
    masks = np.zeros((len(HG_LEVELS) + 1, c, c), np.float32)
    ii, jj = t[:, None], t[None, :]
    for li, s in enumerate(HG_LEVELS):
        for i in range(c):
            mid = (i // (2 * s)) * (2 * s) + s - 1
            if i % (2 * s) >= s:
                m[2 + li, i, mid + 1:i + 1] = 1.0
            else:
                m[2 + li, i, i + 1:mid + 1] = 1.0
        masks[li] = (ii // (2 * s) == jj // (2 * s)) & (ii % (2 * s) >= s) & (jj % (2 * s) < s)
    masks[-1] = ii == jj
    return m.reshape(-1, c), masks


def _even_body(p_ref, cos_ref, sin_ref, lbl_ref, hgain_ref, mall_ref, masks_ref, y_ref, sret_ref, shg_ref, *,
               layer):
    @pl.when(pl.program_id(1) == 0)
    def _():
        sret_ref[...] = jnp.zeros_like(sret_ref)
        shg_ref[...] = jnp.zeros_like(shg_ref)

    c = CHUNK
    cosv = cos_ref[...]
    sinv = sin_ref[...]
    ii = lax.broadcasted_iota(jnp.int32, (c, c), 0)
    jj = lax.broadcasted_iota(jnp.int32, (c, c), 1)
    diff = (ii - jj).astype(F32)
    rowi = lax.broadcasted_iota(jnp.int32, (c, RET_DK), 0).astype(F32)

    nh = RET_HEADS
    lgs = [math.log1p(-(2.0 ** (-5.0 - h))) for h in range(nh)]
    col = lambda base, h: slice(base + h * 128, base + (h + 1) * 128)
    qs = [p_ref[:, col(0, h)] for h in range(nh)]
    ks = [p_ref[:, col(RET_W, h)] for h in range(nh)]
    vs = [p_ref[:, col(2 * RET_W, h)] for h in range(nh)]
    qs = [q * cosv + pltpu.roll(q, RET_DK // 2, 1) * sinv for q in qs]
    ks = [(k * cosv + pltpu.roll(k, RET_DK // 2, 1) * sinv) * (RET_DK ** -0.5) for k in ks]
    ss = [sret_ref[h] for h in range(nh)]
    att = [_mm_nt(qs[h], ks[h]) * jnp.where(diff >= 0, jnp.exp(lgs[h] * jnp.maximum(diff, 0.0)), 0.0)
           for h in range(nh)]
    inter = [_mm(qs[h] * jnp.exp(lgs[h] * (rowi + 1.0)), ss[h]) for h in range(nh)]
    kv = [_mm_tn(ks[h] * jnp.exp(lgs[h] * (c - 1.0 - rowi)), vs[h]) for h in range(nh)]
    outs = [_mm(att[h], vs[h]) + inter[h] for h in range(nh)]
    for h in range(nh):
        sret_ref[h] = ss[h] * math.exp(lgs[h] * c) + kv[h]
        o = outs[h]
        xc = o - jnp.mean(o, axis=-1, keepdims=True)
        oa = xc * lax.rsqrt(jnp.mean(xc * xc, axis=-1, keepdims=True) + EPS)
        y_ref[:, col(0, h)] = oa * _silu(p_ref[:, col(3 * RET_W, h)])

    lgt = lbl_ref[...]
    mx = jnp.max(lgt, axis=0, keepdims=True)
    ex = jnp.exp(lgt - mx)
    sm = ex / jnp.sum(ex, axis=0, keepdims=True)
    lb_all = jnp.sum(sm[0:layer + 1, :], axis=0, keepdims=True)
    base = 4 * RET_W
    mall = mall_ref[...]
    nh = HG_HEADS
    nl = len(HG_LEVELS)
    lbs = [lb_all[:, col(0, h)] for h in range(nh)]
    qs = [p_ref[:, col(base, h)] for h in range(nh)]
    fbs = [p_ref[:, col(base + HG_W, h)] for h in range(nh)]
    vs = [p_ref[:, col(base + 2 * HG_W, h)] for h in range(nh)]
    ks = [(1.0 - lbs[h]) * _sigmoid(-fbs[h]) for h in range(nh)]
    xs = [jnp.exp(_cmm3(mall, jnp.log(lbs[h] + (1.0 - lbs[h]) * _sigmoid(fbs[h])))) for h in range(nh)]
    sts = [shg_ref[h] for h in range(nh)]
    att = [masks_ref[nl] * _mm_nt(qs[h], ks[h]) for h in range(nh)]
    for li in range(nl):
        lv = slice((2 + li) * c, (3 + li) * c)
        att = [att[h] + masks_ref[li] * _mm_nt(qs[h] * xs[h][lv, :], ks[h] * xs[h][lv, :]) for h in range(nh)]
    inter = [_mm_nt(qs[h] * xs[h][0:c, :], sts[h]) for h in range(nh)]
    kv = [_mm_tn(vs[h], ks[h] * xs[h][c:2 * c, :]) for h in range(nh)]
    outs = [_mm(att[h], vs[h]) + inter[h] for h in range(nh)]
    for h in range(nh):
        shg_ref[h] = sts[h] * xs[h][c - 1:c, :] + kv[h]
        ob = _rms(outs[h], hgain_ref[:, col(0, h)])
        y_ref[:, col(RET_W, h)] = ob * _silu(p_ref[:, col(base + 3 * HG_W, h)])


def _even_mixer(p, cos_t, sin_t, lb_logits, hg_gain, layer):
    b, lp, n = p.shape
    nc = lp // CHUNK
    mall, masks = _hg_consts()
    const2 = lambda i, j: (0, 0)
    return pl.pallas_call(
        functools.partial(_even_body, layer=layer),
        grid=(b, nc),
        in_specs=[
            pl.BlockSpec((None, CHUNK, n), lambda i, j: (i, j, 0)),
            pl.BlockSpec((CHUNK, RET_DK), lambda i, j: (j, 0)),
            pl.BlockSpec((CHUNK, RET_DK), lambda i, j: (j, 0)),
            pl.BlockSpec(lb_logits.shape, const2),
            pl.BlockSpec((1, HG_W), const2),
            pl.BlockSpec(mall.shape, const2),
            pl.BlockSpec(masks.shape, lambda i, j: (0, 0, 0)),
        ],
        out_specs=pl.BlockSpec((None, CHUNK, RET_W + HG_W), lambda i, j: (i, j, 0)),
        out_shape=jax.ShapeDtypeStruct((b, lp, RET_W + HG_W), F32),
        scratch_shapes=[pltpu.VMEM((RET_HEADS, RET_DK, RET_DK), F32), pltpu.VMEM((HG_HEADS, HG_DK, HG_DK), F32)],
        compiler_params=pltpu.CompilerParams(
            dimension_semantics=("arbitrary", "arbitrary"), vmem_limit_bytes=VMEM_LIMIT),
        name="even_mixer",
    )(p, cos_t, sin_t, lb_logits, hg_gain.reshape(1, HG_W), jnp.asarray(mall, BF), jnp.asarray(masks))


def _odd_body(p_ref, mu_ref, w0_ref, a0_ref, wa2_ref, g2_ref, kks_ref, kas_ref, rk_ref, lnw_ref, lnb_ref,
              cw_ref, alog_ref, dtb_ref, ggain_ref, tril_ref, bones_ref, y_ref,
              hrw_ref, sgd_ref, cpc_ref, ccv_ref):
    @pl.when(pl.program_id(1) == 0)
    def _():
        hrw_ref[...] = jnp.zeros_like(hrw_ref)
        sgd_ref[...] = jnp.zeros_like(sgd_ref)
        cpc_ref[...] = jnp.zeros_like(cpc_ref)
        ccv_ref[...] = jnp.zeros_like(ccv_ref)

    c = CHUNK
    tril = tril_ref[...]
    bones = bones_ref[...]
    ii = lax.broadcasted_iota(jnp.int32, (c, c), 0)
    jj = lax.broadcasted_iota(jnp.int32, (c, c), 1)
    strict = ii > jj
    incl = ii >= jj

    def seg_sum(x):
        return jnp.concatenate(
            [_mmc2(x[:, p * 128:(p + 1) * 128], bones) for p in range(x.shape[1] // 128)], axis=1)

    pc = p_ref[:, 0:RW_IN]
    row = lax.broadcasted_iota(jnp.int32, (c, RW_IN), 0)
    prev = jnp.where(row == 0, cpc_ref[7:8, :], pltpu.roll(pc, 1, 0))
    cpc_ref[...] = pc[c - 8:c, :]
    pcs = pc + (prev - pc) * mu_ref[...]
    r = pcs[:, 0:RW_W]
    k = pcs[:, RW_W:2 * RW_W]
    v = pcs[:, 2 * RW_W:3 * RW_W]
    lo = pcs[:, 3 * RW_W:3 * RW_W + 128]
    glo = pcs[:, 3 * RW_W + 128:RW_IN]
    lane = lax.broadcasted_iota(jnp.int32, (c, 128), 1)
    wa = _mm(jnp.where(lane < RW_DECAY_LORA, jnp.tanh(lo), lo), wa2_ref[...])
    log_w = -_softplus(-(w0_ref[...] + wa[:, 0:RW_W])) - 0.5
    lw = -jnp.exp(log_w)
    a = _sigmoid(a0_ref[...] + wa[:, RW_W:2 * RW_W])
    gate = _mm(_sigmoid(glo), g2_ref[...])
    kks = k * kks_ref[...]
    kkn = kks * lax.rsqrt(seg_sum(kks * kks) + EPS)
    k2 = k * (1.0 + (a - 1.0) * kas_ref[...])
    alpha = -kkn
    beta = kkn * a
    c_inc = _cmm3(tril, lw)
    c_last = c_inc[c - 1:c, :]
    e_inc = jnp.exp(c_inc)
    e_neg = jnp.exp(-c_inc)
    e_rest = jnp.exp(c_last - c_inc)
    ah = alpha * jnp.exp(c_inc - lw)
    rh = r * e_inc
    bt = beta * e_neg
    kt = k2 * e_neg
    bg = beta * e_rest
    kg = k2 * e_rest
    gc = jnp.exp(c_last)
    heads = [slice(h * RW_HD, (h + 1) * RW_HD) for h in range(RW_HEADS)]
    ar = [jnp.concatenate([ah[:, sl], rh[:, sl]], axis=0).astype(BF) for sl in heads]
    grams = [_mm_nt(ar[h], jnp.concatenate([bt[:, sl], kt[:, sl]], axis=0)) for h, sl in enumerate(heads)]
    ii2 = lax.broadcasted_iota(jnp.int32, (c, 2 * c), 0)
    lane2 = lax.broadcasted_iota(jnp.int32, (c, 2 * c), 1)
    jj2 = jnp.bitwise_and(lane2, c - 1)
    rw_ls = [jnp.where(strict, gm[0:c, 0:c], 0.0) for gm in grams]
    ak = [jnp.where((ii2 > jj2) & (lane2 >= c), gm[0:c, :], 0.0) for gm in grams]
    rbk = [jnp.where(ii2 >= jj2, gm[c:2 * c, :], 0.0) for gm in grams]

    base = RW_IN
    x = p_ref[:, base:base + GDN_CONV_CH]
    rowc = lax.broadcasted_iota(jnp.int32, (c, GDN_CONV_CH), 0)
    xm1, xm2, xm3 = _shift_rows(x, ccv_ref, slice(None), GDN_CONV - 1, rowc)
    ccv_ref[...] = x[c - 8:c, :]
    qkv = _silu(xm3 * cw_ref[0:1, :] + xm2 * cw_ref[1:2, :] + xm1 * cw_ref[2:3, :] + x * cw_ref[3:4, :])
    sc = p_ref[:, base + GDN_CONV_CH + GDN_W:base + GDN_IN_PAD]
    g_all = -jnp.exp(alog_ref[...]) * _softplus(sc + dtb_ref[...])
    b_all = _sigmoid(sc)
    g_b = jnp.concatenate([jnp.broadcast_to(g_all[:, h:h + 1], (c, 128)) for h in range(GDN_HEADS)], axis=1)
    cg_all = _cmm3(tril, g_b)
    eye = ii == jj
    gd = []
    for h in range(GDN_HEADS):
        hs = slice(h * 128, (h + 1) * 128)
        q = qkv[:, hs]
        kd = qkv[:, GDN_W + h * 128:GDN_W + (h + 1) * 128]
        vd = qkv[:, 2 * GDN_W + h * 128:2 * GDN_W + (h + 1) * 128]
        q = q * lax.rsqrt(jnp.sum(q * q, axis=-1, keepdims=True) + EPS) * (GDN_DK ** -0.5)
        kd = kd * lax.rsqrt(jnp.sum(kd * kd, axis=-1, keepdims=True) + EPS)
        b_b = jnp.broadcast_to(b_all[:, GDN_HEADS + h:GDN_HEADS + h + 1], (c, 128))
        cg = cg_all[:, hs]
        cg_row = jnp.sum(jnp.where(eye, cg[:, 0:c], 0.0), axis=0, keepdims=True)
        decay = jnp.exp(jnp.where(incl, cg[:, 0:c] - cg_row, -jnp.inf))
        l_m = -jnp.where(strict, b_b[:, 0:c] * _mm_nt(kd, kd) * decay, 0.0)
        eg = jnp.exp(cg)
        cl = cg[c - 1:c, :]
        gd.append(dict(l=l_m, rhs=jnp.concatenate([b_b * vd, b_b * kd * eg], axis=1), qk=_mm_nt(q, kd) * decay,
                       q_in=q * eg, k_out=kd * jnp.exp(cl - cg), sd=jnp.exp(cl)))

    tinv = _tri_inv_many(rw_ls + [d["l"] for d in gd])

    hts = [hrw_ref[h] for h in range(RW_HEADS)]
    hr = [_mm_nt(ar[h], hts[h]) for h in range(RW_HEADS)]
    vs = [v[:, sl] for sl in heads]
    us = [_mm(tinv[h], hr[h][0:c, :] + _mm(ak[h], jnp.concatenate([vs[h], vs[h]], axis=0)))
          for h in range(RW_HEADS)]
    uv = [jnp.concatenate([us[h], vs[h]], axis=0).astype(BF) for h in range(RW_HEADS)]
    y = jnp.concatenate([hr[h][c:2 * c, :] + _mm(rbk[h], uv[h]) for h in range(RW_HEADS)], axis=1)
    for h, sl in enumerate(heads):
        hrw_ref[h] = hts[h] * gc[:, sl] + _mm_tn(uv[h], jnp.concatenate([bg[:, sl], kg[:, sl]], axis=0))
    inv_hd = 1.0 / RW_HD
    yc = y - seg_sum(y) * inv_hd
    yn = yc * lax.rsqrt(seg_sum(yc * yc) * inv_hd + RW_LNX_EPS) * lnw_ref[...] + lnb_ref[...]
    y_ref[:, 0:RW_W] = (yn + seg_sum(r * k2 * rk_ref[...]) * v) * gate

    sols = [_mm(tinv[RW_HEADS + h], gd[h]["rhs"]) for h in range(GDN_HEADS)]
    ss = [sgd_ref[h] for h in range(GDN_HEADS)]
    v_new = [sols[h][:, 0:128] - _mm(sols[h][:, 128:256], ss[h]) for h in range(GDN_HEADS)]
    for h in range(GDN_HEADS):
        d = gd[h]
        o = _mm(d["q_in"], ss[h]) + _mm(d["qk"], v_new[h])
        sgd_ref[h] = ss[h] * d["sd"] + _mm_tn(d["k_out"], v_new[h])
        og = p_ref[:, base + GDN_CONV_CH + h * 128:base + GDN_CONV_CH + (h + 1) * 128]
        y_ref[:, RW_W + h * 128:RW_W + (h + 1) * 128] = _rms(o, ggain_ref[...]) * _silu(og)


def _odd_mixer(p, mu, w0, a0, wa2, g2, kks, kas, rk, lnw, lnb, conv_w, alog, dtb, ggain):
    b, lp, n = p.shape
    nc = lp // CHUNK
    t = np.arange(CHUNK)
    tril = jnp.asarray(t[None, :] <= t[:, None], BF)
    l = np.arange(128)
    bones = jnp.asarray(l[:, None] // RW_HD == l[None, :] // RW_HD, BF)
    row = lambda a: a.reshape(1, -1)
    consts = [row(mu), row(w0), row(a0), wa2, g2, row(kks), row(kas), row(rk), row(lnw), row(lnb),
              conv_w, row(alog), row(dtb), row(ggain), tril, bones]
    const2 = lambda i, j: (0, 0)
    return pl.pallas_call(
        _odd_body,
        grid=(b, nc),
        in_specs=[pl.BlockSpec((None, CHUNK, n), lambda i, j: (i, j, 0))]
        + [pl.BlockSpec(a.shape, const2) for a in consts],
        out_specs=pl.BlockSpec((None, CHUNK, RW_W + GDN_W), lambda i, j: (i, j, 0)),
        out_shape=jax.ShapeDtypeStruct((b, lp, RW_W + GDN_W), F32),
        scratch_shapes=[
            pltpu.VMEM((RW_HEADS, RW_HD, RW_HD), F32),
            pltpu.VMEM((GDN_HEADS, GDN_DK, GDN_DK), F32),
            pltpu.VMEM((8, RW_IN), F32),
            pltpu.VMEM((8, GDN_CONV_CH), F32),
        ],
        compiler_params=pltpu.CompilerParams(
            dimension_semantics=("arbitrary", "arbitrary"), vmem_limit_bytes=VMEM_LIMIT),
        name="odd_mixer",
    )(p, *consts)


def _row_tile(lp):
    best = 8
    for tm in range(8, min(lp, 704) + 1, 8):
        if lp % tm == 0:
            best = tm
    return best


def kernel(x, meta_tokens, norm_gains, w_in_even, w_out_even, hg_lb_logits, hg_norm_gain, w_in_odd, w_out_odd, rw_mu, rw_w0, rw_w2, rw_a0, rw_a2, rw_g2, rw_kk_scale, rw_ka_scale, rw_rk, rw_lnx_w, rw_lnx_b, gdn_conv_w, gdn_a_log, gdn_dt_bias, gdn_norm_gain, ffn_w_up, ffn_conv_w, ffn_conv_b, ffn_w_down):
    bsz, seq, d = x.shape
    depth = norm_gains.shape[0]
    l = N_META + seq
    pad = (-l) % CHUNK
    lp = l + pad
    tm = _row_tile(lp)
    meta = jnp.broadcast_to(meta_tokens[None].astype(x.dtype), (bsz, N_META, d))
    h = jnp.concatenate([jnp.zeros((bsz, pad, d), x.dtype), meta, x], axis=1)

    half = RET_DK // 2
    pos = (jnp.arange(lp, dtype=jnp.int32) - pad).astype(F32)
    inv = ROPE_BASE ** (-jnp.arange(half, dtype=F32) / half)
    ang = pos[:, None] * inv[None, :]
    cos_t = jnp.concatenate([jnp.cos(ang), jnp.cos(ang)], axis=1)
    sin_t = jnp.concatenate([-jnp.sin(ang), jnp.sin(ang)], axis=1)

    for layer in range(depth):
        g = norm_gains[layer]
        i = layer // 2
        if layer % 2 == 0:
            p = _norm_proj(h, g[0], w_in_even[i].astype(BF), tm)
            y = _even_mixer(p, cos_t, sin_t, hg_lb_logits, hg_norm_gain[i], layer)
            w_out = w_out_even[i]
        else:
            w_in = jnp.pad(w_in_odd[i], ((0, 0), (0, ODD_IN_PAD - w_in_odd.shape[2])))
            p = _norm_proj(h, g[0], w_in.astype(BF), tm)
            wa2 = jnp.zeros((RW_DECAY_LORA + RW_AAA_LORA, 2 * RW_W), F32)
            wa2 = wa2.at[:RW_DECAY_LORA, :RW_W].set(rw_w2[i]).at[RW_DECAY_LORA:, RW_W:].set(rw_a2[i])
            lane_pad = lambda a: jnp.pad(a, (0, LANES - a.shape[0]))
            y = _odd_mixer(p, rw_mu[i], rw_w0[i], rw_a0[i], wa2.astype(BF), rw_g2[i].astype(BF),
                           rw_kk_scale[i], rw_ka_scale[i], rw_rk[i].reshape(-1), rw_lnx_w[i], rw_lnx_b[i],
                           gdn_conv_w[i], lane_pad(gdn_a_log[i]), lane_pad(gdn_dt_bias[i]), gdn_norm_gain[i])
            w_out = w_out_odd[i]
        h = _out_res(y, w_out.astype(BF), g[1], h, tm, pad)
        h = _ffn(h, g[2], ffn_w_up[layer].astype(BF), ffn_conv_w[layer], ffn_conv_b[layer],
                 ffn_w_down[layer].astype(BF), g[3], tm, pad)
    return h[:, pad + N_META:]
```

```python
import functools
import math

import numpy as np
import jax
import jax.numpy as jnp
from jax import lax
from jax.experimental import pallas as pl
from jax.experimental.pallas import tpu as pltpu

F32 = jnp.float32
BF = jnp.bfloat16

D_MODEL = 1024
CHUNK = 64
N_META = 16
EPS = 1e-6
ROPE_BASE = 10000.0

RET_HEADS = D_MODEL // 256
RET_DK = 128
HG_HEADS = D_MODEL // 256
HG_DK = 128
RW_HEADS = D_MODEL // 128
RW_HD = 64
RW_DECAY_LORA = 64
RW_AAA_LORA = 64
RW_GATE_LORA = 128
RW_LNX_EPS = 64e-5
GDN_HEADS = D_MODEL // 256
GDN_DK = 128
GDN_CONV = 4
D_FF = 128 * ((8 * D_MODEL // 3 + 127) // 128)
FFN_CONV = 3

RET_W = RET_HEADS * RET_DK
HG_W = HG_HEADS * HG_DK
EVEN_IN = 4 * RET_W + 4 * HG_W
RW_W = RW_HEADS * RW_HD
RW_IN = 3 * RW_W + RW_DECAY_LORA + RW_AAA_LORA + RW_GATE_LORA
GDN_W = GDN_HEADS * GDN_DK
GDN_CONV_CH = 3 * GDN_W
GDN_IN = GDN_CONV_CH + GDN_W + 2 * GDN_HEADS
LANES = 128
GDN_IN_PAD = LANES * ((GDN_IN + LANES - 1) // LANES)
ODD_IN_PAD = RW_IN + GDN_IN_PAD
HG_LEVELS = (32, 16, 8, 4, 2, 1)
VMEM_LIMIT = 56 * 1024 * 1024
FFN_TILE = 256
ODD_NB = 4
ODD_GROUP = 2


def _dot(a, b):
    return jnp.dot(a, b, preferred_element_type=F32)


def _mm(a, b):
    return _dot(a.astype(BF), b.astype(BF))


def _mm_nt(a, b):
    return lax.dot_general(a.astype(BF), b.astype(BF), (((1,), (1,)), ((), ())), preferred_element_type=F32)


def _mm_tn(a, b):
    return lax.dot_general(a.astype(BF), b.astype(BF), (((0,), (0,)), ((), ())), preferred_element_type=F32)


def _split2(x):
    hi = x.astype(BF)
    lo = (x - hi.astype(F32)).astype(BF)
    return hi, lo


def _split3(x):
    hi = x.astype(BF)
    r1 = x - hi.astype(F32)
    mid = r1.astype(BF)
    lo = (r1 - mid.astype(F32)).astype(BF)
    return hi, mid, lo


def _cmm3(c, x):
    hi, mid, lo = _split3(x)
    return _dot(c, hi) + _dot(c, mid) + _dot(c, lo)


def _mmc2(x, c):
    hi, lo = _split2(x)
    return _dot(hi, c) + _dot(lo, c)


def _sigmoid(x):
    return 1.0 / (1.0 + jnp.exp(-x))


def _silu(x):
    return x * _sigmoid(x)


def _softplus(x):
    return jnp.maximum(x, 0.0) + jnp.log1p(jnp.exp(-jnp.abs(x)))


def _rms(x, g):
    return x * lax.rsqrt(jnp.mean(x * x, axis=-1, keepdims=True) + EPS) * g


def _norm_proj_body(h_ref, g_ref, w_ref, o_ref):
    u = _rms(h_ref[...], g_ref[...])
    o_ref[...] = _dot(u.astype(BF), w_ref[...])


def _norm_proj(h, gain, w, tm):
    b, lp, d = h.shape
    n = w.shape[1]
    return pl.pallas_call(
        _norm_proj_body,
        grid=(b, lp // tm),
        in_specs=[
            pl.BlockSpec((None, tm, d), lambda i, j: (i, j, 0)),
            pl.BlockSpec((1, d), lambda i, j: (0, 0)),
            pl.BlockSpec((d, n), lambda i, j: (0, 0), pipeline_mode=pl.Buffered(1)),
        ],
        out_specs=pl.BlockSpec((None, tm, n), lambda i, j: (i, j, 0)),
        out_shape=jax.ShapeDtypeStruct((b, lp, n), F32),
        compiler_params=pltpu.CompilerParams(
            dimension_semantics=("arbitrary", "arbitrary"), vmem_limit_bytes=VMEM_LIMIT),
        name="norm_proj",
    )(h, gain.reshape(1, d), w)


def _out_res_body(y_ref, w_ref, g_ref, h_ref, o_ref, *, tm, pad):
    m = _dot(y_ref[...].astype(BF), w_ref[...])
    t = pl.program_id(1) * tm + lax.broadcasted_iota(jnp.int32, m.shape, 0)
    o_ref[...] = jnp.where(t >= pad, h_ref[...] + _rms(m, g_ref[...]), 0.0)


def _out_res(y, w, gain, h, tm, pad):
    b, lp, d = h.shape
    k = y.shape[2]
    return pl.pallas_call(
        functools.partial(_out_res_body, tm=tm, pad=pad),
        grid=(b, lp // tm),
        in_specs=[
            pl.BlockSpec((None, tm, k), lambda i, j: (i, j, 0)),
            pl.BlockSpec((k, d), lambda i, j: (0, 0), pipeline_mode=pl.Buffered(1)),
            pl.BlockSpec((1, d), lambda i, j: (0, 0)),
            pl.BlockSpec((None, tm, d), lambda i, j: (i, j, 0)),
        ],
        out_specs=pl.BlockSpec((None, tm, d), lambda i, j: (i, j, 0)),
        out_shape=jax.ShapeDtypeStruct((b, lp, d), F32),
        compiler_params=pltpu.CompilerParams(
            dimension_semantics=("arbitrary", "arbitrary"), vmem_limit_bytes=VMEM_LIMIT),
        name="out_res",
    )(y, w, gain.reshape(1, d), h)


def _shift_rows(a, carry_ref, cols, n_back, row):
    out = []
    for s in range(1, n_back + 1):
        sh = pltpu.roll(a, s, 0)
        for r in range(s):
            sh = jnp.where(row == r, carry_ref[8 - s + r:9 - s + r, cols], sh)
        out.append(sh)
    return out


def _ffn_body(h_ref, g2_ref, wup_ref, cw_ref, cb_ref, wdn_ref, g3_ref, o_ref, carry_ref, *, tm, pad):
    @pl.when(pl.program_id(1) == 0)
    def _():
        carry_ref[...] = jnp.zeros_like(carry_ref)

    x = h_ref[...]
    u = _rms(x, g2_ref[...]).astype(BF)
    row = lax.broadcasted_iota(jnp.int32, (tm, FFN_TILE), 0)
    n_tiles = D_FF // FFN_TILE

    def up(c):
        return [_dot(u, wup_ref[:, part * D_FF + c * FFN_TILE:part * D_FF + (c + 1) * FFN_TILE])
                for part in range(2)]

    acc = jnp.zeros((tm, D_MODEL), F32)
    pending = [up(c) for c in range(min(2, n_tiles))]
    for c in range(n_tiles):
        cur = pending.pop(0)
        if c + 2 < n_tiles:
            pending.append(up(c + 2))
        z = []
        for part in range(2):
            cols = slice(part * D_FF + c * FFN_TILE, part * D_FF + (c + 1) * FFN_TILE)
            a = cur[part]
            am1, am2 = _shift_rows(a, carry_ref, cols, FFN_CONV - 1, row)
            carry_ref[:, cols] = a[tm - 8:tm, :]
            z.append(am2 * cw_ref[0:1, cols] + am1 * cw_ref[1:2, cols] + a * cw_ref[2:3, cols] + cb_ref[:, cols])
        act = _silu(z[0]) * z[1]
        acc = acc + _dot(act.astype(BF), wdn_ref[c * FFN_TILE:(c + 1) * FFN_TILE, :])
    t = pl.program_id(1) * tm + lax.broadcasted_iota(jnp.int32, acc.shape, 0)
    o_ref[...] = jnp.where(t >= pad, x + _rms(acc, g3_ref[...]), 0.0)


def _ffn(h, g2, w_up, conv_w, conv_b, w_down, g3, tm, pad):
    b, lp, d = h.shape
    ff2 = w_up.shape[1]
    const = lambda i, j: (0, 0)
    return pl.pallas_call(
        functools.partial(_ffn_body, tm=tm, pad=pad),
        grid=(b, lp // tm),
        in_specs=[
            pl.BlockSpec((None, tm, d), lambda i, j: (i, j, 0)),
            pl.BlockSpec((1, d), const),
            pl.BlockSpec((d, ff2), const, pipeline_mode=pl.Buffered(1)),
            pl.BlockSpec((FFN_CONV, ff2), const),
            pl.BlockSpec((1, ff2), const),
            pl.BlockSpec((ff2 // 2, d), const, pipeline_mode=pl.Buffered(1)),
            pl.BlockSpec((1, d), const),
        ],
        out_specs=pl.BlockSpec((None, tm, d), lambda i, j: (i, j, 0)),
        out_shape=jax.ShapeDtypeStruct((b, lp, d), F32),
        scratch_shapes=[pltpu.VMEM((8, ff2), F32)],
        compiler_params=pltpu.CompilerParams(
            dimension_semantics=("arbitrary", "arbitrary"), vmem_limit_bytes=VMEM_LIMIT),
        name="conv_ffn",
    )(h, g2.reshape(1, d), w_up, conv_w, conv_b.reshape(1, ff2), w_down, g3.reshape(1, d))


def _hg_consts():
    c = CHUNK
    t = np.arange(c)
    m = np.zeros((2 + len(HG_LEVELS), c, c), np.float32)
    m[0] = t[None, :] <= t[:, None]
    m[1] = t[None, :] > t[:, None]
    masks = np.zeros((len(HG_LEVELS) + 1, c, c), np.float32)
    ii, jj = t[:, None], t[None, :]
    for li, s in enumerate(HG_LEVELS):
        for i in range(c):
            mid = (i // (2 * s)) * (2 * s) + s - 1
            if i % (2 * s) >= s:
                m[2 + li, i, mid + 1:i + 1] = 1.0
            else:
                m[2 + li, i, i + 1:mid + 1] = 1.0
        masks[li] = (ii // (2 * s) == jj // (2 * s)) & (ii % (2 * s) >= s) & (jj % (2 * s) < s)
    masks[-1] = ii == jj
    return m.reshape(-1, c), masks


def _even_body(p_ref, cos_ref, sin_ref, lbl_ref, hgain_ref, mall_ref, masks_ref, y_ref, sret_ref, shg_ref, *,
               layer):
    @pl.when(pl.program_id(1) == 0)
    def _():
        sret_ref[...] = jnp.zeros_like(sret_ref)
        shg_ref[...] = jnp.zeros_like(shg_ref)

    c = CHUNK
    cosv = cos_ref[...]
    sinv = sin_ref[...]
    ii = lax.broadcasted_iota(jnp.int32, (c, c), 0)
    jj = lax.broadcasted_iota(jnp.int32, (c, c), 1)
    diff = (ii - jj).astype(F32)
    rowi = lax.broadcasted_iota(jnp.int32, (c, RET_DK), 0).astype(F32)

    nh = RET_HEADS
    lgs = [math.log1p(-(2.0 ** (-5.0 - h))) for h in range(nh)]
    col = lambda base, h: slice(base + h * 128, base + (h + 1) * 128)
    qs = [p_ref[:, col(0, h)] for h in range(nh)]
    ks = [p_ref[:, col(RET_W, h)] for h in range(nh)]
    vs = [p_ref[:, col(2 * RET_W, h)] for h in range(nh)]
    qs = [q * cosv + pltpu.roll(q, RET_DK // 2, 1) * sinv for q in qs]
    ks = [(k * cosv + pltpu.roll(k, RET_DK // 2, 1) * sinv) * (RET_DK ** -0.5) for k in ks]
    ss = [sret_ref[h] for h in range(nh)]
    att = [_mm_nt(qs[h], ks[h]) * jnp.where(diff >= 0, jnp.exp(lgs[h] * jnp.maximum(diff, 0.0)), 0.0)
           for h in range(nh)]
    inter = [_mm(qs[h] * jnp.exp(lgs[h] * (rowi + 1.0)), ss[h]) for h in range(nh)]
    kv = [_mm_tn(ks[h] * jnp.exp(lgs[h] * (c - 1.0 - rowi)), vs[h]) for h in range(nh)]
    outs = [_mm(att[h], vs[h]) + inter[h] for h in range(nh)]
    for h in range(nh):
        sret_ref[h] = ss[h] * math.exp(lgs[h] * c) + kv[h]
        o = outs[h]
        xc = o - jnp.mean(o, axis=-1, keepdims=True)
        oa = xc * lax.rsqrt(jnp.mean(xc * xc, axis=-1, keepdims=True) + EPS)
        y_ref[:, col(0, h)] = oa * _silu(p_ref[:, col(3 * RET_W, h)])

    lgt = lbl_ref[...]
    mx = jnp.max(lgt, axis=0, keepdims=True)
    ex = jnp.exp(lgt - mx)
    sm = ex / jnp.sum(ex, axis=0, keepdims=True)
    lb_all = jnp.sum(sm[0:layer + 1, :], axis=0, keepdims=True)
    base = 4 * RET_W
    mall = mall_ref[...]
    nh = HG_HEADS
    nl = len(HG_LEVELS)
    lbs = [lb_all[:, col(0, h)] for h in range(nh)]
    qs = [p_ref[:, col(base, h)] for h in range(nh)]
    fbs = [p_ref[:, col(base + HG_W, h)] for h in range(nh)]
    vs = [p_ref[:, col(base + 2 * HG_W, h)] for h in range(nh)]
    ks = [(1.0 - lbs[h]) * _sigmoid(-fbs[h]) for h in range(nh)]
    xs = [jnp.exp(_cmm3(mall, jnp.log(lbs[h] + (1.0 - lbs[h]) * _sigmoid(fbs[h])))) for h in range(nh)]
    sts = [shg_ref[h] for h in range(nh)]
    att = [masks_ref[nl] * _mm_nt(qs[h], ks[h]) for h in range(nh)]
    for li in range(nl):
        lv = slice((2 + li) * c, (3 + li) * c)
        att = [att[h] + masks_ref[li] * _mm_nt(qs[h] * xs[h][lv, :], ks[h] * xs[h][lv, :]) for h in range(nh)]
    inter = [_mm_nt(qs[h] * xs[h][0:c, :], sts[h]) for h in range(nh)]
    kv = [_mm_tn(vs[h], ks[h] * xs[h][c:2 * c, :]) for h in range(nh)]
    outs = [_mm(att[h], vs[h]) + inter[h] for h in range(nh)]
    for h in range(nh):
        shg_ref[h] = sts[h] * xs[h][c - 1:c, :] + kv[h]
        ob = _rms(outs[h], hgain_ref[:, col(0, h)])
        y_ref[:, col(RET_W, h)] = ob * _silu(p_ref[:, col(base + 3 * HG_W, h)])


def _even_mixer(p, cos_t, sin_t, lb_logits, hg_gain, layer):
    b, lp, n = p.shape
    nc = lp // CHUNK
    mall, masks = _hg_consts()
    const2 = lambda i, j: (0, 0)
    return pl.pallas_call(
        functools.partial(_even_body, layer=layer),
        grid=(b, nc),
        in_specs=[
            pl.BlockSpec((None, CHUNK, n), lambda i, j: (i, j, 0)),
            pl.BlockSpec((CHUNK, RET_DK), lambda i, j: (j, 0)),
            pl.BlockSpec((CHUNK, RET_DK), lambda i, j: (j, 0)),
            pl.BlockSpec(lb_logits.shape, const2),
            pl.BlockSpec((1, HG_W), const2),
            pl.BlockSpec(mall.shape, const2),
            pl.BlockSpec(masks.shape, lambda i, j: (0, 0, 0)),
        ],
        out_specs=pl.BlockSpec((None, CHUNK, RET_W + HG_W), lambda i, j: (i, j, 0)),
        out_shape=jax.ShapeDtypeStruct((b, lp, RET_W + HG_W), F32),
        scratch_shapes=[pltpu.VMEM((RET_HEADS, RET_DK, RET_DK), F32), pltpu.VMEM((HG_HEADS, HG_DK, HG_DK), F32)],
        compiler_params=pltpu.CompilerParams(
            dimension_semantics=("arbitrary", "arbitrary"), vmem_limit_bytes=VMEM_LIMIT),
        name="even_mixer",
    )(p, cos_t, sin_t, lb_logits, hg_gain.reshape(1, HG_W), jnp.asarray(mall, BF), jnp.asarray(masks))


def _merge_pieces(*lists):
    total = max(len(l) for l in lists)
    keyed = []
    for li, l in enumerate(lists):
        for k, piece in enumerate(l):
            keyed.append(((k + 0.5) * total / len(l), li, k, piece))
    keyed.sort(key=lambda t: t[:3])
    return [t[3] for t in keyed]


def _odd_body(p_ref, mu_ref, w0_ref, a0_ref, wa2_ref, g2_ref, kks_ref, kas_ref, rk_ref, lnw_ref, lnb_ref,
              cw_ref, alog_ref, dtb_ref, ggain_ref, tril_ref, bones_ref, y_ref,
              hrw_ref, sgd_ref, cpc_ref, ccv_ref):
    @pl.when(pl.program_id(1) == 0)
    def _():
        hrw_ref[...] = jnp.zeros_like(hrw_ref)
        sgd_ref[...] = jnp.zeros_like(sgd_ref)
        cpc_ref[...] = jnp.zeros_like(cpc_ref)
        ccv_ref[...] = jnp.zeros_like(ccv_ref)

    nb = p_ref.shape[0]
    c = CHUNK
    tril = tril_ref[...]
    bones = bones_ref[...]
    ii = lax.broadcasted_iota(jnp.int32, (c, c), 0)
    jj = lax.broadcasted_iota(jnp.int32, (c, c), 1)
    strict = ii > jj
    incl = ii >= jj
    eye = ii == jj
    eye_f = eye.astype(F32)
    ii2 = lax.broadcasted_iota(jnp.int32, (c, 2 * c), 0)
    lane2 = lax.broadcasted_iota(jnp.int32, (c, 2 * c), 1)
    jj2 = jnp.bitwise_and(lane2, c - 1)
    ak_mask = (ii2 > jj2) & (lane2 >= c)
    rbk_mask = ii2 >= jj2
    lane = lax.broadcasted_iota(jnp.int32, (c, 128), 1)
    heads = [slice(h * RW_HD, (h + 1) * RW_HD) for h in range(RW_HEADS)]
    base = RW_IN
    inv_hd = 1.0 / RW_HD

    def seg_sum(x):
        return jnp.concatenate(
            [_mmc2(x[:, p * 128:(p + 1) * 128], bones) for p in range(x.shape[1] // 128)], axis=1)

    def prep_pieces(b, d):
        def shift():
            pc = p_ref[b, :, 0:RW_IN]
            row = lax.broadcasted_iota(jnp.int32, (c, RW_IN), 0)
            prev = jnp.where(row == 0, cpc_ref[b, 7:8, :], pltpu.roll(pc, 1, 0))
            cpc_ref[b] = pc[c - 8:c, :]
            pcs = pc + (prev - pc) * mu_ref[...]
            d.update(r=pcs[:, 0:RW_W], k=pcs[:, RW_W:2 * RW_W], v=pcs[:, 2 * RW_W:3 * RW_W],
                     lo=pcs[:, 3 * RW_W:3 * RW_W + 128], glo=pcs[:, 3 * RW_W + 128:RW_IN])

        def lora():
            lo = d["lo"]
            wa = _mm(jnp.where(lane < RW_DECAY_LORA, jnp.tanh(lo), lo), wa2_ref[...])
            log_w = -_softplus(-(w0_ref[...] + wa[:, 0:RW_W])) - 0.5
            d["lw"] = -jnp.exp(log_w)
            d["a"] = _sigmoid(a0_ref[...] + wa[:, RW_W:2 * RW_W])
            d["gate"] = _mm(_sigmoid(d["glo"]), g2_ref[...])

        def keys():
            kks = d["k"] * kks_ref[...]
            kkn = kks * lax.rsqrt(seg_sum(kks * kks) + EPS)
            d["k2"] = d["k"] * (1.0 + (d["a"] - 1.0) * kas_ref[...])
            d["kkn"] = kkn
            d["beta"] = kkn * d["a"]
            d["c_inc"] = _cmm3(tril, d["lw"])

        def decays():
            c_inc, k2, beta = d["c_inc"], d["k2"], d["beta"]
            c_last = c_inc[c - 1:c, :]
            e_neg = jnp.exp(-c_inc)
            e_rest = jnp.exp(c_last - c_inc)
            d.update(ah=-d["kkn"] * jnp.exp(c_inc - d["lw"]), rh=d["r"] * jnp.exp(c_inc), bt=beta * e_neg,
                     kt=k2 * e_neg, bg=beta * e_rest, kg=k2 * e_rest, gc=jnp.exp(c_last),
                     ar=[], ls=[], ak=[], rbk=[], gcol=[], bgkg=[])

        def gram(h):
            def run():
                sl = heads[h]
                ar = jnp.concatenate([d["ah"][:, sl], d["rh"][:, sl]], axis=0).astype(BF)
                gm = _mm_nt(ar, jnp.concatenate([d["bt"][:, sl], d["kt"][:, sl]], axis=0))
                d["ar"].append(ar)
                d["ls"].append(jnp.where(strict, gm[0:c, 0:c], 0.0))
                d["ak"].append(jnp.where(ak_mask, gm[0:c, :], 0.0))
                d["rbk"].append(jnp.where(rbk_mask, gm[c:2 * c, :], 0.0))
                d["gcol"].append(jnp.sum(eye_f * d["gc"][:, sl], axis=1, keepdims=True))
                d["bgkg"].append(jnp.concatenate([d["bg"][:, sl], d["kg"][:, sl]], axis=0).astype(BF))
            return run

        def conv():
            x = p_ref[b, :, base:base + GDN_CONV_CH]
            rowc = lax.broadcasted_iota(jnp.int32, (c, GDN_CONV_CH), 0)
            xm1, xm2, xm3 = _shift_rows(x, ccv_ref.at[b], slice(None), GDN_CONV - 1, rowc)
            ccv_ref[b] = x[c - 8:c, :]
            d["qkv"] = _silu(xm3 * cw_ref[0:1, :] + xm2 * cw_ref[1:2, :] + xm1 * cw_ref[2:3, :] + x * cw_ref[3:4, :])

        def gates():
            sc = p_ref[b, :, base + GDN_CONV_CH + GDN_W:base + GDN_IN_PAD]
            g_all = -jnp.exp(alog_ref[...]) * _softplus(sc + dtb_ref[...])
            d["b_all"] = _sigmoid(sc)
            g_b = jnp.concatenate(
                [jnp.broadcast_to(g_all[:, h:h + 1], (c, 128)) for h in range(GDN_HEADS)], axis=1)
            d["cg_all"] = _cmm3(tril, g_b)
            d["gd"] = []

        def gdn_head(h):
            def run():
                hs = slice(h * 128, (h + 1) * 128)
                qkv = d["qkv"]
                q = qkv[:, hs]
                kd = qkv[:, GDN_W + h * 128:GDN_W + (h + 1) * 128]
                vd = qkv[:, 2 * GDN_W + h * 128:2 * GDN_W + (h + 1) * 128]
                q = q * lax.rsqrt(jnp.sum(q * q, axis=-1, keepdims=True) + EPS) * (GDN_DK ** -0.5)
                kd = kd * lax.rsqrt(jnp.sum(kd * kd, axis=-1, keepdims=True) + EPS)
                b_b = jnp.broadcast_to(d["b_all"][:, GDN_HEADS + h:GDN_HEADS + h + 1], (c, 128))
                cg = d["cg_all"][:, hs]
                cg_row = jnp.sum(jnp.where(eye, cg[:, 0:c], 0.0), axis=0, keepdims=True)
                decay = jnp.exp(jnp.where(incl, cg[:, 0:c] - cg_row, -jnp.inf))
                eg = jnp.exp(cg)
                cl = cg[c - 1:c, :]
                d["gd"].append(dict(
                    l=-jnp.where(strict, b_b[:, 0:c] * _mm_nt(kd, kd) * decay, 0.0),
                    rhs=jnp.concatenate([b_b * vd, b_b * kd * eg], axis=1), qk=_mm_nt(q, kd) * decay,
                    q_in=q * eg, k_out=kd * jnp.exp(cl - cg), sd=jnp.exp(cl)))
            return run

        return ([shift, lora, keys, decays] + [gram(h) for h in range(RW_HEADS)]
                + [conv, gates] + [gdn_head(h) for h in range(GDN_HEADS)])

    def inverse_pieces(rows, ds, out):
        st = {}

        def start():
            st["l"] = [l for b in rows for l in ds[b]["ls"]] + [g["l"] for b in rows for g in ds[b]["gd"]]
            st["p"] = [eye_f + l for l in st["l"]]

        def square():
            st["l"] = [_mm(l, l) for l in st["l"]]

        def extend():
            st["p"] = [p + _mm(l, p) for l, p in zip(st["l"], st["p"])]

        def finish():
            out["tinv"] = st["p"]

        levels = []
        span = 2
        while span < c:
            levels += [square, extend]
            span *= 2
        return [start] + levels + [finish]

    def state_pieces(rows, ds, inv):
        units = [(b, h) for b in rows for h in range(RW_HEADS)]
        gunits = [(b, h) for b in rows for h in range(GDN_HEADS)]
        st = {}

        def rw_read():
            st["h0"] = [hrw_ref[b, h] for b, h in units]
            st["hr"] = [_mm(ds[b]["ar"][h], st["h0"][u]) for u, (b, h) in enumerate(units)]
            st["vs"] = [ds[b]["v"][:, heads[h]] for b, h in units]

        def rw_mix():
            vs = st["vs"]
            st["x"] = [st["hr"][u][0:c, :] + _mm(ds[b]["ak"][h], jnp.concatenate([vs[u], vs[u]], axis=0))
                       for u, (b, h) in enumerate(units)]

        def rw_solve():
            us = [_mm(inv["tinv"][u], st["x"][u]) for u in range(len(units))]
            st["uv"] = [jnp.concatenate([us[u], st["vs"][u]], axis=0).astype(BF) for u in range(len(units))]

        def rw_out():
            st["ys"] = [st["hr"][u][c:2 * c, :] + _mm(ds[b]["rbk"][h], st["uv"][u])
                        for u, (b, h) in enumerate(units)]
            for u, (b, h) in enumerate(units):
                hrw_ref[b, h] = st["h0"][u] * ds[b]["gcol"][h] + _mm_tn(ds[b]["bgkg"][h], st["uv"][u])

        def rw_finish(k, b):
            def run():
                d = ds[b]
                y = jnp.concatenate(st["ys"][k * RW_HEADS:(k + 1) * RW_HEADS], axis=1)
                yc = y - seg_sum(y) * inv_hd
                yn = yc * lax.rsqrt(seg_sum(yc * yc) * inv_hd + RW_LNX_EPS) * lnw_ref[...] + lnb_ref[...]
                y_ref[b, :, 0:RW_W] = (yn + seg_sum(d["r"] * d["k2"] * rk_ref[...]) * d["v"]) * d["gate"]
            return run

        def gd_solve():
            n_rw = len(units)
            sols = [_mm(inv["tinv"][n_rw + u], ds[b]["gd"][h]["rhs"]) for u, (b, h) in enumerate(gunits)]
            st["s0"] = [sgd_ref[b, h] for b, h in gunits]
            st["v_new"] = [sols[u][:, 0:128] - _mm(sols[u][:, 128:256], st["s0"][u]) for u in range(len(gunits))]

        def gd_out(u, b, h):
            def run():
                g = ds[b]["gd"][h]
                s0, v_new = st["s0"][u], st["v_new"][u]
                o = _mm(g["q_in"], s0) + _mm(g["qk"], v_new)
                sgd_ref[b, h] = s0 * g["sd"] + _mm_tn(g["k_out"], v_new)
                og = p_ref[b, :, base + GDN_CONV_CH + h * 128:base + GDN_CONV_CH + (h + 1) * 128]
                y_ref[b, :, RW_W + h * 128:RW_W + (h + 1) * 128] = _rms(o, ggain_ref[...]) * _silu(og)
            return run

        return ([rw_read, rw_mix, rw_solve, rw_out] + [rw_finish(k, b) for k, b in enumerate(rows)]
                + [gd_solve] + [gd_out(u, b, h) for u, (b, h) in enumerate(gunits)])

    group = min(ODD_GROUP, nb)
    groups = [list(range(g, g + group)) for g in range(0, nb, group)]
    ds = [dict() for _ in range(nb)]
    invs = [dict() for _ in groups]
    stage1 = [_merge_pieces(*[prep_pieces(b, ds[b]) for b in rows]) for rows in groups]
    stage2 = [inverse_pieces(rows, ds, invs[g]) for g, rows in enumerate(groups)]
    stage3 = [state_pieces(rows, ds, invs[g]) for g, rows in enumerate(groups)]
    for slot in range(len(groups) + 2):
        active = []
        if slot < len(groups):
            active.append(stage1[slot])
        if 0 <= slot - 1 < len(groups):
            active.append(stage2[slot - 1])
        if 0 <= slot - 2 < len(groups):
            active.append(stage3[slot - 2])
        for piece in _merge_pieces(*active):
            piece()


def _odd_mixer(p, mu, w0, a0, wa2, g2, kks, kas, rk, lnw, lnb, conv_w, alog, dtb, ggain):
    b, lp, n = p.shape
    nc = lp // CHUNK
    nb = ODD_NB if b % ODD_NB == 0 else 1
    t = np.arange(CHUNK)
    tril = jnp.asarray(t[None, :] <= t[:, None], BF)
    l = np.arange(128)
    bones = jnp.asarray(l[:, None] // RW_HD == l[None, :] // RW_HD, BF)
    row = lambda a: a.reshape(1, -1)
    consts = [row(mu), row(w0), row(a0), wa2, g2, row(kks), row(kas), row(rk), row(lnw), row(lnb),
              conv_w, row(alog), row(dtb), row(ggain), tril, bones]
    const2 = lambda i, j: (0, 0)
    return pl.pallas_call(
        _odd_body,
        grid=(b // nb, nc),
        in_specs=[pl.BlockSpec((nb, CHUNK, n), lambda i, j: (i, j, 0))]
        + [pl.BlockSpec(a.shape, const2) for a in consts],
        out_specs=pl.BlockSpec((nb, CHUNK, RW_W + GDN_W), lambda i, j: (i, j, 0)),
        out_shape=jax.ShapeDtypeStruct((b, lp, RW_W + GDN_W), F32),
        scratch_shapes=[
            pltpu.VMEM((nb, RW_HEADS, RW_HD, RW_HD), F32),
            pltpu.VMEM((nb, GDN_HEADS, GDN_DK, GDN_DK), F32),
            pltpu.VMEM((nb, 8, RW_IN), F32),
            pltpu.VMEM((nb, 8, GDN_CONV_CH), F32),
        ],
        compiler_params=pltpu.CompilerParams(
            dimension_semantics=("arbitrary", "arbitrary"), vmem_limit_bytes=VMEM_LIMIT),
        name="odd_mixer",
    )(p, *consts)


def _row_tile(lp):
    best = 8
    for tm in range(8, min(lp, 704) + 1, 8):
        if lp % tm == 0:
            best = tm
    return best


def kernel(x, meta_tokens, norm_gains, w_in_even, w_out_even, hg_lb_logits, hg_norm_gain, w_in_odd, w_out_odd, rw_mu, rw_w0, rw_w2, rw_a0, rw_a2, rw_g2, rw_kk_scale, rw_ka_scale, rw_rk, rw_lnx_w, rw_lnx_b, gdn_conv_w, gdn_a_log, gdn_dt_bias, gdn_norm_gain, ffn_w_up, ffn_conv_w, ffn_conv_b, ffn_w_down):
    bsz, seq, d = x.shape
    depth = norm_gains.shape[0]
    l = N_META + seq
    pad = (-l) % CHUNK
    lp = l + pad
    tm = _row_tile(lp)
    meta = jnp.broadcast_to(meta_tokens[None].astype(x.dtype), (bsz, N_META, d))
    h = jnp.concatenate([jnp.zeros((bsz, pad, d), x.dtype), meta, x], axis=1)

    half = RET_DK // 2
    pos = (jnp.arange(lp, dtype=jnp.int32) - pad).astype(F32)
    inv = ROPE_BASE ** (-jnp.arange(half, dtype=F32) / half)
    ang = pos[:, None] * inv[None, :]
    cos_t = jnp.concatenate([jnp.cos(ang), jnp.cos(ang)], axis=1)
    sin_t = jnp.concatenate([-jnp.sin(ang), jnp.sin(ang)], axis=1)

    for layer in range(depth):
        g = norm_gains[layer]
        i = layer // 2
        if layer % 2 == 0:
            p = _norm_proj(h, g[0], w_in_even[i].astype(BF), tm)
            y = _even_mixer(p, cos_t, sin_t, hg_lb_logits, hg_norm_gain[i], layer)
            w_out = w_out_even[i]
        else:
            w_in = jnp.pad(w_in_odd[i], ((0, 0), (0, ODD_IN_PAD - w_in_odd.shape[2])))
            p = _norm_proj(h, g[0], w_in.astype(BF), tm)
            wa2 = jnp.zeros((RW_DECAY_LORA + RW_AAA_LORA, 2 * RW_W), F32)
            wa2 = wa2.at[:RW_DECAY_LORA, :RW_W].set(rw_w2[i]).at[RW_DECAY_LORA:, RW_W:].set(rw_a2[i])
            lane_pad = lambda a: jnp.pad(a, (0, LANES - a.shape[0]))
            y = _odd_mixer(p, rw_mu[i], rw_w0[i], rw_a0[i], wa2.astype(BF), rw_g2[i].astype(BF),
                           rw_kk_scale[i], rw_ka_scale[i], rw_rk[i].reshape(-1), rw_lnx_w[i], rw_lnx_b[i],
                           gdn_conv_w[i], lane_pad(gdn_a_log[i]), lane_pad(gdn_dt_bias[i]), gdn_norm_gain[i])
            w_out = w_out_odd[i]
        h = _out_res(y, w_out.astype(BF), g[1], h, tm, pad)
        h = _ffn(h, g[2], ffn_w_up[layer].astype(BF), ffn_conv_w[layer], ffn_conv_b[layer],
                 ffn_w_down[layer].astype(BF), g[3], tm, pad)
    return h[:, pad + N_META:]
```

```python
import functools
import math

import numpy as np
import jax
import jax.numpy as jnp
from jax import lax
from jax.experimental import pallas as pl
from jax.experimental.pallas import tpu as pltpu

F32 = jnp.float32
BF = jnp.bfloat16

D_MODEL = 1024
CHUNK = 64
N_META = 16
EPS = 1e-6
ROPE_BASE = 10000.0

RET_HEADS = D_MODEL // 256
RET_DK = 128
HG_HEADS = D_MODEL // 256
HG_DK = 128
RW_HEADS = D_MODEL // 128
RW_HD = 64
RW_DECAY_LORA = 64
RW_AAA_LORA = 64
RW_GATE_LORA = 128
RW_LNX_EPS = 64e-5
GDN_HEADS = D_MODEL // 256
GDN_DK = 128
GDN_CONV = 4
D_FF = 128 * ((8 * D_MODEL // 3 + 127) // 128)
FFN_CONV = 3

RET_W = RET_HEADS * RET_DK
HG_W = HG_HEADS * HG_DK
EVEN_IN = 4 * RET_W + 4 * HG_W
RW_W = RW_HEADS * RW_HD
RW_IN = 3 * RW_W + RW_DECAY_LORA + RW_AAA_LORA + RW_GATE_LORA
GDN_W = GDN_HEADS * GDN_DK
GDN_CONV_CH = 3 * GDN_W
GDN_IN = GDN_CONV_CH + GDN_W + 2 * GDN_HEADS
LANES = 128
GDN_IN_PAD = LANES * ((GDN_IN + LANES - 1) // LANES)
ODD_IN_PAD = RW_IN + GDN_IN_PAD
HG_LEVELS = (32, 16, 8, 4, 2, 1)
VMEM_LIMIT = 56 * 1024 * 1024
FFN_TILE = 256
EVEN_NB = 4
ODD_NB = 4
ODD_GROUP = 2


def _dot(a, b):
    return jnp.dot(a, b, preferred_element_type=F32)


def _mm(a, b):
    return _dot(a.astype(BF), b.astype(BF))


def _mm_nt(a, b):
    return lax.dot_general(a.astype(BF), b.astype(BF), (((1,), (1,)), ((), ())), preferred_element_type=F32)


def _mm_tn(a, b):
    return lax.dot_general(a.astype(BF), b.astype(BF), (((0,), (0,)), ((), ())), preferred_element_type=F32)


def _split2(x):
    hi = x.astype(BF)
    lo = (x - hi.astype(F32)).astype(BF)
    return hi, lo


def _split3(x):
    hi = x.astype(BF)
    r1 = x - hi.astype(F32)
    mid = r1.astype(BF)
    lo = (r1 - mid.astype(F32)).astype(BF)
    return hi, mid, lo


def _cmm3(c, x):
    hi, mid, lo = _split3(x)
    return _dot(c, hi) + _dot(c, mid) + _dot(c, lo)


def _cmm2(c, x):
    hi, lo = _split2(x)
    return _dot(c, hi) + _dot(c, lo)


def _mmc2(x, c):
    hi, lo = _split2(x)
    return _dot(hi, c) + _dot(lo, c)


def _sigmoid(x):
    return 1.0 / (1.0 + jnp.exp(-x))


def _silu(x):
    return x * _sigmoid(x)


def _softplus(x):
    return jnp.maximum(x, 0.0) + jnp.log1p(jnp.exp(-jnp.abs(x)))


def _rms(x, g):
    return x * lax.rsqrt(jnp.mean(x * x, axis=-1, keepdims=True) + EPS) * g


def _norm_proj_body(h_ref, g_ref, w_ref, o_ref):
    u = _rms(h_ref[...], g_ref[...])
    o_ref[...] = _dot(u.astype(BF), w_ref[...])


def _norm_proj(h, gain, w, tm):
    b, lp, d = h.shape
    n = w.shape[1]
    return pl.pallas_call(
        _norm_proj_body,
        grid=(b, lp // tm),
        in_specs=[
            pl.BlockSpec((None, tm, d), lambda i, j: (i, j, 0)),
            pl.BlockSpec((1, d), lambda i, j: (0, 0)),
            pl.BlockSpec((d, n), lambda i, j: (0, 0), pipeline_mode=pl.Buffered(1)),
        ],
        out_specs=pl.BlockSpec((None, tm, n), lambda i, j: (i, j, 0)),
        out_shape=jax.ShapeDtypeStruct((b, lp, n), F32),
        compiler_params=pltpu.CompilerParams(
            dimension_semantics=("arbitrary", "arbitrary"), vmem_limit_bytes=VMEM_LIMIT),
        name="norm_proj",
    )(h, gain.reshape(1, d), w)


def _out_res_body(y_ref, w_ref, g_ref, h_ref, o_ref, *, tm, pad):
    m = _dot(y_ref[...].astype(BF), w_ref[...])
    t = pl.program_id(1) * tm + lax.broadcasted_iota(jnp.int32, m.shape, 0)
    o_ref[...] = jnp.where(t >= pad, h_ref[...] + _rms(m, g_ref[...]), 0.0)


def _out_res(y, w, gain, h, tm, pad):
    b, lp, d = h.shape
    k = y.shape[2]
    return pl.pallas_call(
        functools.partial(_out_res_body, tm=tm, pad=pad),
        grid=(b, lp // tm),
        in_specs=[
            pl.BlockSpec((None, tm, k), lambda i, j: (i, j, 0)),
            pl.BlockSpec((k, d), lambda i, j: (0, 0), pipeline_mode=pl.Buffered(1)),
            pl.BlockSpec((1, d), lambda i, j: (0, 0)),
            pl.BlockSpec((None, tm, d), lambda i, j: (i, j, 0)),
        ],
        out_specs=pl.BlockSpec((None, tm, d), lambda i, j: (i, j, 0)),
        out_shape=jax.ShapeDtypeStruct((b, lp, d), F32),
        compiler_params=pltpu.CompilerParams(
            dimension_semantics=("arbitrary", "arbitrary"), vmem_limit_bytes=VMEM_LIMIT),
        name="out_res",
    )(y, w, gain.reshape(1, d), h)


def _shift_rows(a, carry_ref, cols, n_back, row):
    out = []
    for s in range(1, n_back + 1):
        sh = pltpu.roll(a, s, 0)
        for r in range(s):
            sh = jnp.where(row == r, carry_ref[8 - s + r:9 - s + r, cols], sh)
        out.append(sh)
    return out


def _ffn_body(h_ref, g2_ref, wup_ref, cw_ref, cb_ref, wdn_ref, g3_ref, o_ref, carry_ref, *, tm, pad):
    @pl.when(pl.program_id(1) == 0)
    def _():
        carry_ref[...] = jnp.zeros_like(carry_ref)

    x = h_ref[...]
    u = _rms(x, g2_ref[...]).astype(BF)
    row = lax.broadcasted_iota(jnp.int32, (tm, FFN_TILE), 0)
    n_tiles = D_FF // FFN_TILE

    def up(c):
        return [_dot(u, wup_ref[:, part * D_FF + c * FFN_TILE:part * D_FF + (c + 1) * FFN_TILE])
                for part in range(2)]

    acc = jnp.zeros((tm, D_MODEL), F32)
    pending = [up(c) for c in range(min(2, n_tiles))]
    for c in range(n_tiles):
        cur = pending.pop(0)
        if c + 2 < n_tiles:
            pending.append(up(c + 2))
        z = []
        for part in range(2):
            cols = slice(part * D_FF + c * FFN_TILE, part * D_FF + (c + 1) * FFN_TILE)
            a = cur[part]
            am1, am2 = _shift_rows(a, carry_ref, cols, FFN_CONV - 1, row)
            carry_ref[:, cols] = a[tm - 8:tm, :]
            z.append(am2 * cw_ref[0:1, cols] + am1 * cw_ref[1:2, cols] + a * cw_ref[2:3, cols] + cb_ref[:, cols])
        act = _silu(z[0]) * z[1]
        acc = acc + _dot(act.astype(BF), wdn_ref[c * FFN_TILE:(c + 1) * FFN_TILE, :])
    t = pl.program_id(1) * tm + lax.broadcasted_iota(jnp.int32, acc.shape, 0)
    o_ref[...] = jnp.where(t >= pad, x + _rms(acc, g3_ref[...]), 0.0)


def _ffn(h, g2, w_up, conv_w, conv_b, w_down, g3, tm, pad):
    b, lp, d = h.shape
    ff2 = w_up.shape[1]
    const = lambda i, j: (0, 0)
    return pl.pallas_call(
        functools.partial(_ffn_body, tm=tm, pad=pad),
        grid=(b, lp // tm),
        in_specs=[
            pl.BlockSpec((None, tm, d), lambda i, j: (i, j, 0)),
            pl.BlockSpec((1, d), const),
            pl.BlockSpec((d, ff2), const, pipeline_mode=pl.Buffered(1)),
            pl.BlockSpec((FFN_CONV, ff2), const),
            pl.BlockSpec((1, ff2), const),
            pl.BlockSpec((ff2 // 2, d), const, pipeline_mode=pl.Buffered(1)),
            pl.BlockSpec((1, d), const),
        ],
        out_specs=pl.BlockSpec((None, tm, d), lambda i, j: (i, j, 0)),
        out_shape=jax.ShapeDtypeStruct((b, lp, d), F32),
        scratch_shapes=[pltpu.VMEM((8, ff2), F32)],
        compiler_params=pltpu.CompilerParams(
            dimension_semantics=("arbitrary", "arbitrary"), vmem_limit_bytes=VMEM_LIMIT),
        name="conv_ffn",
    )(h, g2.reshape(1, d), w_up, conv_w, conv_b.reshape(1, ff2), w_down, g3.reshape(1, d))


def _hg_consts():
    c = CHUNK
    t = np.arange(c)
    m = np.zeros((2 + len(HG_LEVELS), c, c), np.float32)
    m[0] = t[None, :] <= t[:, None]
    m[1] = t[None, :] > t[:, None]
    masks = np.zeros((len(HG_LEVELS) + 1, c, c), np.float32)
    ii, jj = t[:, None], t[None, :]
    for li, s in enumerate(HG_LEVELS):
        for i in range(c):
            mid = (i // (2 * s)) * (2 * s) + s - 1
            if i % (2 * s) >= s:
                m[2 + li, i, mid + 1:i + 1] = 1.0
            else:
                m[2 + li, i, i + 1:mid + 1] = 1.0
        masks[li] = (ii // (2 * s) == jj // (2 * s)) & (ii % (2 * s) >= s) & (jj % (2 * s) < s)
    masks[-1] = ii == jj
    return m.reshape(-1, c), masks


def _even_body(p_ref, cos_ref, sin_ref, lbl_ref, hgain_ref, mall_ref, masks_ref, y_ref, sret_ref, shg_ref, *,
               layer):
    @pl.when(pl.program_id(1) == 0)
    def _():
        sret_ref[...] = jnp.zeros_like(sret_ref)
        shg_ref[...] = jnp.zeros_like(shg_ref)

    c = CHUNK
    cosv = cos_ref[...]
    sinv = sin_ref[...]
    ii = lax.broadcasted_iota(jnp.int32, (c, c), 0)
    jj = lax.broadcasted_iota(jnp.int32, (c, c), 1)
    diff = (ii - jj).astype(F32)
    rowi = lax.broadcasted_iota(jnp.int32, (c, RET_DK), 0).astype(F32)

    nb = p_ref.shape[0]
    col = lambda base, h: slice(base + h * 128, base + (h + 1) * 128)
    units = [(b, h) for b in range(nb) for h in range(RET_HEADS)]
    nu = len(units)
    lgs = [math.log1p(-(2.0 ** (-5.0 - h))) for _, h in units]
    qs = [p_ref[b, :, col(0, h)] for b, h in units]
    ks = [p_ref[b, :, col(RET_W, h)] for b, h in units]
    vs = [p_ref[b, :, col(2 * RET_W, h)] for b, h in units]
    qs = [q * cosv + pltpu.roll(q, RET_DK // 2, 1) * sinv for q in qs]
    ks = [(k * cosv + pltpu.roll(k, RET_DK // 2, 1) * sinv) * (RET_DK ** -0.5) for k in ks]
    ss = [sret_ref[b, h] for b, h in units]
    att = [_mm_nt(qs[u], ks[u]) * jnp.where(diff >= 0, jnp.exp(lgs[u] * jnp.maximum(diff, 0.0)), 0.0)
           for u in range(nu)]
    inter = [_mm(qs[u] * jnp.exp(lgs[u] * (rowi + 1.0)), ss[u]) for u in range(nu)]
    kv = [_mm_tn(ks[u] * jnp.exp(lgs[u] * (c - 1.0 - rowi)), vs[u]) for u in range(nu)]
    outs = [_mm(att[u], vs[u]) + inter[u] for u in range(nu)]
    for u, (b, h) in enumerate(units):
        sret_ref[b, h] = ss[u] * math.exp(lgs[u] * c) + kv[u]
        o = outs[u]
        xc = o - jnp.mean(o, axis=-1, keepdims=True)
        oa = xc * lax.rsqrt(jnp.mean(xc * xc, axis=-1, keepdims=True) + EPS)
        y_ref[b, :, col(0, h)] = (oa * _silu(p_ref[b, :, col(3 * RET_W, h)])).astype(y_ref.dtype)

    lgt = lbl_ref[...]
    mx = jnp.max(lgt, axis=0, keepdims=True)
    ex = jnp.exp(lgt - mx)
    sm = ex / jnp.sum(ex, axis=0, keepdims=True)
    lb_all = jnp.sum(sm[0:layer + 1, :], axis=0, keepdims=True)
    base = 4 * RET_W
    mall = mall_ref[...]
    nl = len(HG_LEVELS)
    units = [(b, h) for b in range(nb) for h in range(HG_HEADS)]
    nu = len(units)
    lbs = [lb_all[:, col(0, h)] for _, h in units]
    qs = [p_ref[b, :, col(base, h)] for b, h in units]
    fbs = [p_ref[b, :, col(base + HG_W, h)] for b, h in units]
    vs = [p_ref[b, :, col(base + 2 * HG_W, h)] for b, h in units]
    ks = [(1.0 - lbs[u]) * _sigmoid(-fbs[u]) for u in range(nu)]
    xs = [jnp.exp(_cmm2(mall, jnp.log(lbs[u] + (1.0 - lbs[u]) * _sigmoid(fbs[u])))) for u in range(nu)]
    sts = [shg_ref[b, h] for b, h in units]
    att = [masks_ref[nl] * _mm_nt(qs[u], ks[u]) for u in range(nu)]
    for li in range(nl):
        lv = slice((2 + li) * c, (3 + li) * c)
        att = [att[u] + masks_ref[li] * _mm_nt(qs[u] * xs[u][lv, :], ks[u] * xs[u][lv, :]) for u in range(nu)]
    inter = [_mm_nt(qs[u] * xs[u][0:c, :], sts[u]) for u in range(nu)]
    kv = [_mm_tn(vs[u], ks[u] * xs[u][c:2 * c, :]) for u in range(nu)]
    outs = [_mm(att[u], vs[u]) + inter[u] for u in range(nu)]
    for u, (b, h) in enumerate(units):
        shg_ref[b, h] = sts[u] * xs[u][c - 1:c, :] + kv[u]
        ob = _rms(outs[u], hgain_ref[:, col(0, h)])
        y_ref[b, :, col(RET_W, h)] = (ob * _silu(p_ref[b, :, col(base + 3 * HG_W, h)])).astype(y_ref.dtype)


def _even_mixer(p, cos_t, sin_t, lb_logits, hg_gain, layer):
    b, lp, n = p.shape
    nc = lp // CHUNK
    nb = EVEN_NB if b % EVEN_NB == 0 else 1
    mall, masks = _hg_consts()
    const2 = lambda i, j: (0, 0)
    return pl.pallas_call(
        functools.partial(_even_body, layer=layer),
        grid=(b // nb, nc),
        in_specs=[
            pl.BlockSpec((nb, CHUNK, n), lambda i, j: (i, j, 0)),
            pl.BlockSpec((CHUNK, RET_DK), lambda i, j: (j, 0)),
            pl.BlockSpec((CHUNK, RET_DK), lambda i, j: (j, 0)),
            pl.BlockSpec(lb_logits.shape, const2),
            pl.BlockSpec((1, HG_W), const2),
            pl.BlockSpec(mall.shape, const2),
            pl.BlockSpec(masks.shape, lambda i, j: (0, 0, 0)),
        ],
        out_specs=pl.BlockSpec((nb, CHUNK, RET_W + HG_W), lambda i, j: (i, j, 0)),
        out_shape=jax.ShapeDtypeStruct((b, lp, RET_W + HG_W), BF),
        scratch_shapes=[pltpu.VMEM((nb, RET_HEADS, RET_DK, RET_DK), F32),
                        pltpu.VMEM((nb, HG_HEADS, HG_DK, HG_DK), F32)],
        compiler_params=pltpu.CompilerParams(
            dimension_semantics=("arbitrary", "arbitrary"), vmem_limit_bytes=VMEM_LIMIT),
        name="even_mixer",
    )(p, cos_t, sin_t, lb_logits, hg_gain.reshape(1, HG_W), jnp.asarray(mall, BF), jnp.asarray(masks))


def _merge_pieces(*lists):
    total = max(len(l) for l in lists)
    keyed = []
    for li, l in enumerate(lists):
        for k, piece in enumerate(l):
            keyed.append(((k + 0.5) * total / len(l), li, k, piece))
    keyed.sort(key=lambda t: t[:3])
    return [t[3] for t in keyed]


def _odd_body(p_ref, mu_ref, w0_ref, a0_ref, wa2_ref, g2_ref, kks_ref, kas_ref, rk_ref, lnw_ref, lnb_ref,
              cw_ref, alog_ref, dtb_ref, ggain_ref, tril_ref, bones_ref, y_ref,
              hrw_ref, sgd_ref, cpc_ref, ccv_ref):
    @pl.when(pl.program_id(1) == 0)
    def _():
        hrw_ref[...] = jnp.zeros_like(hrw_ref)
        sgd_ref[...] = jnp.zeros_like(sgd_ref)
        cpc_ref[...] = jnp.zeros_like(cpc_ref)
        ccv_ref[...] = jnp.zeros_like(ccv_ref)

    nb = p_ref.shape[0]
    c = CHUNK
    tril = tril_ref[...]
    bones = bones_ref[...]
    ii = lax.broadcasted_iota(jnp.int32, (c, c), 0)
    jj = lax.broadcasted_iota(jnp.int32, (c, c), 1)
    strict = ii > jj
    incl = ii >= jj
    eye = ii == jj
    eye_f = eye.astype(F32)
    ii2 = lax.broadcasted_iota(jnp.int32, (c, 2 * c), 0)
    lane2 = lax.broadcasted_iota(jnp.int32, (c, 2 * c), 1)
    jj2 = jnp.bitwise_and(lane2, c - 1)
    ak_mask = (ii2 > jj2) & (lane2 >= c)
    rbk_mask = ii2 >= jj2
    lane = lax.broadcasted_iota(jnp.int32, (c, 128), 1)
    heads = [slice(h * RW_HD, (h + 1) * RW_HD) for h in range(RW_HEADS)]
    base = RW_IN
    inv_hd = 1.0 / RW_HD

    def seg_sum(x):
        return jnp.concatenate(
            [_mmc2(x[:, p * 128:(p + 1) * 128], bones) for p in range(x.shape[1] // 128)], axis=1)

    def prep_pieces(b, d):
        def shift():
            pc = p_ref[b, :, 0:RW_IN]
            row = lax.broadcasted_iota(jnp.int32, (c, RW_IN), 0)
            prev = jnp.where(row == 0, cpc_ref[b, 7:8, :], pltpu.roll(pc, 1, 0))
            cpc_ref[b] = pc[c - 8:c, :]
            pcs = pc + (prev - pc) * mu_ref[...]
            d.update(r=pcs[:, 0:RW_W], k=pcs[:, RW_W:2 * RW_W], v=pcs[:, 2 * RW_W:3 * RW_W],
                     lo=pcs[:, 3 * RW_W:3 * RW_W + 128], glo=pcs[:, 3 * RW_W + 128:RW_IN])

        def lora():
            lo = d["lo"]
            wa = _mm(jnp.where(lane < RW_DECAY_LORA, jnp.tanh(lo), lo), wa2_ref[...])
            log_w = -_softplus(-(w0_ref[...] + wa[:, 0:RW_W])) - 0.5
            d["lw"] = -jnp.exp(log_w)
            d["a"] = _sigmoid(a0_ref[...] + wa[:, RW_W:2 * RW_W])
            d["gate"] = _mm(_sigmoid(d["glo"]), g2_ref[...])

        def keys():
            kks = d["k"] * kks_ref[...]
            kkn = kks * lax.rsqrt(seg_sum(kks * kks) + EPS)
            d["k2"] = d["k"] * (1.0 + (d["a"] - 1.0) * kas_ref[...])
            d["kkn"] = kkn
            d["beta"] = kkn * d["a"]
            d["c_inc"] = _cmm3(tril, d["lw"])

        def decays():
            c_inc, k2, beta = d["c_inc"], d["k2"], d["beta"]
            c_last = c_inc[c - 1:c, :]
            e_neg = jnp.exp(-c_inc)
            e_rest = jnp.exp(c_last - c_inc)
            d.update(ah=-d["kkn"] * jnp.exp(c_inc - d["lw"]), rh=d["r"] * jnp.exp(c_inc), bt=beta * e_neg,
                     kt=k2 * e_neg, bg=beta * e_rest, kg=k2 * e_rest, gc=jnp.exp(c_last),
                     ar=[], ls=[], ak=[], rbk=[], gcol=[], bgkg=[])

        def gram(h):
            def run():
                sl = heads[h]
                ar = jnp.concatenate([d["ah"][:, sl], d["rh"][:, sl]], axis=0).astype(BF)
                gm = _mm_nt(ar, jnp.concatenate([d["bt"][:, sl], d["kt"][:, sl]], axis=0))
                d["ar"].append(ar)
                d["ls"].append(jnp.where(strict, gm[0:c, 0:c], 0.0))
                d["ak"].append(jnp.where(ak_mask, gm[0:c, :], 0.0))
                d["rbk"].append(jnp.where(rbk_mask, gm[c:2 * c, :], 0.0))
                d["gcol"].append(jnp.sum(eye_f * d["gc"][:, sl], axis=1, keepdims=True))
                d["bgkg"].append(jnp.concatenate([d["bg"][:, sl], d["kg"][:, sl]], axis=0).astype(BF))
            return run

        def conv():
            x = p_ref[b, :, base:base + GDN_CONV_CH]
            rowc = lax.broadcasted_iota(jnp.int32, (c, GDN_CONV_CH), 0)
            xm1, xm2, xm3 = _shift_rows(x, ccv_ref.at[b], slice(None), GDN_CONV - 1, rowc)
            ccv_ref[b] = x[c - 8:c, :]
            d["qkv"] = _silu(xm3 * cw_ref[0:1, :] + xm2 * cw_ref[1:2, :] + xm1 * cw_ref[2:3, :] + x * cw_ref[3:4, :])

        def gates():
            sc = p_ref[b, :, base + GDN_CONV_CH + GDN_W:base + GDN_IN_PAD]
            g_all = -jnp.exp(alog_ref[...]) * _softplus(sc + dtb_ref[...])
            d["b_all"] = _sigmoid(sc)
            g_b = jnp.concatenate(
                [jnp.broadcast_to(g_all[:, h:h + 1], (c, 128)) for h in range(GDN_HEADS)], axis=1)
            d["cg_all"] = _cmm3(tril, g_b)
            d["gd"] = []

        def gdn_head(h):
            def run():
                hs = slice(h * 128, (h + 1) * 128)
                qkv = d["qkv"]
                q = qkv[:, hs]
                kd = qkv[:, GDN_W + h * 128:GDN_W + (h + 1) * 128]
                vd = qkv[:, 2 * GDN_W + h * 128:2 * GDN_W + (h + 1) * 128]
                q = q * lax.rsqrt(jnp.sum(q * q, axis=-1, keepdims=True) + EPS) * (GDN_DK ** -0.5)
                kd = kd * lax.rsqrt(jnp.sum(kd * kd, axis=-1, keepdims=True) + EPS)
                b_b = jnp.broadcast_to(d["b_all"][:, GDN_HEADS + h:GDN_HEADS + h + 1], (c, 128))
                cg = d["cg_all"][:, hs]
                cg_row = jnp.sum(jnp.where(eye, cg[:, 0:c], 0.0), axis=0, keepdims=True)
                decay = jnp.exp(jnp.where(incl, cg[:, 0:c] - cg_row, -jnp.inf))
                eg = jnp.exp(cg)
                cl = cg[c - 1:c, :]
                d["gd"].append(dict(
                    l=-jnp.where(strict, b_b[:, 0:c] * _mm_nt(kd, kd) * decay, 0.0),
                    rhs=jnp.concatenate([b_b * vd, b_b * kd * eg], axis=1), qk=_mm_nt(q, kd) * decay,
                    q_in=q * eg, k_out=kd * jnp.exp(cl - cg), sd=jnp.exp(cl)))
            return run

        return ([shift, lora, keys, decays] + [gram(h) for h in range(RW_HEADS)]
                + [conv, gates] + [gdn_head(h) for h in range(GDN_HEADS)])

    def inverse_pieces(rows, ds, out):
        st = {}

        def start():
            st["l"] = [l for b in rows for l in ds[b]["ls"]] + [g["l"] for b in rows for g in ds[b]["gd"]]
            st["p"] = [eye_f + l for l in st["l"]]

        def square():
            st["l"] = [_mm(l, l) for l in st["l"]]

        def extend():
            st["p"] = [p + _mm(l, p) for l, p in zip(st["l"], st["p"])]

        def finish():
            out["tinv"] = st["p"]

        levels = []
        span = 2
        while span < c:
            levels += [square, extend]
            span *= 2
        return [start] + levels + [finish]

    def state_pieces(rows, ds, inv):
        units = [(b, h) for b in rows for h in range(RW_HEADS)]
        gunits = [(b, h) for b in rows for h in range(GDN_HEADS)]
        st = {}

        def rw_read():
            st["h0"] = [hrw_ref[b, h] for b, h in units]
            st["hr"] = [_mm(ds[b]["ar"][h], st["h0"][u]) for u, (b, h) in enumerate(units)]
            st["vs"] = [ds[b]["v"][:, heads[h]] for b, h in units]

        def rw_mix():
            vs = st["vs"]
            st["x"] = [st["hr"][u][0:c, :] + _mm(ds[b]["ak"][h], jnp.concatenate([vs[u], vs[u]], axis=0))
                       for u, (b, h) in enumerate(units)]

        def rw_solve():
            us = [_mm(inv["tinv"][u], st["x"][u]) for u in range(len(units))]
            st["uv"] = [jnp.concatenate([us[u], st["vs"][u]], axis=0).astype(BF) for u in range(len(units))]

        def rw_out():
            st["ys"] = [st["hr"][u][c:2 * c, :] + _mm(ds[b]["rbk"][h], st["uv"][u])
                        for u, (b, h) in enumerate(units)]
            for u, (b, h) in enumerate(units):
                hrw_ref[b, h] = st["h0"][u] * ds[b]["gcol"][h] + _mm_tn(ds[b]["bgkg"][h], st["uv"][u])

        def rw_finish(k, b):
            def run():
                d = ds[b]
                y = jnp.concatenate(st["ys"][k * RW_HEADS:(k + 1) * RW_HEADS], axis=1)
                yc = y - seg_sum(y) * inv_hd
                yn = yc * lax.rsqrt(seg_sum(yc * yc) * inv_hd + RW_LNX_EPS) * lnw_ref[...] + lnb_ref[...]
                out = (yn + seg_sum(d["r"] * d["k2"] * rk_ref[...]) * d["v"]) * d["gate"]
                y_ref[b, :, 0:RW_W] = out.astype(y_ref.dtype)
            return run

        def gd_solve():
            n_rw = len(units)
            sols = [_mm(inv["tinv"][n_rw + u], ds[b]["gd"][h]["rhs"]) for u, (b, h) in enumerate(gunits)]
            st["s0"] = [sgd_ref[b, h] for b, h in gunits]
            st["v_new"] = [sols[u][:, 0:128] - _mm(sols[u][:, 128:256], st["s0"][u]) for u in range(len(gunits))]

        def gd_out(u, b, h):
            def run():
                g = ds[b]["gd"][h]
                s0, v_new = st["s0"][u], st["v_new"][u]
                o = _mm(g["q_in"], s0) + _mm(g["qk"], v_new)
                sgd_ref[b, h] = s0 * g["sd"] + _mm_tn(g["k_out"], v_new)
                og = p_ref[b, :, base + GDN_CONV_CH + h * 128:base + GDN_CONV_CH + (h + 1) * 128]
                out = _rms(o, ggain_ref[...]) * _silu(og)
                y_ref[b, :, RW_W + h * 128:RW_W + (h + 1) * 128] = out.astype(y_ref.dtype)
            return run

        return ([rw_read, rw_mix, rw_solve, rw_out] + [rw_finish(k, b) for k, b in enumerate(rows)]
                + [gd_solve] + [gd_out(u, b, h) for u, (b, h) in enumerate(gunits)])

    group = min(ODD_GROUP, nb)
    groups = [list(range(g, g + group)) for g in range(0, nb, group)]
    ds = [dict() for _ in range(nb)]
    invs = [dict() for _ in groups]
    stage1 = [_merge_pieces(*[prep_pieces(b, ds[b]) for b in rows]) for rows in groups]
    stage2 = [inverse_pieces(rows, ds, invs[g]) for g, rows in enumerate(groups)]
    stage3 = [state_pieces(rows, ds, invs[g]) for g, rows in enumerate(groups)]
    for slot in range(len(groups) + 2):
        active = []
        if slot < len(groups):
            active.append(stage1[slot])
        if 0 <= slot - 1 < len(groups):
            active.append(stage2[slot - 1])
        if 0 <= slot - 2 < len(groups):
            active.append(stage3[slot - 2])
        for piece in _merge_pieces(*active):
            piece()


def _odd_mixer(p, mu, w0, a0, wa2, g2, kks, kas, rk, lnw, lnb, conv_w, alog, dtb, ggain):
    b, lp, n = p.shape
    nc = lp // CHUNK
    nb = ODD_NB if b % ODD_NB == 0 else 1
    t = np.arange(CHUNK)
    tril = jnp.asarray(t[None, :] <= t[:, None], BF)
    l = np.arange(128)
    bones = jnp.asarray(l[:, None] // RW_HD == l[None, :] // RW_HD, BF)
    row = lambda a: a.reshape(1, -1)
    consts = [row(mu), row(w0), row(a0), wa2, g2, row(kks), row(kas), row(rk), row(lnw), row(lnb),
              conv_w, row(alog), row(dtb), row(ggain), tril, bones]
    const2 = lambda i, j: (0, 0)
    return pl.pallas_call(
        _odd_body,
        grid=(b // nb, nc),
        in_specs=[pl.BlockSpec((nb, CHUNK, n), lambda i, j: (i, j, 0))]
        + [pl.BlockSpec(a.shape, const2) for a in consts],
        out_specs=pl.BlockSpec((nb, CHUNK, RW_W + GDN_W), lambda i, j: (i, j, 0)),
        out_shape=jax.ShapeDtypeStruct((b, lp, RW_W + GDN_W), BF),
        scratch_shapes=[
            pltpu.VMEM((nb, RW_HEADS, RW_HD, RW_HD), F32),
            pltpu.VMEM((nb, GDN_HEADS, GDN_DK, GDN_DK), F32),
            pltpu.VMEM((nb, 8, RW_IN), F32),
            pltpu.VMEM((nb, 8, GDN_CONV_CH), F32),
        ],
        compiler_params=pltpu.CompilerParams(
            dimension_semantics=("arbitrary", "arbitrary"), vmem_limit_bytes=VMEM_LIMIT),
        name="odd_mixer",
    )(p, *consts)


def _row_tile(lp):
    best = 8
    for tm in range(8, min(lp, 704) + 1, 8):
        if lp % tm == 0:
            best = tm
    return best


def kernel(x, meta_tokens, norm_gains, w_in_even, w_out_even, hg_lb_logits, hg_norm_gain, w_in_odd, w_out_odd, rw_mu, rw_w0, rw_w2, rw_a0, rw_a2, rw_g2, rw_kk_scale, rw_ka_scale, rw_rk, rw_lnx_w, rw_lnx_b, gdn_conv_w, gdn_a_log, gdn_dt_bias, gdn_norm_gain, ffn_w_up, ffn_conv_w, ffn_conv_b, ffn_w_down):
    bsz, seq, d = x.shape
    depth = norm_gains.shape[0]
    l = N_META + seq
    pad = (-l) % CHUNK
    lp = l + pad
    tm = _row_tile(lp)
    meta = jnp.broadcast_to(meta_tokens[None].astype(x.dtype), (bsz, N_META, d))
    h = jnp.concatenate([jnp.zeros((bsz, pad, d), x.dtype), meta, x], axis=1)

    half = RET_DK // 2
    pos = (jnp.arange(lp, dtype=jnp.int32) - pad).astype(F32)
    inv = ROPE_BASE ** (-jnp.arange(half, dtype=F32) / half)
    ang = pos[:, None] * inv[None, :]
    cos_t = jnp.concatenate([jnp.cos(ang), jnp.cos(ang)], axis=1)
    sin_t = jnp.concatenate([-jnp.sin(ang), jnp.sin(ang)], axis=1)

    for layer in range(depth):
        g = norm_gains[layer]
        i = layer // 2
        if layer % 2 == 0:
            p = _norm_proj(h, g[0], w_in_even[i].astype(BF), tm)
            y = _even_mixer(p, cos_t, sin_t, hg_lb_logits, hg_norm_gain[i], layer)
            w_out = w_out_even[i]
        else:
            w_in = jnp.pad(w_in_odd[i], ((0, 0), (0, ODD_IN_PAD - w_in_odd.shape[2])))
            p = _norm_proj(h, g[0], w_in.astype(BF), tm)
            wa2 = jnp.zeros((RW_DECAY_LORA + RW_AAA_LORA, 2 * RW_W), F32)
            wa2 = wa2.at[:RW_DECAY_LORA, :RW_W].set(rw_w2[i]).at[RW_DECAY_LORA:, RW_W:].set(rw_a2[i])
            lane_pad = lambda a: jnp.pad(a, (0, LANES - a.shape[0]))
            y = _odd_mixer(p, rw_mu[i], rw_w0[i], rw_a0[i], wa2.astype(BF), rw_g2[i].astype(BF),
                           rw_kk_scale[i], rw_ka_scale[i], rw_rk[i].reshape(-1), rw_lnx_w[i], rw_lnx_b[i],
                           gdn_conv_w[i], lane_pad(gdn_a_log[i]), lane_pad(gdn_dt_bias[i]), gdn_norm_gain[i])
            w_out = w_out_odd[i]
        h = _out_res(y, w_out.astype(BF), g[1], h, tm, pad)
        h = _ffn(h, g[2], ffn_w_up[layer].astype(BF), ffn_conv_w[layer], ffn_conv_b[layer],
                 ffn_w_down[layer].astype(BF), g[3], tm, pad)
    return h[:, pad + N_META:]
```

```python
import functools
import math

import numpy as np
import jax
import jax.numpy as jnp
from jax import lax
from jax.experimental import pallas as pl
from jax.experimental.pallas import tpu as pltpu

F32 = jnp.float32
BF = jnp.bfloat16

D_MODEL = 1024
CHUNK = 64
N_META = 16
EPS = 1e-6
ROPE_BASE = 10000.0

RET_HEADS = D_MODEL // 256
RET_DK = 128
HG_HEADS = D_MODEL // 256
HG_DK = 128
RW_HEADS = D_MODEL // 128
RW_HD = 64
RW_DECAY_LORA = 64
RW_AAA_LORA = 64
RW_GATE_LORA = 128
RW_LNX_EPS = 64e-5
GDN_HEADS = D_MODEL // 256
GDN_DK = 128
GDN_CONV = 4
D_FF = 128 * ((8 * D_MODEL // 3 + 127) // 128)
FFN_CONV = 3

RET_W = RET_HEADS * RET_DK
HG_W = HG_HEADS * HG_DK
EVEN_IN = 4 * RET_W + 4 * HG_W
RW_W = RW_HEADS * RW_HD
RW_IN = 3 * RW_W + RW_DECAY_LORA + RW_AAA_LORA + RW_GATE_LORA
GDN_W = GDN_HEADS * GDN_DK
GDN_CONV_CH = 3 * GDN_W
GDN_IN = GDN_CONV_CH + GDN_W + 2 * GDN_HEADS
LANES = 128
GDN_IN_PAD = LANES * ((GDN_IN + LANES - 1) // LANES)
ODD_IN_PAD = RW_IN + GDN_IN_PAD
HG_LEVELS = (32, 16, 8, 4, 2, 1)
VMEM_LIMIT = 56 * 1024 * 1024
FFN_TILE = 256
EVEN_NB = 4
ODD_NB = 4
ODD_GROUP = 2


def _dot(a, b):
    return jnp.dot(a, b, preferred_element_type=F32)


def _mm(a, b):
    return _dot(a.astype(BF), b.astype(BF))


def _mm_nt(a, b):
    return lax.dot_general(a.astype(BF), b.astype(BF), (((1,), (1,)), ((), ())), preferred_element_type=F32)


def _mm_tn(a, b):
    return lax.dot_general(a.astype(BF), b.astype(BF), (((0,), (0,)), ((), ())), preferred_element_type=F32)


def _split2(x):
    hi = x.astype(BF)
    lo = (x - hi.astype(F32)).astype(BF)
    return hi, lo


def _split3(x):
    hi = x.astype(BF)
    r1 = x - hi.astype(F32)
    mid = r1.astype(BF)
    lo = (r1 - mid.astype(F32)).astype(BF)
    return hi, mid, lo


def _cmm3(c, x):
    hi, mid, lo = _split3(x)
    return _dot(c, hi) + _dot(c, mid) + _dot(c, lo)


def _cmm2(c, x):
    hi, lo = _split2(x)
    return _dot(c, hi) + _dot(c, lo)


def _mmc2(x, c):
    hi, lo = _split2(x)
    return _dot(hi, c) + _dot(lo, c)


def _sigmoid(x):
    return 1.0 / (1.0 + jnp.exp(-x))


def _silu(x):
    return x * _sigmoid(x)


def _softplus(x):
    return jnp.maximum(x, 0.0) + jnp.log1p(jnp.exp(-jnp.abs(x)))


def _rms(x, g):
    return x * lax.rsqrt(jnp.mean(x * x, axis=-1, keepdims=True) + EPS) * g


def _norm_proj_body(h_ref, g_ref, w_ref, o_ref):
    u = _rms(h_ref[...], g_ref[...])
    o_ref[...] = _dot(u.astype(BF), w_ref[...])


def _norm_proj(h, gain, w, tm):
    b, lp, d = h.shape
    n = w.shape[1]
    return pl.pallas_call(
        _norm_proj_body,
        grid=(b, lp // tm),
        in_specs=[
            pl.BlockSpec((None, tm, d), lambda i, j: (i, j, 0)),
            pl.BlockSpec((1, d), lambda i, j: (0, 0)),
            pl.BlockSpec((d, n), lambda i, j: (0, 0), pipeline_mode=pl.Buffered(1)),
        ],
        out_specs=pl.BlockSpec((None, tm, n), lambda i, j: (i, j, 0)),
        out_shape=jax.ShapeDtypeStruct((b, lp, n), F32),
        compiler_params=pltpu.CompilerParams(
            dimension_semantics=("arbitrary", "arbitrary"), vmem_limit_bytes=VMEM_LIMIT),
        name="norm_proj",
    )(h, gain.reshape(1, d), w)


def _embed_proj_body(x_ref, meta_ref, g_ref, w_ref, p_ref, h_ref, *, lead):
    xb = x_ref[0]
    tm, d = xb.shape
    n_zero = lead - meta_ref.shape[0]
    first = jnp.concatenate([jnp.zeros((n_zero, d), F32), meta_ref[...], xb[0:tm - lead, :]], axis=0)
    blk = jnp.where(pl.program_id(1) == 0, first, xb)
    h_ref[...] = blk
    p_ref[...] = _dot(_rms(blk, g_ref[...]).astype(BF), w_ref[...])


def _embed_proj(x, meta, gain, w, tm, lead):
    b, s, d = x.shape
    lp = s + lead
    n = w.shape[1]
    assert lead % 8 == 0 and meta.shape[0] % 8 == 0 and lead < tm and lp % tm == 0
    x_rows = lambda i, j: (i, pl.multiple_of(jnp.maximum(tm * j - lead, 0), 8), 0)
    return pl.pallas_call(
        functools.partial(_embed_proj_body, lead=lead),
        grid=(b, lp // tm),
        in_specs=[
            pl.BlockSpec((pl.Element(1), pl.Element(tm), pl.Element(d)), x_rows),
            pl.BlockSpec(meta.shape, lambda i, j: (0, 0)),
            pl.BlockSpec((1, d), lambda i, j: (0, 0)),
            pl.BlockSpec((d, n), lambda i, j: (0, 0), pipeline_mode=pl.Buffered(1)),
        ],
        out_specs=[pl.BlockSpec((None, tm, n), lambda i, j: (i, j, 0)),
                   pl.BlockSpec((None, tm, d), lambda i, j: (i, j, 0))],
        out_shape=[jax.ShapeDtypeStruct((b, lp, n), F32), jax.ShapeDtypeStruct((b, lp, d), F32)],
        compiler_params=pltpu.CompilerParams(
            dimension_semantics=("arbitrary", "arbitrary"), vmem_limit_bytes=VMEM_LIMIT),
        name="embed_proj",
    )(x, meta, gain.reshape(1, d), w)


def _shift_rows(a, carry_ref, cols, n_back, row):
    out = []
    for s in range(1, n_back + 1):
        sh = pltpu.roll(a, s, 0)
        for r in range(s):
            sh = jnp.where(row == r, carry_ref[8 - s + r:9 - s + r, cols], sh)
        out.append(sh)
    return out


def _conv_ffn_rows(x, shift, g2_ref, wup_ref, cw_ref, cb_ref, wdn_ref, g3_ref):
    u = _rms(x, g2_ref[...]).astype(BF)
    n_tiles = D_FF // FFN_TILE

    def up(c):
        return [_dot(u, wup_ref[:, part * D_FF + c * FFN_TILE:part * D_FF + (c + 1) * FFN_TILE])
                for part in range(2)]

    acc = jnp.zeros(x.shape, F32)
    pending = [up(c) for c in range(min(2, n_tiles))]
    for c in range(n_tiles):
        cur = pending.pop(0)
        if c + 2 < n_tiles:
            pending.append(up(c + 2))
        z = []
        for part in range(2):
            cols = slice(part * D_FF + c * FFN_TILE, part * D_FF + (c + 1) * FFN_TILE)
            a = cur[part]
            am1, am2 = shift(a, cols)
            z.append(am2 * cw_ref[0:1, cols] + am1 * cw_ref[1:2, cols] + a * cw_ref[2:3, cols] + cb_ref[:, cols])
        act = _silu(z[0]) * z[1]
        acc = acc + _dot(act.astype(BF), wdn_ref[c * FFN_TILE:(c + 1) * FFN_TILE, :])
    return x + _rms(acc, g3_ref[...])


def _ffn_body(y_ref, wout_ref, g1_ref, h_ref, g2_ref, wup_ref, cw_ref, cb_ref, wdn_ref, g3_ref, o_ref, carry_ref,
              *, tm, pad):
    @pl.when(pl.program_id(1) == 0)
    def _():
        carry_ref[...] = jnp.zeros_like(carry_ref)

    t = pl.program_id(1) * tm + lax.broadcasted_iota(jnp.int32, (tm, D_MODEL), 0)
    x = jnp.where(t >= pad, h_ref[...] + _rms(_dot(y_ref[...], wout_ref[...]), g1_ref[...]), 0.0)
    row = lax.broadcasted_iota(jnp.int32, (tm, FFN_TILE), 0)

    def shift(a, cols):
        am1, am2 = _shift_rows(a, carry_ref, cols, FFN_CONV - 1, row)
        carry_ref[:, cols] = a[tm - 8:tm, :]
        return am1, am2

    out = _conv_ffn_rows(x, shift, g2_ref, wup_ref, cw_ref, cb_ref, wdn_ref, g3_ref)
    o_ref[...] = jnp.where(t >= pad, out, 0.0)


def _ffn_tail_body(y_ref, wout_ref, g1_ref, h_ref, g2_ref, wup_ref, cw_ref, cb_ref, wdn_ref, g3_ref, o_ref, *,
                   halo):
    x = h_ref[0] + _rms(_dot(y_ref[0], wout_ref[...]), g1_ref[...])
    shift = lambda a, cols: (pltpu.roll(a, 1, 0), pltpu.roll(a, 2, 0))
    out = _conv_ffn_rows(x, shift, g2_ref, wup_ref, cw_ref, cb_ref, wdn_ref, g3_ref)
    o_ref[...] = out[halo:, :]


def _ffn_weight_specs(k, d, ff2):
    const = lambda i, j: (0, 0)
    return dict(
        w_out=pl.BlockSpec((k, d), const, pipeline_mode=pl.Buffered(1)), gain=pl.BlockSpec((1, d), const),
        w_up=pl.BlockSpec((d, ff2), const, pipeline_mode=pl.Buffered(1)),
        conv_w=pl.BlockSpec((FFN_CONV, ff2), const), conv_b=pl.BlockSpec((1, ff2), const),
        w_down=pl.BlockSpec((ff2 // 2, d), const, pipeline_mode=pl.Buffered(1)))


def _out_ffn(y, w_out, g1, h, g2, w_up, conv_w, conv_b, w_down, g3, tm, pad):
    b, lp, d = h.shape
    k = y.shape[2]
    ff2 = w_up.shape[1]
    ws = _ffn_weight_specs(k, d, ff2)
    rows = lambda width: pl.BlockSpec((None, tm, width), lambda i, j: (i, j, 0))
    return pl.pallas_call(
        functools.partial(_ffn_body, tm=tm, pad=pad),
        grid=(b, lp // tm),
        in_specs=[rows(k), ws["w_out"], ws["gain"], rows(d), ws["gain"], ws["w_up"], ws["conv_w"], ws["conv_b"],
                  ws["w_down"], ws["gain"]],
        out_specs=rows(d),
        out_shape=jax.ShapeDtypeStruct((b, lp, d), F32),
        scratch_shapes=[pltpu.VMEM((8, ff2), F32)],
        compiler_params=pltpu.CompilerParams(
            dimension_semantics=("arbitrary", "arbitrary"), vmem_limit_bytes=VMEM_LIMIT),
        name="conv_ffn",
    )(y, w_out, g1.reshape(1, d), h, g2.reshape(1, d), w_up, conv_w, conv_b.reshape(1, ff2), w_down,
      g3.reshape(1, d))


def _out_ffn_tail(y, w_out, g1, h, g2, w_up, conv_w, conv_b, w_down, g3, tm, lead):
    b, lp, d = h.shape
    seq = lp - lead
    k = y.shape[2]
    ff2 = w_up.shape[1]
    halo = 8
    assert seq % tm == 0 and lead >= halo and lead % 8 == 0
    ws = _ffn_weight_specs(k, d, ff2)
    rows = lambda width: pl.BlockSpec(
        (pl.Element(1), pl.Element(tm + halo), pl.Element(width)),
        lambda i, j: (i, pl.multiple_of(lead - halo + tm * j, 8), 0))
    return pl.pallas_call(
        functools.partial(_ffn_tail_body, halo=halo),
        grid=(b, seq // tm),
        in_specs=[rows(k), ws["w_out"], ws["gain"], rows(d), ws["gain"], ws["w_up"], ws["conv_w"], ws["conv_b"],
                  ws["w_down"], ws["gain"]],
        out_specs=pl.BlockSpec((None, tm, d), lambda i, j: (i, j, 0)),
        out_shape=jax.ShapeDtypeStruct((b, seq, d), F32),
        compiler_params=pltpu.CompilerParams(
            dimension_semantics=("arbitrary", "arbitrary"), vmem_limit_bytes=VMEM_LIMIT),
        name="conv_ffn_tail",
    )(y, w_out, g1.reshape(1, d), h, g2.reshape(1, d), w_up, conv_w, conv_b.reshape(1, ff2), w_down,
      g3.reshape(1, d))


def _hg_consts():
    c = CHUNK
    t = np.arange(c)
    m = np.zeros((2 + len(HG_LEVELS), c, c), np.float32)
    m[0] = t[None, :] <= t[:, None]
    m[1] = t[None, :] > t[:, None]
    masks = np.zeros((len(HG_LEVELS) + 1, c, c), np.float32)
    ii, jj = t[:, None], t[None, :]
    for li, s in enumerate(HG_LEVELS):
        for i in range(c):
            mid = (i // (2 * s)) * (2 * s) + s - 1
            if i % (2 * s) >= s:
                m[2 + li, i, mid + 1:i + 1] = 1.0
            else:
                m[2 + li, i, i + 1:mid + 1] = 1.0
        masks[li] = (ii // (2 * s) == jj // (2 * s)) & (ii % (2 * s) >= s) & (jj % (2 * s) < s)
    masks[-1] = ii == jj
    return m.reshape(-1, c), masks


def _even_body(p_ref, cos_ref, sin_ref, lbl_ref, hgain_ref, mall_ref, masks_ref, y_ref, sret_ref, shg_ref, *,
               layer):
    @pl.when(pl.program_id(1) == 0)
    def _():
        sret_ref[...] = jnp.zeros_like(sret_ref)
        shg_ref[...] = jnp.zeros_like(shg_ref)

    c = CHUNK
    cosv = cos_ref[...]
    sinv = sin_ref[...]
    ii = lax.broadcasted_iota(jnp.int32, (c, c), 0)
    jj = lax.broadcasted_iota(jnp.int32, (c, c), 1)
    diff = (ii - jj).astype(F32)
    rowi = lax.broadcasted_iota(jnp.int32, (c, RET_DK), 0).astype(F32)

    nb = p_ref.shape[0]
    col = lambda base, h: slice(base + h * 128, base + (h + 1) * 128)
    units = [(b, h) for b in range(nb) for h in range(RET_HEADS)]
    nu = len(units)
    lgs = [math.log1p(-(2.0 ** (-5.0 - h))) for _, h in units]
    qs = [p_ref[b, :, col(0, h)] for b, h in units]
    ks = [p_ref[b, :, col(RET_W, h)] for b, h in units]
    vs = [p_ref[b, :, col(2 * RET_W, h)] for b, h in units]
    qs = [q * cosv + pltpu.roll(q, RET_DK // 2, 1) * sinv for q in qs]
    ks = [(k * cosv + pltpu.roll(k, RET_DK // 2, 1) * sinv) * (RET_DK ** -0.5) for k in ks]
    ss = [sret_ref[b, h] for b, h in units]
    att = [_mm_nt(qs[u], ks[u]) * jnp.where(diff >= 0, jnp.exp(lgs[u] * jnp.maximum(diff, 0.0)), 0.0)
           for u in range(nu)]
    inter = [_mm(qs[u] * jnp.exp(lgs[u] * (rowi + 1.0)), ss[u]) for u in range(nu)]
    kv = [_mm_tn(ks[u] * jnp.exp(lgs[u] * (c - 1.0 - rowi)), vs[u]) for u in range(nu)]
    outs = [_mm(att[u], vs[u]) + inter[u] for u in range(nu)]
    for u, (b, h) in enumerate(units):
        sret_ref[b, h] = ss[u] * math.exp(lgs[u] * c) + kv[u]
        o = outs[u]
        xc = o - jnp.mean(o, axis=-1, keepdims=True)
        oa = xc * lax.rsqrt(jnp.mean(xc * xc, axis=-1, keepdims=True) + EPS)
        y_ref[b, :, col(0, h)] = (oa * _silu(p_ref[b, :, col(3 * RET_W, h)])).astype(y_ref.dtype)

    lgt = lbl_ref[...]
    mx = jnp.max(lgt, axis=0, keepdims=True)
    ex = jnp.exp(lgt - mx)
    sm = ex / jnp.sum(ex, axis=0, keepdims=True)
    lb_all = jnp.sum(sm[0:layer + 1, :], axis=0, keepdims=True)
    base = 4 * RET_W
    mall = mall_ref[...]
    nl = len(HG_LEVELS)
    units = [(b, h) for b in range(nb) for h in range(HG_HEADS)]
    nu = len(units)
    lbs = [lb_all[:, col(0, h)] for _, h in units]
    qs = [p_ref[b, :, col(base, h)] for b, h in units]
    fbs = [p_ref[b, :, col(base + HG_W, h)] for b, h in units]
    vs = [p_ref[b, :, col(base + 2 * HG_W, h)] for b, h in units]
    ks = [(1.0 - lbs[u]) * _sigmoid(-fbs[u]) for u in range(nu)]
    xs = [jnp.exp(_cmm2(mall, jnp.log(lbs[u] + (1.0 - lbs[u]) * _sigmoid(fbs[u])))) for u in range(nu)]
    sts = [shg_ref[b, h] for b, h in units]
    att = [masks_ref[nl] * _mm_nt(qs[u], ks[u]) for u in range(nu)]
    for li in range(nl):
        lv = slice((2 + li) * c, (3 + li) * c)
        att = [att[u] + masks_ref[li] * _mm_nt(qs[u] * xs[u][lv, :], ks[u] * xs[u][lv, :]) for u in range(nu)]
    inter = [_mm_nt(qs[u] * xs[u][0:c, :], sts[u]) for u in range(nu)]
    kv = [_mm_tn(vs[u], ks[u] * xs[u][c:2 * c, :]) for u in range(nu)]
    outs = [_mm(att[u], vs[u]) + inter[u] for u in range(nu)]
    for u, (b, h) in enumerate(units):
        shg_ref[b, h] = sts[u] * xs[u][c - 1:c, :] + kv[u]
        ob = _rms(outs[u], hgain_ref[:, col(0, h)])
        y_ref[b, :, col(RET_W, h)] = (ob * _silu(p_ref[b, :, col(base + 3 * HG_W, h)])).astype(y_ref.dtype)


def _even_mixer(p, cos_t, sin_t, lb_logits, hg_gain, layer):
    b, lp, n = p.shape
    nc = lp // CHUNK
    nb = EVEN_NB if b % EVEN_NB == 0 else 1
    mall, masks = _hg_consts()
    const2 = lambda i, j: (0, 0)
    return pl.pallas_call(
        functools.partial(_even_body, layer=layer),
        grid=(b // nb, nc),
        in_specs=[
            pl.BlockSpec((nb, CHUNK, n), lambda i, j: (i, j, 0)),
            pl.BlockSpec((CHUNK, RET_DK), lambda i, j: (j, 0)),
            pl.BlockSpec((CHUNK, RET_DK), lambda i, j: (j, 0)),
            pl.BlockSpec(lb_logits.shape, const2),
            pl.BlockSpec((1, HG_W), const2),
            pl.BlockSpec(mall.shape, const2),
            pl.BlockSpec(masks.shape, lambda i, j: (0, 0, 0)),
        ],
        out_specs=pl.BlockSpec((nb, CHUNK, RET_W + HG_W), lambda i, j: (i, j, 0)),
        out_shape=jax.ShapeDtypeStruct((b, lp, RET_W + HG_W), BF),
        scratch_shapes=[pltpu.VMEM((nb, RET_HEADS, RET_DK, RET_DK), F32),
                        pltpu.VMEM((nb, HG_HEADS, HG_DK, HG_DK), F32)],
        compiler_params=pltpu.CompilerParams(
            dimension_semantics=("arbitrary", "arbitrary"), vmem_limit_bytes=VMEM_LIMIT),
        name="even_mixer",
    )(p, cos_t, sin_t, lb_logits, hg_gain.reshape(1, HG_W), jnp.asarray(mall, BF), jnp.asarray(masks))


def _merge_pieces(*lists):
    total = max(len(l) for l in lists)
    keyed = []
    for li, l in enumerate(lists):
        for k, piece in enumerate(l):
            keyed.append(((k + 0.5) * total / len(l), li, k, piece))
    keyed.sort(key=lambda t: t[:3])
    return [t[3] for t in keyed]


def _odd_body(p_ref, mu_ref, w0_ref, a0_ref, wa2_ref, g2_ref, kks_ref, kas_ref, rk_ref, lnw_ref, lnb_ref,
              cw_ref, alog_ref, dtb_ref, ggain_ref, tril_ref, bones_ref, y_ref,
              hrw_ref, sgd_ref, cpc_ref, ccv_ref):
    @pl.when(pl.program_id(1) == 0)
    def _():
        hrw_ref[...] = jnp.zeros_like(hrw_ref)
        sgd_ref[...] = jnp.zeros_like(sgd_ref)
        cpc_ref[...] = jnp.zeros_like(cpc_ref)
        ccv_ref[...] = jnp.zeros_like(ccv_ref)

    nb = p_ref.shape[0]
    c = CHUNK
    tril = tril_ref[...]
    bones = bones_ref[...]
    ii = lax.broadcasted_iota(jnp.int32, (c, c), 0)
    jj = lax.broadcasted_iota(jnp.int32, (c, c), 1)
    strict = ii > jj
    incl = ii >= jj
    eye = ii == jj
    eye_f = eye.astype(F32)
    ii2 = lax.broadcasted_iota(jnp.int32, (c, 2 * c), 0)
    lane2 = lax.broadcasted_iota(jnp.int32, (c, 2 * c), 1)
    jj2 = jnp.bitwise_and(lane2, c - 1)
    ak_mask = (ii2 > jj2) & (lane2 >= c)
    rbk_mask = ii2 >= jj2
    lane = lax.broadcasted_iota(jnp.int32, (c, 128), 1)
    heads = [slice(h * RW_HD, (h + 1) * RW_HD) for h in range(RW_HEADS)]
    base = RW_IN
    inv_hd = 1.0 / RW_HD

    def seg_sum(x):
        return jnp.concatenate(
            [_mmc2(x[:, p * 128:(p + 1) * 128], bones) for p in range(x.shape[1] // 128)], axis=1)

    def prep_pieces(b, d):
        def shift():
            pc = p_ref[b, :, 0:RW_IN]
            row = lax.broadcasted_iota(jnp.int32, (c, RW_IN), 0)
            prev = jnp.where(row == 0, cpc_ref[b, 7:8, :], pltpu.roll(pc, 1, 0))
            cpc_ref[b] = pc[c - 8:c, :]
            pcs = pc + (prev - pc) * mu_ref[...]
            d.update(r=pcs[:, 0:RW_W], k=pcs[:, RW_W:2 * RW_W], v=pcs[:, 2 * RW_W:3 * RW_W],
                     lo=pcs[:, 3 * RW_W:3 * RW_W + 128], glo=pcs[:, 3 * RW_W + 128:RW_IN])

        def lora():
            lo = d["lo"]
            wa = _mm(jnp.where(lane < RW_DECAY_LORA, jnp.tanh(lo), lo), wa2_ref[...])
            log_w = -_softplus(-(w0_ref[...] + wa[:, 0:RW_W])) - 0.5
            d["lw"] = -jnp.exp(log_w)
            d["a"] = _sigmoid(a0_ref[...] + wa[:, RW_W:2 * RW_W])
            d["gate"] = _mm(_sigmoid(d["glo"]), g2_ref[...])

        def keys():
            kks = d["k"] * kks_ref[...]
            kkn = kks * lax.rsqrt(seg_sum(kks * kks) + EPS)
            d["k2"] = d["k"] * (1.0 + (d["a"] - 1.0) * kas_ref[...])
            d["kkn"] = kkn
            d["beta"] = kkn * d["a"]
            d["c_inc"] = _cmm3(tril, d["lw"])

        def decays():
            c_inc, k2, beta = d["c_inc"], d["k2"], d["beta"]
            c_last = c_inc[c - 1:c, :]
            e_neg = jnp.exp(-c_inc)
            e_rest = jnp.exp(c_last - c_inc)
            d.update(ah=-d["kkn"] * jnp.exp(c_inc - d["lw"]), rh=d["r"] * jnp.exp(c_inc), bt=beta * e_neg,
                     kt=k2 * e_neg, bg=beta * e_rest, kg=k2 * e_rest, gc=jnp.exp(c_last),
                     ar=[], ls=[], ak=[], rbk=[], gcol=[], bgkg=[])

        def gram(h):
            def run():
                sl = heads[h]
                ar = jnp.concatenate([d["ah"][:, sl], d["rh"][:, sl]], axis=0).astype(BF)
                gm = _mm_nt(ar, jnp.concatenate([d["bt"][:, sl], d["kt"][:, sl]], axis=0))
                d["ar"].append(ar)
                d["ls"].append(jnp.where(strict, gm[0:c, 0:c], 0.0))
                d["ak"].append(jnp.where(ak_mask, gm[0:c, :], 0.0))
                d["rbk"].append(jnp.where(rbk_mask, gm[c:2 * c, :], 0.0))
                d["gcol"].append(jnp.sum(eye_f * d["gc"][:, sl], axis=1, keepdims=True))
                d["bgkg"].append(jnp.concatenate([d["bg"][:, sl], d["kg"][:, sl]], axis=0).astype(BF))
            return run

        def conv():
            x = p_ref[b, :, base:base + GDN_CONV_CH]
            rowc = lax.broadcasted_iota(jnp.int32, (c, GDN_CONV_CH), 0)
            xm1, xm2, xm3 = _shift_rows(x, ccv_ref.at[b], slice(None), GDN_CONV - 1, rowc)
            ccv_ref[b] = x[c - 8:c, :]
            d["qkv"] = _silu(xm3 * cw_ref[0:1, :] + xm2 * cw_ref[1:2, :] + xm1 * cw_ref[2:3, :] + x * cw_ref[3:4, :])

        def gates():
            sc = p_ref[b, :, base + GDN_CONV_CH + GDN_W:base + GDN_IN_PAD]
            g_all = -jnp.exp(alog_ref[...]) * _softplus(sc + dtb_ref[...])
            d["b_all"] = _sigmoid(sc)
            g_b = jnp.concatenate(
                [jnp.broadcast_to(g_all[:, h:h + 1], (c, 128)) for h in range(GDN_HEADS)], axis=1)
            d["cg_all"] = _cmm3(tril, g_b)
            d["gd"] = []

        def gdn_head(h):
            def run():
                hs = slice(h * 128, (h + 1) * 128)
                qkv = d["qkv"]
                q = qkv[:, hs]
                kd = qkv[:, GDN_W + h * 128:GDN_W + (h + 1) * 128]
                vd = qkv[:, 2 * GDN_W + h * 128:2 * GDN_W + (h + 1) * 128]
                q = q * lax.rsqrt(jnp.sum(q * q, axis=-1, keepdims=True) + EPS) * (GDN_DK ** -0.5)
                kd = kd * lax.rsqrt(jnp.sum(kd * kd, axis=-1, keepdims=True) + EPS)
                b_b = jnp.broadcast_to(d["b_all"][:, GDN_HEADS + h:GDN_HEADS + h + 1], (c, 128))
                cg = d["cg_all"][:, hs]
                cg_row = jnp.sum(jnp.where(eye, cg[:, 0:c], 0.0), axis=0, keepdims=True)
                decay = jnp.exp(jnp.where(incl, cg[:, 0:c] - cg_row, -jnp.inf))
                eg = jnp.exp(cg)
                cl = cg[c - 1:c, :]
                d["gd"].append(dict(
                    l=-jnp.where(strict, b_b[:, 0:c] * _mm_nt(kd, kd) * decay, 0.0),
                    rhs=jnp.concatenate([b_b * vd, b_b * kd * eg], axis=1), qk=_mm_nt(q, kd) * decay,
                    q_in=q * eg, k_out=kd * jnp.exp(cl - cg), sd=jnp.exp(cl)))
            return run

        return ([shift, lora, keys, decays] + [gram(h) for h in range(RW_HEADS)]
                + [conv, gates] + [gdn_head(h) for h in range(GDN_HEADS)])

    def inverse_pieces(rows, ds, out):
        st = {}

        def start():
            st["l"] = [l for b in rows for l in ds[b]["ls"]] + [g["l"] for b in rows for g in ds[b]["gd"]]
            st["p"] = [eye_f + l for l in st["l"]]

        def square():
            st["l"] = [_mm(l, l) for l in st["l"]]

        def extend():
            st["p"] = [p + _mm(l, p) for l, p in zip(st["l"], st["p"])]

        def finish():
            out["tinv"] = st["p"]

        levels = []
        span = 2
        while span < c:
            levels += [square, extend]
            span *= 2
        return [start] + levels + [finish]

    def state_pieces(rows, ds, inv):
        units = [(b, h) for b in rows for h in range(RW_HEADS)]
        gunits = [(b, h) for b in rows for h in range(GDN_HEADS)]
        st = {}

        def rw_read():
            st["h0"] = [hrw_ref[b, h] for b, h in units]
            st["hr"] = [_mm(ds[b]["ar"][h], st["h0"][u]) for u, (b, h) in enumerate(units)]
            st["vs"] = [ds[b]["v"][:, heads[h]] for b, h in units]

        def rw_mix():
            vs = st["vs"]
            st["x"] = [st["hr"][u][0:c, :] + _mm(ds[b]["ak"][h], jnp.concatenate([vs[u], vs[u]], axis=0))
                       for u, (b, h) in enumerate(units)]

        def rw_solve():
            us = [_mm(inv["tinv"][u], st["x"][u]) for u in range(len(units))]
            st["uv"] = [jnp.concatenate([us[u], st["vs"][u]], axis=0).astype(BF) for u in range(len(units))]

        def rw_out():
            st["ys"] = [st["hr"][u][c:2 * c, :] + _mm(ds[b]["rbk"][h], st["uv"][u])
                        for u, (b, h) in enumerate(units)]
            for u, (b, h) in enumerate(units):
                hrw_ref[b, h] = st["h0"][u] * ds[b]["gcol"][h] + _mm_tn(ds[b]["bgkg"][h], st["uv"][u])

        def rw_finish(k, b):
            def run():
                d = ds[b]
                y = jnp.concatenate(st["ys"][k * RW_HEADS:(k + 1) * RW_HEADS], axis=1)
                yc = y - seg_sum(y) * inv_hd
                yn = yc * lax.rsqrt(seg_sum(yc * yc) * inv_hd + RW_LNX_EPS) * lnw_ref[...] + lnb_ref[...]
                out = (yn + seg_sum(d["r"] * d["k2"] * rk_ref[...]) * d["v"]) * d["gate"]
                y_ref[b, :, 0:RW_W] = out.astype(y_ref.dtype)
            return run

        def gd_solve():
            n_rw = len(units)
            sols = [_mm(inv["tinv"][n_rw + u], ds[b]["gd"][h]["rhs"]) for u, (b, h) in enumerate(gunits)]
            st["s0"] = [sgd_ref[b, h] for b, h in gunits]
            st["v_new"] = [sols[u][:, 0:128] - _mm(sols[u][:, 128:256], st["s0"][u]) for u in range(len(gunits))]

        def gd_out(u, b, h):
            def run():
                g = ds[b]["gd"][h]
                s0, v_new = st["s0"][u], st["v_new"][u]
                o = _mm(g["q_in"], s0) + _mm(g["qk"], v_new)
                sgd_ref[b, h] = s0 * g["sd"] + _mm_tn(g["k_out"], v_new)
                og = p_ref[b, :, base + GDN_CONV_CH + h * 128:base + GDN_CONV_CH + (h + 1) * 128]
                out = _rms(o, ggain_ref[...]) * _silu(og)
                y_ref[b, :, RW_W + h * 128:RW_W + (h + 1) * 128] = out.astype(y_ref.dtype)
            return run

        return ([rw_read, rw_mix, rw_solve, rw_out] + [rw_finish(k, b) for k, b in enumerate(rows)]
                + [gd_solve] + [gd_out(u, b, h) for u, (b, h) in enumerate(gunits)])

    group = min(ODD_GROUP, nb)
    groups = [list(range(g, g + group)) for g in range(0, nb, group)]
    ds = [dict() for _ in range(nb)]
    invs = [dict() for _ in groups]
    stage1 = [_merge_pieces(*[prep_pieces(b, ds[b]) for b in rows]) for rows in groups]
    stage2 = [inverse_pieces(rows, ds, invs[g]) for g, rows in enumerate(groups)]
    stage3 = [state_pieces(rows, ds, invs[g]) for g, rows in enumerate(groups)]
    for slot in range(len(groups) + 2):
        active = []
        if slot < len(groups):
            active.append(stage1[slot])
        if 0 <= slot - 1 < len(groups):
            active.append(stage2[slot - 1])
        if 0 <= slot - 2 < len(groups):
            active.append(stage3[slot - 2])
        for piece in _merge_pieces(*active):
            piece()


def _odd_mixer(p, mu, w0, a0, wa2, g2, kks, kas, rk, lnw, lnb, conv_w, alog, dtb, ggain):
    b, lp, n = p.shape
    nc = lp // CHUNK
    nb = ODD_NB if b % ODD_NB == 0 else 1
    t = np.arange(CHUNK)
    tril = jnp.asarray(t[None, :] <= t[:, None], BF)
    l = np.arange(128)
    bones = jnp.asarray(l[:, None] // RW_HD == l[None, :] // RW_HD, BF)
    row = lambda a: a.reshape(1, -1)
    consts = [row(mu), row(w0), row(a0), wa2, g2, row(kks), row(kas), row(rk), row(lnw), row(lnb),
              conv_w, row(alog), row(dtb), row(ggain), tril, bones]
    const2 = lambda i, j: (0, 0)
    return pl.pallas_call(
        _odd_body,
        grid=(b // nb, nc),
        in_specs=[pl.BlockSpec((nb, CHUNK, n), lambda i, j: (i, j, 0))]
        + [pl.BlockSpec(a.shape, const2) for a in consts],
        out_specs=pl.BlockSpec((nb, CHUNK, RW_W + GDN_W), lambda i, j: (i, j, 0)),
        out_shape=jax.ShapeDtypeStruct((b, lp, RW_W + GDN_W), BF),
        scratch_shapes=[
            pltpu.VMEM((nb, RW_HEADS, RW_HD, RW_HD), F32),
            pltpu.VMEM((nb, GDN_HEADS, GDN_DK, GDN_DK), F32),
            pltpu.VMEM((nb, 8, RW_IN), F32),
            pltpu.VMEM((nb, 8, GDN_CONV_CH), F32),
        ],
        compiler_params=pltpu.CompilerParams(
            dimension_semantics=("arbitrary", "arbitrary"), vmem_limit_bytes=VMEM_LIMIT),
        name="odd_mixer",
    )(p, *consts)


def _row_tile(lp):
    best = 8
    for tm in range(8, min(lp, 704) + 1, 8):
        if lp % tm == 0:
            best = tm
    return best


def kernel(x, meta_tokens, norm_gains, w_in_even, w_out_even, hg_lb_logits, hg_norm_gain, w_in_odd, w_out_odd, rw_mu, rw_w0, rw_w2, rw_a0, rw_a2, rw_g2, rw_kk_scale, rw_ka_scale, rw_rk, rw_lnx_w, rw_lnx_b, gdn_conv_w, gdn_a_log, gdn_dt_bias, gdn_norm_gain, ffn_w_up, ffn_conv_w, ffn_conv_b, ffn_w_down):
    bsz, seq, d = x.shape
    depth = norm_gains.shape[0]
    l = N_META + seq
    pad = (-l) % CHUNK
    lp = l + pad
    tm = _row_tile(lp)
    lead = pad + N_META

    half = RET_DK // 2
    pos = (jnp.arange(lp, dtype=jnp.int32) - pad).astype(F32)
    inv = ROPE_BASE ** (-jnp.arange(half, dtype=F32) / half)
    ang = pos[:, None] * inv[None, :]
    cos_t = jnp.concatenate([jnp.cos(ang), jnp.cos(ang)], axis=1)
    sin_t = jnp.concatenate([-jnp.sin(ang), jnp.sin(ang)], axis=1)

    h = None
    for layer in range(depth):
        g = norm_gains[layer]
        i = layer // 2
        if layer % 2 == 0:
            w_in = w_in_even[i].astype(BF)
        else:
            w_in = jnp.pad(w_in_odd[i], ((0, 0), (0, ODD_IN_PAD - w_in_odd.shape[2]))).astype(BF)
        if layer == 0:
            p, h = _embed_proj(x, meta_tokens.astype(x.dtype), g[0], w_in, tm, lead)
        else:
            p = _norm_proj(h, g[0], w_in, tm)
        if layer % 2 == 0:
            y = _even_mixer(p, cos_t, sin_t, hg_lb_logits, hg_norm_gain[i], layer)
            w_out = w_out_even[i]
        else:
            wa2 = jnp.zeros((RW_DECAY_LORA + RW_AAA_LORA, 2 * RW_W), F32)
            wa2 = wa2.at[:RW_DECAY_LORA, :RW_W].set(rw_w2[i]).at[RW_DECAY_LORA:, RW_W:].set(rw_a2[i])
            lane_pad = lambda a: jnp.pad(a, (0, LANES - a.shape[0]))
            y = _odd_mixer(p, rw_mu[i], rw_w0[i], rw_a0[i], wa2.astype(BF), rw_g2[i].astype(BF),
                           rw_kk_scale[i], rw_ka_scale[i], rw_rk[i].reshape(-1), rw_lnx_w[i], rw_lnx_b[i],
                           gdn_conv_w[i], lane_pad(gdn_a_log[i]), lane_pad(gdn_dt_bias[i]), gdn_norm_gain[i])
            w_out = w_out_odd[i]
        ffn_args = (y, w_out.astype(BF), g[1], h, g[2], ffn_w_up[layer].astype(BF), ffn_conv_w[layer],
                    ffn_conv_b[layer], ffn_w_down[layer].astype(BF), g[3])
        if layer + 1 < depth:
            h = _out_ffn(*ffn_args, tm, pad)
        else:
            h = _out_ffn_tail(*ffn_args, _row_tile(seq), lead)
    return h
```

```python
import functools
import math

import numpy as np
import jax
import jax.numpy as jnp
from jax import lax
from jax.experimental import pallas as pl
from jax.experimental.pallas import tpu as pltpu

F32 = jnp.float32
BF = jnp.bfloat16

D_MODEL = 1024
CHUNK = 64
N_META = 16
EPS = 1e-6
ROPE_BASE = 10000.0

RET_HEADS = D_MODEL // 256
RET_DK = 128
HG_HEADS = D_MODEL // 256
HG_DK = 128
RW_HEADS = D_MODEL // 128
RW_HD = 64
RW_DECAY_LORA = 64
RW_AAA_LORA = 64
RW_GATE_LORA = 128
RW_LNX_EPS = 64e-5
GDN_HEADS = D_MODEL // 256
GDN_DK = 128
GDN_CONV = 4
D_FF = 128 * ((8 * D_MODEL // 3 + 127) // 128)
FFN_CONV = 3

RET_W = RET_HEADS * RET_DK
HG_W = HG_HEADS * HG_DK
EVEN_IN = 4 * RET_W + 4 * HG_W
RW_W = RW_HEADS * RW_HD
RW_IN = 3 * RW_W + RW_DECAY_LORA + RW_AAA_LORA + RW_GATE_LORA
GDN_W = GDN_HEADS * GDN_DK
GDN_CONV_CH = 3 * GDN_W
GDN_IN = GDN_CONV_CH + GDN_W + 2 * GDN_HEADS
LANES = 128
GDN_IN_PAD = LANES * ((GDN_IN + LANES - 1) // LANES)
ODD_IN_PAD = RW_IN + GDN_IN_PAD
HG_LEVELS = (32, 16, 8, 4, 2, 1)
VMEM_LIMIT = 56 * 1024 * 1024
FFN_TILE = 256
FFN_DOWN_GROUP = D_FF // FFN_TILE
EVEN_NB = 4
ODD_NB = 4
ODD_GROUP = 2


def _dot(a, b):
    return jnp.dot(a, b, preferred_element_type=F32)


def _mm(a, b):
    return _dot(a.astype(BF), b.astype(BF))


def _mm_nt(a, b):
    return lax.dot_general(a.astype(BF), b.astype(BF), (((1,), (1,)), ((), ())), preferred_element_type=F32)


def _mm_tn(a, b):
    return lax.dot_general(a.astype(BF), b.astype(BF), (((0,), (0,)), ((), ())), preferred_element_type=F32)


def _split2(x):
    hi = x.astype(BF)
    lo = (x - hi.astype(F32)).astype(BF)
    return hi, lo


def _split3(x):
    hi = x.astype(BF)
    r1 = x - hi.astype(F32)
    mid = r1.astype(BF)
    lo = (r1 - mid.astype(F32)).astype(BF)
    return hi, mid, lo


def _cmm3(c, x):
    hi, mid, lo = _split3(x)
    return _dot(c, hi) + _dot(c, mid) + _dot(c, lo)


def _cmm2(c, x):
    hi, lo = _split2(x)
    return _dot(c, hi) + _dot(c, lo)


def _mmc2(x, c):
    hi, lo = _split2(x)
    return _dot(hi, c) + _dot(lo, c)


def _sigmoid(x):
    return 1.0 / (1.0 + jnp.exp(-x))


def _silu(x):
    return x * _sigmoid(x)


def _softplus(x):
    return jnp.maximum(x, 0.0) + jnp.log1p(jnp.exp(-jnp.abs(x)))


def _rms(x, g):
    return x * lax.rsqrt(jnp.mean(x * x, axis=-1, keepdims=True) + EPS) * g


def _norm_proj_body(h_ref, g_ref, w_ref, o_ref):
    u = _rms(h_ref[...], g_ref[...])
    o_ref[...] = _dot(u.astype(BF), w_ref[...])


def _norm_proj(h, gain, w, tm):
    b, lp, d = h.shape
    n = w.shape[1]
    return pl.pallas_call(
        _norm_proj_body,
        grid=(b, lp // tm),
        in_specs=[
            pl.BlockSpec((None, tm, d), lambda i, j: (i, j, 0)),
            pl.BlockSpec((1, d), lambda i, j: (0, 0)),
            pl.BlockSpec((d, n), lambda i, j: (0, 0), pipeline_mode=pl.Buffered(1)),
        ],
        out_specs=pl.BlockSpec((None, tm, n), lambda i, j: (i, j, 0)),
        out_shape=jax.ShapeDtypeStruct((b, lp, n), F32),
        compiler_params=pltpu.CompilerParams(
            dimension_semantics=("arbitrary", "arbitrary"), vmem_limit_bytes=VMEM_LIMIT),
        name="norm_proj",
    )(h, gain.reshape(1, d), w)


def _embed_proj_body(x_ref, meta_ref, g_ref, w_ref, p_ref, h_ref, *, lead):
    xb = x_ref[0]
    tm, d = xb.shape
    n_zero = lead - meta_ref.shape[0]
    first = jnp.concatenate([jnp.zeros((n_zero, d), F32), meta_ref[...], xb[0:tm - lead, :]], axis=0)
    blk = jnp.where(pl.program_id(1) == 0, first, xb)
    h_ref[...] = blk
    p_ref[...] = _dot(_rms(blk, g_ref[...]).astype(BF), w_ref[...])


def _embed_proj(x, meta, gain, w, tm, lead):
    b, s, d = x.shape
    lp = s + lead
    n = w.shape[1]
    assert lead % 8 == 0 and meta.shape[0] % 8 == 0 and lead < tm and lp % tm == 0
    x_rows = lambda i, j: (i, pl.multiple_of(jnp.maximum(tm * j - lead, 0), 8), 0)
    return pl.pallas_call(
        functools.partial(_embed_proj_body, lead=lead),
        grid=(b, lp // tm),
        in_specs=[
            pl.BlockSpec((pl.Element(1), pl.Element(tm), pl.Element(d)), x_rows),
            pl.BlockSpec(meta.shape, lambda i, j: (0, 0)),
            pl.BlockSpec((1, d), lambda i, j: (0, 0)),
            pl.BlockSpec((d, n), lambda i, j: (0, 0), pipeline_mode=pl.Buffered(1)),
        ],
        out_specs=[pl.BlockSpec((None, tm, n), lambda i, j: (i, j, 0)),
                   pl.BlockSpec((None, tm, d), lambda i, j: (i, j, 0))],
        out_shape=[jax.ShapeDtypeStruct((b, lp, n), F32), jax.ShapeDtypeStruct((b, lp, d), F32)],
        compiler_params=pltpu.CompilerParams(
            dimension_semantics=("arbitrary", "arbitrary"), vmem_limit_bytes=VMEM_LIMIT),
        name="embed_proj",
    )(x, meta, gain.reshape(1, d), w)


def _shift_rows(a, carry_ref, cols, n_back, row):
    out = []
    for s in range(1, n_back + 1):
        sh = pltpu.roll(a, s, 0)
        for r in range(s):
            sh = jnp.where(row == r, carry_ref[8 - s + r:9 - s + r, cols], sh)
        out.append(sh)
    return out


def _conv_ffn_rows(x, shift, g2_ref, wup_ref, cw_ref, cb_ref, wdn_ref, g3_ref):
    u = _rms(x, g2_ref[...]).astype(BF)
    n_tiles = D_FF // FFN_TILE

    def up(c):
        return [_dot(u, wup_ref[:, part * D_FF + c * FFN_TILE:part * D_FF + (c + 1) * FFN_TILE])
                for part in range(2)]

    acc = jnp.zeros(x.shape, F32)
    acts = []
    pending = [up(c) for c in range(min(2, n_tiles))]
    for c in range(n_tiles):
        cur = pending.pop(0)
        if c + 2 < n_tiles:
            pending.append(up(c + 2))
        z = []
        for part in range(2):
            cols = slice(part * D_FF + c * FFN_TILE, part * D_FF + (c + 1) * FFN_TILE)
            a = cur[part]
            am1, am2 = shift(a, cols)
            z.append(am2 * cw_ref[0:1, cols] + am1 * cw_ref[1:2, cols] + a * cw_ref[2:3, cols] + cb_ref[:, cols])
        acts.append((_silu(z[0]) * z[1]).astype(BF))
        if len(acts) == FFN_DOWN_GROUP or c == n_tiles - 1:
            lo = (c + 1 - len(acts)) * FFN_TILE
            acc = acc + _dot(jnp.concatenate(acts, axis=1), wdn_ref[lo:(c + 1) * FFN_TILE, :])
            acts = []
    return x + _rms(acc, g3_ref[...])


def _ffn_body(y_ref, wout_ref, g1_ref, h_ref, g2_ref, wup_ref, cw_ref, cb_ref, wdn_ref, g3_ref, o_ref, carry_ref,
              *, tm, pad):
    @pl.when(pl.program_id(1) == 0)
    def _():
        carry_ref[...] = jnp.zeros_like(carry_ref)

    t = pl.program_id(1) * tm + lax.broadcasted_iota(jnp.int32, (tm, D_MODEL), 0)
    x = jnp.where(t >= pad, h_ref[...] + _rms(_dot(y_ref[...], wout_ref[...]), g1_ref[...]), 0.0)
    row = lax.broadcasted_iota(jnp.int32, (tm, FFN_TILE), 0)

    def shift(a, cols):
        am1, am2 = _shift_rows(a, carry_ref, cols, FFN_CONV - 1, row)
        carry_ref[:, cols] = a[tm - 8:tm, :]
        return am1, am2

    out = _conv_ffn_rows(x, shift, g2_ref, wup_ref, cw_ref, cb_ref, wdn_ref, g3_ref)
    o_ref[...] = jnp.where(t >= pad, out, 0.0)


def _ffn_tail_body(y_ref, wout_ref, g1_ref, h_ref, g2_ref, wup_ref, cw_ref, cb_ref, wdn_ref, g3_ref, o_ref, *,
                   halo):
    x = h_ref[0] + _rms(_dot(y_ref[0], wout_ref[...]), g1_ref[...])
    shift = lambda a, cols: (pltpu.roll(a, 1, 0), pltpu.roll(a, 2, 0))
    out = _conv_ffn_rows(x, shift, g2_ref, wup_ref, cw_ref, cb_ref, wdn_ref, g3_ref)
    o_ref[...] = out[halo:, :]


def _ffn_weight_specs(k, d, ff2):
    const = lambda i, j: (0, 0)
    return dict(
        w_out=pl.BlockSpec((k, d), const, pipeline_mode=pl.Buffered(1)), gain=pl.BlockSpec((1, d), const),
        w_up=pl.BlockSpec((d, ff2), const, pipeline_mode=pl.Buffered(1)),
        conv_w=pl.BlockSpec((FFN_CONV, ff2), const), conv_b=pl.BlockSpec((1, ff2), const),
        w_down=pl.BlockSpec((ff2 // 2, d), const, pipeline_mode=pl.Buffered(1)))


def _out_ffn(y, w_out, g1, h, g2, w_up, conv_w, conv_b, w_down, g3, tm, pad):
    b, lp, d = h.shape
    k = y.shape[2]
    ff2 = w_up.shape[1]
    ws = _ffn_weight_specs(k, d, ff2)
    rows = lambda width: pl.BlockSpec((None, tm, width), lambda i, j: (i, j, 0))
    return pl.pallas_call(
        functools.partial(_ffn_body, tm=tm, pad=pad),
        grid=(b, lp // tm),
        in_specs=[rows(k), ws["w_out"], ws["gain"], rows(d), ws["gain"], ws["w_up"], ws["conv_w"], ws["conv_b"],
                  ws["w_down"], ws["gain"]],
        out_specs=rows(d),
        out_shape=jax.ShapeDtypeStruct((b, lp, d), F32),
        scratch_shapes=[pltpu.VMEM((8, ff2), F32)],
        compiler_params=pltpu.CompilerParams(
            dimension_semantics=("arbitrary", "arbitrary"), vmem_limit_bytes=VMEM_LIMIT),
        name="conv_ffn",
    )(y, w_out, g1.reshape(1, d), h, g2.reshape(1, d), w_up, conv_w, conv_b.reshape(1, ff2), w_down,
      g3.reshape(1, d))


def _out_ffn_tail(y, w_out, g1, h, g2, w_up, conv_w, conv_b, w_down, g3, tm, lead):
    b, lp, d = h.shape
    seq = lp - lead
    k = y.shape[2]
    ff2 = w_up.shape[1]
    halo = 8
    assert seq % tm == 0 and lead >= halo and lead % 8 == 0
    ws = _ffn_weight_specs(k, d, ff2)
    rows = lambda width: pl.BlockSpec(
        (pl.Element(1), pl.Element(tm + halo), pl.Element(width)),
        lambda i, j: (i, pl.multiple_of(lead - halo + tm * j, 8), 0))
    return pl.pallas_call(
        functools.partial(_ffn_tail_body, halo=halo),
        grid=(b, seq // tm),
        in_specs=[rows(k), ws["w_out"], ws["gain"], rows(d), ws["gain"], ws["w_up"], ws["conv_w"], ws["conv_b"],
                  ws["w_down"], ws["gain"]],
        out_specs=pl.BlockSpec((None, tm, d), lambda i, j: (i, j, 0)),
        out_shape=jax.ShapeDtypeStruct((b, seq, d), F32),
        compiler_params=pltpu.CompilerParams(
            dimension_semantics=("arbitrary", "arbitrary"), vmem_limit_bytes=VMEM_LIMIT),
        name="conv_ffn_tail",
    )(y, w_out, g1.reshape(1, d), h, g2.reshape(1, d), w_up, conv_w, conv_b.reshape(1, ff2), w_down,
      g3.reshape(1, d))


def _hg_consts():
    c = CHUNK
    t = np.arange(c)
    m = np.zeros((2 + len(HG_LEVELS), c, c), np.float32)
    m[0] = t[None, :] <= t[:, None]
    m[1] = t[None, :] > t[:, None]
    masks = np.zeros((len(HG_LEVELS) + 1, c, c), np.float32)
    ii, jj = t[:, None], t[None, :]
    for li, s in enumerate(HG_LEVELS):
        for i in range(c):
            mid = (i // (2 * s)) * (2 * s) + s - 1
            if i % (2 * s) >= s:
                m[2 + li, i, mid + 1:i + 1] = 1.0
            else:
                m[2 + li, i, i + 1:mid + 1] = 1.0
        masks[li] = (ii // (2 * s) == jj // (2 * s)) & (ii % (2 * s) >= s) & (jj % (2 * s) < s)
    masks[-1] = ii == jj
    return m.reshape(-1, c), masks


def _even_body(p_ref, cos_ref, sin_ref, lbl_ref, hgain_ref, mall_ref, masks_ref, y_ref, sret_ref, shg_ref, *,
               layer):
    @pl.when(pl.program_id(1) == 0)
    def _():
        sret_ref[...] = jnp.zeros_like(sret_ref)
        shg_ref[...] = jnp.zeros_like(shg_ref)

    c = CHUNK
    cosv = cos_ref[...]
    sinv = sin_ref[...]
    ii = lax.broadcasted_iota(jnp.int32, (c, c), 0)
    jj = lax.broadcasted_iota(jnp.int32, (c, c), 1)
    diff = (ii - jj).astype(F32)
    rowi = lax.broadcasted_iota(jnp.int32, (c, RET_DK), 0).astype(F32)

    nb = p_ref.shape[0]
    col = lambda base, h: slice(base + h * 128, base + (h + 1) * 128)
    units = [(b, h) for b in range(nb) for h in range(RET_HEADS)]
    nu = len(units)
    lgs = [math.log1p(-(2.0 ** (-5.0 - h))) for _, h in units]
    qs = [p_ref[b, :, col(0, h)] for b, h in units]
    ks = [p_ref[b, :, col(RET_W, h)] for b, h in units]
    vs = [p_ref[b, :, col(2 * RET_W, h)] for b, h in units]
    qs = [q * cosv + pltpu.roll(q, RET_DK // 2, 1) * sinv for q in qs]
    ks = [(k * cosv + pltpu.roll(k, RET_DK // 2, 1) * sinv) * (RET_DK ** -0.5) for k in ks]
    ss = [sret_ref[b, h] for b, h in units]
    att = [_mm_nt(qs[u], ks[u]) * jnp.where(diff >= 0, jnp.exp(lgs[u] * jnp.maximum(diff, 0.0)), 0.0)
           for u in range(nu)]
    inter = [_mm(qs[u] * jnp.exp(lgs[u] * (rowi + 1.0)), ss[u]) for u in range(nu)]
    kv = [_mm_tn(ks[u] * jnp.exp(lgs[u] * (c - 1.0 - rowi)), vs[u]) for u in range(nu)]
    outs = [_mm(att[u], vs[u]) + inter[u] for u in range(nu)]
    for u, (b, h) in enumerate(units):
        sret_ref[b, h] = ss[u] * math.exp(lgs[u] * c) + kv[u]
        o = outs[u]
        xc = o - jnp.mean(o, axis=-1, keepdims=True)
        oa = xc * lax.rsqrt(jnp.mean(xc * xc, axis=-1, keepdims=True) + EPS)
        y_ref[b, :, col(0, h)] = (oa * _silu(p_ref[b, :, col(3 * RET_W, h)])).astype(y_ref.dtype)

    lgt = lbl_ref[...]
    mx = jnp.max(lgt, axis=0, keepdims=True)
    ex = jnp.exp(lgt - mx)
    sm = ex / jnp.sum(ex, axis=0, keepdims=True)
    lb_all = jnp.sum(sm[0:layer + 1, :], axis=0, keepdims=True)
    base = 4 * RET_W
    mall = mall_ref[...]
    nl = len(HG_LEVELS)
    units = [(b, h) for b in range(nb) for h in range(HG_HEADS)]
    nu = len(units)
    lbs = [lb_all[:, col(0, h)] for _, h in units]
    qs = [p_ref[b, :, col(base, h)] for b, h in units]
    fbs = [p_ref[b, :, col(base + HG_W, h)] for b, h in units]
    vs = [p_ref[b, :, col(base + 2 * HG_W, h)] for b, h in units]
    ks = [(1.0 - lbs[u]) * _sigmoid(-fbs[u]) for u in range(nu)]
    xs = [jnp.exp(_cmm2(mall, jnp.log(lbs[u] + (1.0 - lbs[u]) * _sigmoid(fbs[u])))) for u in range(nu)]
    sts = [shg_ref[b, h] for b, h in units]
    att = [masks_ref[nl] * _mm_nt(qs[u], ks[u]) for u in range(nu)]
    for li in range(nl):
        lv = slice((2 + li) * c, (3 + li) * c)
        att = [att[u] + masks_ref[li] * _mm_nt(qs[u] * xs[u][lv, :], ks[u] * xs[u][lv, :]) for u in range(nu)]
    inter = [_mm_nt(qs[u] * xs[u][0:c, :], sts[u]) for u in range(nu)]
    kv = [_mm_tn(vs[u], ks[u] * xs[u][c:2 * c, :]) for u in range(nu)]
    outs = [_mm(att[u], vs[u]) + inter[u] for u in range(nu)]
    for u, (b, h) in enumerate(units):
        shg_ref[b, h] = sts[u] * xs[u][c - 1:c, :] + kv[u]
        ob = _rms(outs[u], hgain_ref[:, col(0, h)])
        y_ref[b, :, col(RET_W, h)] = (ob * _silu(p_ref[b, :, col(base + 3 * HG_W, h)])).astype(y_ref.dtype)


def _even_mixer(p, cos_t, sin_t, lb_logits, hg_gain, layer):
    b, lp, n = p.shape
    nc = lp // CHUNK
    nb = EVEN_NB if b % EVEN_NB == 0 else 1
    mall, masks = _hg_consts()
    const2 = lambda i, j: (0, 0)
    return pl.pallas_call(
        functools.partial(_even_body, layer=layer),
        grid=(b // nb, nc),
        in_specs=[
            pl.BlockSpec((nb, CHUNK, n), lambda i, j: (i, j, 0)),
            pl.BlockSpec((CHUNK, RET_DK), lambda i, j: (j, 0)),
            pl.BlockSpec((CHUNK, RET_DK), lambda i, j: (j, 0)),
            pl.BlockSpec(lb_logits.shape, const2),
            pl.BlockSpec((1, HG_W), const2),
            pl.BlockSpec(mall.shape, const2),
            pl.BlockSpec(masks.shape, lambda i, j: (0, 0, 0)),
        ],
        out_specs=pl.BlockSpec((nb, CHUNK, RET_W + HG_W), lambda i, j: (i, j, 0)),
        out_shape=jax.ShapeDtypeStruct((b, lp, RET_W + HG_W), BF),
        scratch_shapes=[pltpu.VMEM((nb, RET_HEADS, RET_DK, RET_DK), F32),
                        pltpu.VMEM((nb, HG_HEADS, HG_DK, HG_DK), F32)],
        compiler_params=pltpu.CompilerParams(
            dimension_semantics=("arbitrary", "arbitrary"), vmem_limit_bytes=VMEM_LIMIT),
        name="even_mixer",
    )(p, cos_t, sin_t, lb_logits, hg_gain.reshape(1, HG_W), jnp.asarray(mall, BF), jnp.asarray(masks))


def _merge_pieces(*lists):
    total = max(len(l) for l in lists)
    keyed = []
    for li, l in enumerate(lists):
        for k, piece in enumerate(l):
            keyed.append(((k + 0.5) * total / len(l), li, k, piece))
    keyed.sort(key=lambda t: t[:3])
    return [t[3] for t in keyed]


def _odd_body(p_ref, mu_ref, w0_ref, a0_ref, wa2_ref, g2_ref, kks_ref, kas_ref, rk_ref, lnw_ref, lnb_ref,
              cw_ref, alog_ref, dtb_ref, ggain_ref, tril_ref, bones_ref, y_ref,
              hrw_ref, sgd_ref, cpc_ref, ccv_ref):
    @pl.when(pl.program_id(1) == 0)
    def _():
        hrw_ref[...] = jnp.zeros_like(hrw_ref)
        sgd_ref[...] = jnp.zeros_like(sgd_ref)
        cpc_ref[...] = jnp.zeros_like(cpc_ref)
        ccv_ref[...] = jnp.zeros_like(ccv_ref)

    nb = p_ref.shape[0]
    c = CHUNK
    tril = tril_ref[...]
    bones = bones_ref[...]
    ii = lax.broadcasted_iota(jnp.int32, (c, c), 0)
    jj = lax.broadcasted_iota(jnp.int32, (c, c), 1)
    strict = ii > jj
    incl = ii >= jj
    eye = ii == jj
    eye_f = eye.astype(F32)
    ii2 = lax.broadcasted_iota(jnp.int32, (c, 2 * c), 0)
    lane2 = lax.broadcasted_iota(jnp.int32, (c, 2 * c), 1)
    jj2 = jnp.bitwise_and(lane2, c - 1)
    ak_mask = (ii2 > jj2) & (lane2 >= c)
    rbk_mask = ii2 >= jj2
    lane = lax.broadcasted_iota(jnp.int32, (c, 128), 1)
    heads = [slice(h * RW_HD, (h + 1) * RW_HD) for h in range(RW_HEADS)]
    base = RW_IN
    inv_hd = 1.0 / RW_HD

    def seg_sum(x):
        return jnp.concatenate(
            [_mmc2(x[:, p * 128:(p + 1) * 128], bones) for p in range(x.shape[1] // 128)], axis=1)

    def prep_pieces(b, d):
        def shift():
            pc = p_ref[b, :, 0:RW_IN]
            row = lax.broadcasted_iota(jnp.int32, (c, RW_IN), 0)
            prev = jnp.where(row == 0, cpc_ref[b, 7:8, :], pltpu.roll(pc, 1, 0))
            cpc_ref[b] = pc[c - 8:c, :]
            pcs = pc + (prev - pc) * mu_ref[...]
            d.update(r=pcs[:, 0:RW_W], k=pcs[:, RW_W:2 * RW_W], v=pcs[:, 2 * RW_W:3 * RW_W],
                     lo=pcs[:, 3 * RW_W:3 * RW_W + 128], glo=pcs[:, 3 * RW_W + 128:RW_IN])

        def lora():
            lo = d["lo"]
            wa = _mm(jnp.where(lane < RW_DECAY_LORA, jnp.tanh(lo), lo), wa2_ref[...])
            log_w = -_softplus(-(w0_ref[...] + wa[:, 0:RW_W])) - 0.5
            d["lw"] = -jnp.exp(log_w)
            d["a"] = _sigmoid(a0_ref[...] + wa[:, RW_W:2 * RW_W])
            d["gate"] = _mm(_sigmoid(d["glo"]), g2_ref[...])

        def keys():
            kks = d["k"] * kks_ref[...]
            kkn = kks * lax.rsqrt(seg_sum(kks * kks) + EPS)
            d["k2"] = d["k"] * (1.0 + (d["a"] - 1.0) * kas_ref[...])
            d["kkn"] = kkn
            d["beta"] = kkn * d["a"]
            d["c_inc"] = _cmm3(tril, d["lw"])

        def decays():
            c_inc, k2, beta = d["c_inc"], d["k2"], d["beta"]
            c_last = c_inc[c - 1:c, :]
            e_neg = jnp.exp(-c_inc)
            e_rest = jnp.exp(c_last - c_inc)
            d.update(ah=-d["kkn"] * jnp.exp(c_inc - d["lw"]), rh=d["r"] * jnp.exp(c_inc), bt=beta * e_neg,
                     kt=k2 * e_neg, bg=beta * e_rest, kg=k2 * e_rest, gc=jnp.exp(c_last),
                     ar=[], ls=[], ak=[], rbk=[], gcol=[], bgkg=[])

        def gram(h):
            def run():
                sl = heads[h]
                ar = jnp.concatenate([d["ah"][:, sl], d["rh"][:, sl]], axis=0).astype(BF)
                gm = _mm_nt(ar, jnp.concatenate([d["bt"][:, sl], d["kt"][:, sl]], axis=0))
                d["ar"].append(ar)
                d["ls"].append(jnp.where(strict, gm[0:c, 0:c], 0.0))
                d["ak"].append(jnp.where(ak_mask, gm[0:c, :], 0.0))
                d["rbk"].append(jnp.where(rbk_mask, gm[c:2 * c, :], 0.0))
                d["gcol"].append(jnp.sum(eye_f * d["gc"][:, sl], axis=1, keepdims=True))
                d["bgkg"].append(jnp.concatenate([d["bg"][:, sl], d["kg"][:, sl]], axis=0).astype(BF))
            return run

        def conv():
            x = p_ref[b, :, base:base + GDN_CONV_CH]
            rowc = lax.broadcasted_iota(jnp.int32, (c, GDN_CONV_CH), 0)
            xm1, xm2, xm3 = _shift_rows(x, ccv_ref.at[b], slice(None), GDN_CONV - 1, rowc)
            ccv_ref[b] = x[c - 8:c, :]
            d["qkv"] = _silu(xm3 * cw_ref[0:1, :] + xm2 * cw_ref[1:2, :] + xm1 * cw_ref[2:3, :] + x * cw_ref[3:4, :])

        def gates():
            sc = p_ref[b, :, base + GDN_CONV_CH + GDN_W:base + GDN_IN_PAD]
            g_all = -jnp.exp(alog_ref[...]) * _softplus(sc + dtb_ref[...])
            d["b_all"] = _sigmoid(sc)
            g_b = jnp.concatenate(
                [jnp.broadcast_to(g_all[:, h:h + 1], (c, 128)) for h in range(GDN_HEADS)], axis=1)
            d["cg_all"] = _cmm3(tril, g_b)
            d["gd"] = []

        def gdn_head(h):
            def run():
                hs = slice(h * 128, (h + 1) * 128)
                qkv = d["qkv"]
                q = qkv[:, hs]
                kd = qkv[:, GDN_W + h * 128:GDN_W + (h + 1) * 128]
                vd = qkv[:, 2 * GDN_W + h * 128:2 * GDN_W + (h + 1) * 128]
                q = q * lax.rsqrt(jnp.sum(q * q, axis=-1, keepdims=True) + EPS) * (GDN_DK ** -0.5)
                kd = kd * lax.rsqrt(jnp.sum(kd * kd, axis=-1, keepdims=True) + EPS)
                b_b = jnp.broadcast_to(d["b_all"][:, GDN_HEADS + h:GDN_HEADS + h + 1], (c, 128))
                cg = d["cg_all"][:, hs]
                cg_row = jnp.sum(jnp.where(eye, cg[:, 0:c], 0.0), axis=0, keepdims=True)
                decay = jnp.exp(jnp.where(incl, cg[:, 0:c] - cg_row, -jnp.inf))
                eg = jnp.exp(cg)
                cl = cg[c - 1:c, :]
                d["gd"].append(dict(
                    l=-jnp.where(strict, b_b[:, 0:c] * _mm_nt(kd, kd) * decay, 0.0),
                    rhs=jnp.concatenate([b_b * vd, b_b * kd * eg], axis=1), qk=_mm_nt(q, kd) * decay,
                    q_in=q * eg, k_out=kd * jnp.exp(cl - cg), sd=jnp.exp(cl)))
            return run

        return ([shift, lora, keys, decays] + [gram(h) for h in range(RW_HEADS)]
                + [conv, gates] + [gdn_head(h) for h in range(GDN_HEADS)])

    def inverse_pieces(rows, ds, out):
        st = {}

        def start():
            st["l"] = [l for b in rows for l in ds[b]["ls"]] + [g["l"] for b in rows for g in ds[b]["gd"]]
            st["p"] = [eye_f + l for l in st["l"]]

        def square():
            st["l"] = [_mm(l, l) for l in st["l"]]

        def extend():
            st["p"] = [p + _mm(l, p) for l, p in zip(st["l"], st["p"])]

        def finish():
            out["tinv"] = st["p"]

        levels = []
        span = 2
        while span < c:
            levels += [square, extend]
            span *= 2
        return [start] + levels + [finish]

    def state_pieces(rows, ds, inv):
        units = [(b, h) for b in rows for h in range(RW_HEADS)]
        gunits = [(b, h) for b in rows for h in range(GDN_HEADS)]
        st = {}

        def rw_read():
            st["h0"] = [hrw_ref[b, h] for b, h in units]
            st["hr"] = [_mm(ds[b]["ar"][h], st["h0"][u]) for u, (b, h) in enumerate(units)]
            st["vs"] = [ds[b]["v"][:, heads[h]] for b, h in units]

        def rw_mix():
            vs = st["vs"]
            st["x"] = [st["hr"][u][0:c, :] + _mm(ds[b]["ak"][h], jnp.concatenate([vs[u], vs[u]], axis=0))
                       for u, (b, h) in enumerate(units)]

        def rw_solve():
            us = [_mm(inv["tinv"][u], st["x"][u]) for u in range(len(units))]
            st["uv"] = [jnp.concatenate([us[u], st["vs"][u]], axis=0).astype(BF) for u in range(len(units))]

        def rw_out():
            st["ys"] = [st["hr"][u][c:2 * c, :] + _mm(ds[b]["rbk"][h], st["uv"][u])
                        for u, (b, h) in enumerate(units)]
            for u, (b, h) in enumerate(units):
                hrw_ref[b, h] = st["h0"][u] * ds[b]["gcol"][h] + _mm_tn(ds[b]["bgkg"][h], st["uv"][u])

        def rw_finish(k, b):
            def run():
                d = ds[b]
                y = jnp.concatenate(st["ys"][k * RW_HEADS:(k + 1) * RW_HEADS], axis=1)
                yc = y - seg_sum(y) * inv_hd
                yn = yc * lax.rsqrt(seg_sum(yc * yc) * inv_hd + RW_LNX_EPS) * lnw_ref[...] + lnb_ref[...]
                out = (yn + seg_sum(d["r"] * d["k2"] * rk_ref[...]) * d["v"]) * d["gate"]
                y_ref[b, :, 0:RW_W] = out.astype(y_ref.dtype)
            return run

        def gd_solve():
            n_rw = len(units)
            sols = [_mm(inv["tinv"][n_rw + u], ds[b]["gd"][h]["rhs"]) for u, (b, h) in enumerate(gunits)]
            st["s0"] = [sgd_ref[b, h] for b, h in gunits]
            st["v_new"] = [sols[u][:, 0:128] - _mm(sols[u][:, 128:256], st["s0"][u]) for u in range(len(gunits))]

        def gd_out(u, b, h):
            def run():
                g = ds[b]["gd"][h]
                s0, v_new = st["s0"][u], st["v_new"][u]
                o = _mm(g["q_in"], s0) + _mm(g["qk"], v_new)
                sgd_ref[b, h] = s0 * g["sd"] + _mm_tn(g["k_out"], v_new)
                og = p_ref[b, :, base + GDN_CONV_CH + h * 128:base + GDN_CONV_CH + (h + 1) * 128]
                out = _rms(o, ggain_ref[...]) * _silu(og)
                y_ref[b, :, RW_W + h * 128:RW_W + (h + 1) * 128] = out.astype(y_ref.dtype)
            return run

        return ([rw_read, rw_mix, rw_solve, rw_out] + [rw_finish(k, b) for k, b in enumerate(rows)]
                + [gd_solve] + [gd_out(u, b, h) for u, (b, h) in enumerate(gunits)])

    group = min(ODD_GROUP, nb)
    groups = [list(range(g, g + group)) for g in range(0, nb, group)]
    ds = [dict() for _ in range(nb)]
    invs = [dict() for _ in groups]
    stage1 = [_merge_pieces(*[prep_pieces(b, ds[b]) for b in rows]) for rows in groups]
    stage2 = [inverse_pieces(rows, ds, invs[g]) for g, rows in enumerate(groups)]
    stage3 = [state_pieces(rows, ds, invs[g]) for g, rows in enumerate(groups)]
    for slot in range(len(groups) + 2):
        active = []
        if slot < len(groups):
            active.append(stage1[slot])
        if 0 <= slot - 1 < len(groups):
            active.append(stage2[slot - 1])
        if 0 <= slot - 2 < len(groups):
            active.append(stage3[slot - 2])
        for piece in _merge_pieces(*active):
            piece()


def _odd_mixer(p, mu, w0, a0, wa2, g2, kks, kas, rk, lnw, lnb, conv_w, alog, dtb, ggain):
    b, lp, n = p.shape
    nc = lp // CHUNK
    nb = ODD_NB if b % ODD_NB == 0 else 1
    t = np.arange(CHUNK)
    tril = jnp.asarray(t[None, :] <= t[:, None], BF)
    l = np.arange(128)
    bones = jnp.asarray(l[:, None] // RW_HD == l[None, :] // RW_HD, BF)
    row = lambda a: a.reshape(1, -1)
    consts = [row(mu), row(w0), row(a0), wa2, g2, row(kks), row(kas), row(rk), row(lnw), row(lnb),
              conv_w, row(alog), row(dtb), row(ggain), tril, bones]
    const2 = lambda i, j: (0, 0)
    return pl.pallas_call(
        _odd_body,
        grid=(b // nb, nc),
        in_specs=[pl.BlockSpec((nb, CHUNK, n), lambda i, j: (i, j, 0))]
        + [pl.BlockSpec(a.shape, const2) for a in consts],
        out_specs=pl.BlockSpec((nb, CHUNK, RW_W + GDN_W), lambda i, j: (i, j, 0)),
        out_shape=jax.ShapeDtypeStruct((b, lp, RW_W + GDN_W), BF),
        scratch_shapes=[
            pltpu.VMEM((nb, RW_HEADS, RW_HD, RW_HD), F32),
            pltpu.VMEM((nb, GDN_HEADS, GDN_DK, GDN_DK), F32),
            pltpu.VMEM((nb, 8, RW_IN), F32),
            pltpu.VMEM((nb, 8, GDN_CONV_CH), F32),
        ],
        compiler_params=pltpu.CompilerParams(
            dimension_semantics=("arbitrary", "arbitrary"), vmem_limit_bytes=VMEM_LIMIT),
        name="odd_mixer",
    )(p, *consts)


def _row_tile(lp):
    best = 8
    for tm in range(8, min(lp, 704) + 1, 8):
        if lp % tm == 0:
            best = tm
    return best


def kernel(x, meta_tokens, norm_gains, w_in_even, w_out_even, hg_lb_logits, hg_norm_gain, w_in_odd, w_out_odd, rw_mu, rw_w0, rw_w2, rw_a0, rw_a2, rw_g2, rw_kk_scale, rw_ka_scale, rw_rk, rw_lnx_w, rw_lnx_b, gdn_conv_w, gdn_a_log, gdn_dt_bias, gdn_norm_gain, ffn_w_up, ffn_conv_w, ffn_conv_b, ffn_w_down):
    bsz, seq, d = x.shape
    depth = norm_gains.shape[0]
    l = N_META + seq
    pad = (-l) % CHUNK
    lp = l + pad
    tm = _row_tile(lp)
    lead = pad + N_META

    half = RET_DK // 2
    pos = (jnp.arange(lp, dtype=jnp.int32) - pad).astype(F32)
    inv = ROPE_BASE ** (-jnp.arange(half, dtype=F32) / half)
    ang = pos[:, None] * inv[None, :]
    cos_t = jnp.concatenate([jnp.cos(ang), jnp.cos(ang)], axis=1)
    sin_t = jnp.concatenate([-jnp.sin(ang), jnp.sin(ang)], axis=1)

    h = None
    for layer in range(depth):
        g = norm_gains[layer]
        i = layer // 2
        if layer % 2 == 0:
            w_in = w_in_even[i].astype(BF)
        else:
            w_in = jnp.pad(w_in_odd[i], ((0, 0), (0, ODD_IN_PAD - w_in_odd.shape[2]))).astype(BF)
        if layer == 0:
            p, h = _embed_proj(x, meta_tokens.astype(x.dtype), g[0], w_in, tm, lead)
        else:
            p = _norm_proj(h, g[0], w_in, tm)
        if layer % 2 == 0:
            y = _even_mixer(p, cos_t, sin_t, hg_lb_logits, hg_norm_gain[i], layer)
            w_out = w_out_even[i]
        else:
            wa2 = jnp.zeros((RW_DECAY_LORA + RW_AAA_LORA, 2 * RW_W), F32)
            wa2 = wa2.at[:RW_DECAY_LORA, :RW_W].set(rw_w2[i]).at[RW_DECAY_LORA:, RW_W:].set(rw_a2[i])
            lane_pad = lambda a: jnp.pad(a, (0, LANES - a.shape[0]))
            y = _odd_mixer(p, rw_mu[i], rw_w0[i], rw_a0[i], wa2.astype(BF), rw_g2[i].astype(BF),
                           rw_kk_scale[i], rw_ka_scale[i], rw_rk[i].reshape(-1), rw_lnx_w[i], rw_lnx_b[i],
                           gdn_conv_w[i], lane_pad(gdn_a_log[i]), lane_pad(gdn_dt_bias[i]), gdn_norm_gain[i])
            w_out = w_out_odd[i]
        ffn_args = (y, w_out.astype(BF), g[1], h, g[2], ffn_w_up[layer].astype(BF), ffn_conv_w[layer],
                    ffn_conv_b[layer], ffn_w_down[layer].astype(BF), g[3])
        if layer + 1 < depth:
            h = _out_ffn(*ffn_args, tm, pad)
        else:
            h = _out_ffn_tail(*ffn_args, _row_tile(seq), lead)
    return h
```

```python
import functools
import math

import numpy as np
import jax
import jax.numpy as jnp
from jax import lax
from jax.experimental import pallas as pl
from jax.experimental.pallas import tpu as pltpu

F32 = jnp.float32
BF = jnp.bfloat16

D_MODEL = 1024
CHUNK = 64
N_META = 16
EPS = 1e-6
ROPE_BASE = 10000.0

RET_HEADS = D_MODEL // 256
RET_DK = 128
HG_HEADS = D_MODEL // 256
HG_DK = 128
RW_HEADS = D_MODEL // 128
RW_HD = 64
RW_DECAY_LORA = 64
RW_AAA_LORA = 64
RW_GATE_LORA = 128
RW_LNX_EPS = 64e-5
GDN_HEADS = D_MODEL // 256
GDN_DK = 128
GDN_CONV = 4
D_FF = 128 * ((8 * D_MODEL // 3 + 127) // 128)
FFN_CONV = 3

RET_W = RET_HEADS * RET_DK
HG_W = HG_HEADS * HG_DK
EVEN_IN = 4 * RET_W + 4 * HG_W
RW_W = RW_HEADS * RW_HD
RW_IN = 3 * RW_W + RW_DECAY_LORA + RW_AAA_LORA + RW_GATE_LORA
GDN_W = GDN_HEADS * GDN_DK
GDN_CONV_CH = 3 * GDN_W
GDN_IN = GDN_CONV_CH + GDN_W + 2 * GDN_HEADS
LANES = 128
GDN_IN_PAD = LANES * ((GDN_IN + LANES - 1) // LANES)
ODD_IN_PAD = RW_IN + GDN_IN_PAD
HG_LEVELS = (32, 16, 8, 4, 2, 1)
VMEM_LIMIT = 56 * 1024 * 1024
FFN_TILE = 256
EVEN_NB = 4
ODD_NB = 4
ODD_GROUP = 2


def _dot(a, b):
    return jnp.dot(a, b, preferred_element_type=F32)


def _mm(a, b):
    return _dot(a.astype(BF), b.astype(BF))


def _mm_nt(a, b):
    return lax.dot_general(a.astype(BF), b.astype(BF), (((1,), (1,)), ((), ())), preferred_element_type=F32)


def _mm_tn(a, b):
    return lax.dot_general(a.astype(BF), b.astype(BF), (((0,), (0,)), ((), ())), preferred_element_type=F32)


def _split2(x):
    hi = x.astype(BF)
    lo = (x - hi.astype(F32)).astype(BF)
    return hi, lo


def _split3(x):
    hi = x.astype(BF)
    r1 = x - hi.astype(F32)
    mid = r1.astype(BF)
    lo = (r1 - mid.astype(F32)).astype(BF)
    return hi, mid, lo


def _cmm3(c, x):
    hi, mid, lo = _split3(x)
    return _dot(c, hi) + _dot(c, mid) + _dot(c, lo)


def _cmm2(c, x):
    hi, lo = _split2(x)
    return _dot(c, hi) + _dot(c, lo)


def _mmc2(x, c):
    hi, lo = _split2(x)
    return _dot(hi, c) + _dot(lo, c)


def _sigmoid(x):
    return 1.0 / (1.0 + jnp.exp(-x))


def _silu(x):
    return x * _sigmoid(x)


def _softplus(x):
    return jnp.maximum(x, 0.0) + jnp.log1p(jnp.exp(-jnp.abs(x)))


def _rms(x, g):
    return x * lax.rsqrt(jnp.mean(x * x, axis=-1, keepdims=True) + EPS) * g


def _norm_proj_body(h_ref, g_ref, w_ref, o_ref):
    u = _rms(h_ref[...], g_ref[...])
    o_ref[...] = _dot(u.astype(BF), w_ref[...])


def _norm_proj(h, gain, w, tm):
    b, lp, d = h.shape
    n = w.shape[1]
    return pl.pallas_call(
        _norm_proj_body,
        grid=(b, lp // tm),
        in_specs=[
            pl.BlockSpec((None, tm, d), lambda i, j: (i, j, 0)),
            pl.BlockSpec((1, d), lambda i, j: (0, 0)),
            pl.BlockSpec((d, n), lambda i, j: (0, 0), pipeline_mode=pl.Buffered(1)),
        ],
        out_specs=pl.BlockSpec((None, tm, n), lambda i, j: (i, j, 0)),
        out_shape=jax.ShapeDtypeStruct((b, lp, n), F32),
        compiler_params=pltpu.CompilerParams(
            dimension_semantics=("arbitrary", "arbitrary"), vmem_limit_bytes=VMEM_LIMIT),
        name="norm_proj",
    )(h, gain.reshape(1, d), w)


def _embed_proj_body(x_ref, meta_ref, g_ref, w_ref, p_ref, h_ref, *, lead):
    xb = x_ref[0]
    tm, d = xb.shape
    n_zero = lead - meta_ref.shape[0]
    first = jnp.concatenate([jnp.zeros((n_zero, d), F32), meta_ref[...], xb[0:tm - lead, :]], axis=0)
    blk = jnp.where(pl.program_id(1) == 0, first, xb)
    h_ref[...] = blk
    p_ref[...] = _dot(_rms(blk, g_ref[...]).astype(BF), w_ref[...])


def _embed_proj(x, meta, gain, w, tm, lead):
    b, s, d = x.shape
    lp = s + lead
    n = w.shape[1]
    assert lead % 8 == 0 and meta.shape[0] % 8 == 0 and lead < tm and lp % tm == 0
    x_rows = lambda i, j: (i, pl.multiple_of(jnp.maximum(tm * j - lead, 0), 8), 0)
    return pl.pallas_call(
        functools.partial(_embed_proj_body, lead=lead),
        grid=(b, lp // tm),
        in_specs=[
            pl.BlockSpec((pl.Element(1), pl.Element(tm), pl.Element(d)), x_rows),
            pl.BlockSpec(meta.shape, lambda i, j: (0, 0)),
            pl.BlockSpec((1, d), lambda i, j: (0, 0)),
            pl.BlockSpec((d, n), lambda i, j: (0, 0), pipeline_mode=pl.Buffered(1)),
        ],
        out_specs=[pl.BlockSpec((None, tm, n), lambda i, j: (i, j, 0)),
                   pl.BlockSpec((None, tm, d), lambda i, j: (i, j, 0))],
        out_shape=[jax.ShapeDtypeStruct((b, lp, n), F32), jax.ShapeDtypeStruct((b, lp, d), F32)],
        compiler_params=pltpu.CompilerParams(
            dimension_semantics=("arbitrary", "arbitrary"), vmem_limit_bytes=VMEM_LIMIT),
        name="embed_proj",
    )(x, meta, gain.reshape(1, d), w)


def _shift_rows(a, carry_ref, cols, n_back, row):
    out = []
    for s in range(1, n_back + 1):
        sh = pltpu.roll(a, s, 0)
        for r in range(s):
            sh = jnp.where(row == r, carry_ref[8 - s + r:9 - s + r, cols], sh)
        out.append(sh)
    return out


def _merge_pieces(*lists):
    total = max(len(l) for l in lists)
    keyed = []
    for li, l in enumerate(lists):
        for k, piece in enumerate(l):
            keyed.append(((k + 0.5) * total / len(l), li, k, piece))
    keyed.sort(key=lambda t: t[:3])
    return [t[3] for t in keyed]


def _ffn_pieces(make_x, shift, g2_ref, wup_ref, cw_ref, cb_ref, wdn_ref, g3_ref, emit):
    n_tiles = D_FF // FFN_TILE
    st = {}

    def up(c):
        return [_dot(st["u"], wup_ref[:, part * D_FF + c * FFN_TILE:part * D_FF + (c + 1) * FFN_TILE])
                for part in range(2)]

    def head():
        st["x"] = make_x()
        st["u"] = _rms(st["x"], g2_ref[...]).astype(BF)
        st["acts"] = []

    def lead_in():
        st["pending"] = [up(c) for c in range(min(2, n_tiles))]

    def tile(c):
        def run():
            cur = st["pending"].pop(0)
            if c + 2 < n_tiles:
                st["pending"].append(up(c + 2))
            z = []
            for part in range(2):
                cols = slice(part * D_FF + c * FFN_TILE, part * D_FF + (c + 1) * FFN_TILE)
                a = cur[part]
                am1, am2 = shift(a, cols)
                z.append(am2 * cw_ref[0:1, cols] + am1 * cw_ref[1:2, cols] + a * cw_ref[2:3, cols] + cb_ref[:, cols])
            st["acts"].append((_silu(z[0]) * z[1]).astype(BF))
        return run

    def tail():
        acc = _dot(jnp.concatenate(st["acts"], axis=1), wdn_ref[...])
        emit(st["x"] + _rms(acc, g3_ref[...]))

    return [head, lead_in] + [tile(c) for c in range(n_tiles)] + [tail]


def _ffn_body(y_ref, wout_ref, g1_ref, h_ref, g2_ref, wup_ref, cw_ref, cb_ref, wdn_ref, g3_ref, o_ref, carry_ref,
              *, tm, pad):
    @pl.when(pl.program_id(1) == 0)
    def _():
        carry_ref[...] = jnp.zeros_like(carry_ref)

    half = tm // 2
    lists = []
    for r0 in (0, half):
        rows = slice(r0, r0 + half)
        t = pl.program_id(1) * tm + r0 + lax.broadcasted_iota(jnp.int32, (half, D_MODEL), 0)
        row = lax.broadcasted_iota(jnp.int32, (half, FFN_TILE), 0)

        def make_x(rows=rows, t=t):
            return jnp.where(t >= pad, h_ref[rows, :] + _rms(_dot(y_ref[rows, :], wout_ref[...]), g1_ref[...]), 0.0)

        def shift(a, cols, row=row):
            am1, am2 = _shift_rows(a, carry_ref, cols, FFN_CONV - 1, row)
            carry_ref[:, cols] = a[half - 8:half, :]
            return am1, am2

        def emit(out, rows=rows, t=t):
            o_ref[rows, :] = jnp.where(t >= pad, out, 0.0)

        lists.append(_ffn_pieces(make_x, shift, g2_ref, wup_ref, cw_ref, cb_ref, wdn_ref, g3_ref, emit))
    for piece in _merge_pieces(*lists):
        piece()


def _ffn_tail_body(y_ref, wout_ref, g1_ref, h_ref, g2_ref, wup_ref, cw_ref, cb_ref, wdn_ref, g3_ref, o_ref, *,
                   halo):
    half = o_ref.shape[0] // 2
    shift = lambda a, cols: (pltpu.roll(a, 1, 0), pltpu.roll(a, 2, 0))
    lists = []
    for r0 in (0, half):
        def make_x(r0=r0):
            rows = slice(r0, r0 + half + halo)
            return h_ref[0, rows, :] + _rms(_dot(y_ref[0, rows, :], wout_ref[...]), g1_ref[...])

        def emit(out, r0=r0):
            o_ref[r0:r0 + half, :] = out[halo:, :]

        lists.append(_ffn_pieces(make_x, shift, g2_ref, wup_ref, cw_ref, cb_ref, wdn_ref, g3_ref, emit))
    for piece in _merge_pieces(*lists):
        piece()


def _ffn_weight_specs(k, d, ff2):
    const = lambda i, j: (0, 0)
    return dict(
        w_out=pl.BlockSpec((k, d), const, pipeline_mode=pl.Buffered(1)), gain=pl.BlockSpec((1, d), const),
        w_up=pl.BlockSpec((d, ff2), const, pipeline_mode=pl.Buffered(1)),
        conv_w=pl.BlockSpec((FFN_CONV, ff2), const), conv_b=pl.BlockSpec((1, ff2), const),
        w_down=pl.BlockSpec((ff2 // 2, d), const, pipeline_mode=pl.Buffered(1)))


def _out_ffn(y, w_out, g1, h, g2, w_up, conv_w, conv_b, w_down, g3, tm, pad):
    b, lp, d = h.shape
    k = y.shape[2]
    ff2 = w_up.shape[1]
    assert lp % tm == 0 and tm % 16 == 0
    ws = _ffn_weight_specs(k, d, ff2)
    rows = lambda width: pl.BlockSpec((None, tm, width), lambda i, j: (i, j, 0))
    return pl.pallas_call(
        functools.partial(_ffn_body, tm=tm, pad=pad),
        grid=(b, lp // tm),
        in_specs=[rows(k), ws["w_out"], ws["gain"], rows(d), ws["gain"], ws["w_up"], ws["conv_w"], ws["conv_b"],
                  ws["w_down"], ws["gain"]],
        out_specs=rows(d),
        out_shape=jax.ShapeDtypeStruct((b, lp, d), F32),
        scratch_shapes=[pltpu.VMEM((8, ff2), F32)],
        compiler_params=pltpu.CompilerParams(
            dimension_semantics=("arbitrary", "arbitrary"), vmem_limit_bytes=VMEM_LIMIT),
        name="conv_ffn",
    )(y, w_out, g1.reshape(1, d), h, g2.reshape(1, d), w_up, conv_w, conv_b.reshape(1, ff2), w_down,
      g3.reshape(1, d))


def _out_ffn_tail(y, w_out, g1, h, g2, w_up, conv_w, conv_b, w_down, g3, tm, lead):
    b, lp, d = h.shape
    seq = lp - lead
    k = y.shape[2]
    ff2 = w_up.shape[1]
    halo = 8
    assert seq % tm == 0 and tm % 16 == 0 and lead >= halo and lead % 8 == 0
    ws = _ffn_weight_specs(k, d, ff2)
    rows = lambda width: pl.BlockSpec(
        (pl.Element(1), pl.Element(tm + halo), pl.Element(width)),
        lambda i, j: (i, pl.multiple_of(lead - halo + tm * j, 8), 0))
    return pl.pallas_call(
        functools.partial(_ffn_tail_body, halo=halo),
        grid=(b, seq // tm),
        in_specs=[rows(k), ws["w_out"], ws["gain"], rows(d), ws["gain"], ws["w_up"], ws["conv_w"], ws["conv_b"],
                  ws["w_down"], ws["gain"]],
        out_specs=pl.BlockSpec((None, tm, d), lambda i, j: (i, j, 0)),
        out_shape=jax.ShapeDtypeStruct((b, seq, d), F32),
        compiler_params=pltpu.CompilerParams(
            dimension_semantics=("arbitrary", "arbitrary"), vmem_limit_bytes=VMEM_LIMIT),
        name="conv_ffn_tail",
    )(y, w_out, g1.reshape(1, d), h, g2.reshape(1, d), w_up, conv_w, conv_b.reshape(1, ff2), w_down,
      g3.reshape(1, d))


def _hg_consts():
    c = CHUNK
    t = np.arange(c)
    m = np.zeros((2 + len(HG_LEVELS), c, c), np.float32)
    m[0] = t[None, :] <= t[:, None]
    m[1] = t[None, :] > t[:, None]
    masks = np.zeros((len(HG_LEVELS) + 1, c, c), np.float32)
    ii, jj = t[:, None], t[None, :]
    for li, s in enumerate(HG_LEVELS):
        for i in range(c):
            mid = (i // (2 * s)) * (2 * s) + s - 1
            if i % (2 * s) >= s:
                m[2 + li, i, mid + 1:i + 1] = 1.0
            else:
                m[2 + li, i, i + 1:mid + 1] = 1.0
        masks[li] = (ii // (2 * s) == jj // (2 * s)) & (ii % (2 * s) >= s) & (jj % (2 * s) < s)
    masks[-1] = ii == jj
    return m.reshape(-1, c), masks


def _even_body(p_ref, cos_ref, sin_ref, lbl_ref, hgain_ref, mall_ref, masks_ref, y_ref, sret_ref, shg_ref, *,
               layer):
    @pl.when(pl.program_id(1) == 0)
    def _():
        sret_ref[...] = jnp.zeros_like(sret_ref)
        shg_ref[...] = jnp.zeros_like(shg_ref)

    c = CHUNK
    cosv = cos_ref[...]
    sinv = sin_ref[...]
    ii = lax.broadcasted_iota(jnp.int32, (c, c), 0)
    jj = lax.broadcasted_iota(jnp.int32, (c, c), 1)
    diff = (ii - jj).astype(F32)
    rowi = lax.broadcasted_iota(jnp.int32, (c, RET_DK), 0).astype(F32)

    nb = p_ref.shape[0]
    col = lambda base, h: slice(base + h * 128, base + (h + 1) * 128)
    units = [(b, h) for b in range(nb) for h in range(RET_HEADS)]
    nu = len(units)
    lgs = [math.log1p(-(2.0 ** (-5.0 - h))) for _, h in units]
    qs = [p_ref[b, :, col(0, h)] for b, h in units]
    ks = [p_ref[b, :, col(RET_W, h)] for b, h in units]
    vs = [p_ref[b, :, col(2 * RET_W, h)] for b, h in units]
    qs = [q * cosv + pltpu.roll(q, RET_DK // 2, 1) * sinv for q in qs]
    ks = [(k * cosv + pltpu.roll(k, RET_DK // 2, 1) * sinv) * (RET_DK ** -0.5) for k in ks]
    ss = [sret_ref[b, h] for b, h in units]
    att = [_mm_nt(qs[u], ks[u]) * jnp.where(diff >= 0, jnp.exp(lgs[u] * jnp.maximum(diff, 0.0)), 0.0)
           for u in range(nu)]
    inter = [_mm(qs[u] * jnp.exp(lgs[u] * (rowi + 1.0)), ss[u]) for u in range(nu)]
    kv = [_mm_tn(ks[u] * jnp.exp(lgs[u] * (c - 1.0 - rowi)), vs[u]) for u in range(nu)]
    outs = [_mm(att[u], vs[u]) + inter[u] for u in range(nu)]
    for u, (b, h) in enumerate(units):
        sret_ref[b, h] = ss[u] * math.exp(lgs[u] * c) + kv[u]
        o = outs[u]
        xc = o - jnp.mean(o, axis=-1, keepdims=True)
        oa = xc * lax.rsqrt(jnp.mean(xc * xc, axis=-1, keepdims=True) + EPS)
        y_ref[b, :, col(0, h)] = (oa * _silu(p_ref[b, :, col(3 * RET_W, h)])).astype(y_ref.dtype)

    lgt = lbl_ref[...]
    mx = jnp.max(lgt, axis=0, keepdims=True)
    ex = jnp.exp(lgt - mx)
    sm = ex / jnp.sum(ex, axis=0, keepdims=True)
    lb_all = jnp.sum(sm[0:layer + 1, :], axis=0, keepdims=True)
    base = 4 * RET_W
    mall = mall_ref[...]
    nl = len(HG_LEVELS)
    units = [(b, h) for b in range(nb) for h in range(HG_HEADS)]
    nu = len(units)
    lbs = [lb_all[:, col(0, h)] for _, h in units]
    qs = [p_ref[b, :, col(base, h)] for b, h in units]
    fbs = [p_ref[b, :, col(base + HG_W, h)] for b, h in units]
    vs = [p_ref[b, :, col(base + 2 * HG_W, h)] for b, h in units]
    ks = [(1.0 - lbs[u]) * _sigmoid(-fbs[u]) for u in range(nu)]
    xs = [jnp.exp(_cmm2(mall, jnp.log(lbs[u] + (1.0 - lbs[u]) * _sigmoid(fbs[u])))) for u in range(nu)]
    sts = [shg_ref[b, h] for b, h in units]
    att = [masks_ref[nl] * _mm_nt(qs[u], ks[u]) for u in range(nu)]
    for li in range(nl):
        lv = slice((2 + li) * c, (3 + li) * c)
        att = [att[u] + masks_ref[li] * _mm_nt(qs[u] * xs[u][lv, :], ks[u] * xs[u][lv, :]) for u in range(nu)]
    inter = [_mm_nt(qs[u] * xs[u][0:c, :], sts[u]) for u in range(nu)]
    kv = [_mm_tn(vs[u], ks[u] * xs[u][c:2 * c, :]) for u in range(nu)]
    outs = [_mm(att[u], vs[u]) + inter[u] for u in range(nu)]
    for u, (b, h) in enumerate(units):
        shg_ref[b, h] = sts[u] * xs[u][c - 1:c, :] + kv[u]
        ob = _rms(outs[u], hgain_ref[:, col(0, h)])
        y_ref[b, :, col(RET_W, h)] = (ob * _silu(p_ref[b, :, col(base + 3 * HG_W, h)])).astype(y_ref.dtype)


def _even_mixer(p, cos_t, sin_t, lb_logits, hg_gain, layer):
    b, lp, n = p.shape
    nc = lp // CHUNK
    nb = EVEN_NB if b % EVEN_NB == 0 else 1
    mall, masks = _hg_consts()
    const2 = lambda i, j: (0, 0)
    return pl.pallas_call(
        functools.partial(_even_body, layer=layer),
        grid=(b // nb, nc),
        in_specs=[
            pl.BlockSpec((nb, CHUNK, n), lambda i, j: (i, j, 0)),
            pl.BlockSpec((CHUNK, RET_DK), lambda i, j: (j, 0)),
            pl.BlockSpec((CHUNK, RET_DK), lambda i, j: (j, 0)),
            pl.BlockSpec(lb_logits.shape, const2),
            pl.BlockSpec((1, HG_W), const2),
            pl.BlockSpec(mall.shape, const2),
            pl.BlockSpec(masks.shape, lambda i, j: (0, 0, 0)),
        ],
        out_specs=pl.BlockSpec((nb, CHUNK, RET_W + HG_W), lambda i, j: (i, j, 0)),
        out_shape=jax.ShapeDtypeStruct((b, lp, RET_W + HG_W), BF),
        scratch_shapes=[pltpu.VMEM((nb, RET_HEADS, RET_DK, RET_DK), F32),
                        pltpu.VMEM((nb, HG_HEADS, HG_DK, HG_DK), F32)],
        compiler_params=pltpu.CompilerParams(
            dimension_semantics=("arbitrary", "arbitrary"), vmem_limit_bytes=VMEM_LIMIT),
        name="even_mixer",
    )(p, cos_t, sin_t, lb_logits, hg_gain.reshape(1, HG_W), jnp.asarray(mall, BF), jnp.asarray(masks))


def _odd_body(p_ref, mu_ref, w0_ref, a0_ref, wa2_ref, g2_ref, kks_ref, kas_ref, rk_ref, lnw_ref, lnb_ref,
              cw_ref, alog_ref, dtb_ref, ggain_ref, tril_ref, bones_ref, y_ref,
              hrw_ref, sgd_ref, cpc_ref, ccv_ref):
    @pl.when(pl.program_id(1) == 0)
    def _():
        hrw_ref[...] = jnp.zeros_like(hrw_ref)
        sgd_ref[...] = jnp.zeros_like(sgd_ref)
        cpc_ref[...] = jnp.zeros_like(cpc_ref)
        ccv_ref[...] = jnp.zeros_like(ccv_ref)

    nb = p_ref.shape[0]
    c = CHUNK
    tril = tril_ref[...]
    bones = bones_ref[...]
    ii = lax.broadcasted_iota(jnp.int32, (c, c), 0)
    jj = lax.broadcasted_iota(jnp.int32, (c, c), 1)
    strict = ii > jj
    incl = ii >= jj
    eye = ii == jj
    eye_f = eye.astype(F32)
    ii2 = lax.broadcasted_iota(jnp.int32, (c, 2 * c), 0)
    lane2 = lax.broadcasted_iota(jnp.int32, (c, 2 * c), 1)
    jj2 = jnp.bitwise_and(lane2, c - 1)
    ak_mask = (ii2 > jj2) & (lane2 >= c)
    rbk_mask = ii2 >= jj2
    lane = lax.broadcasted_iota(jnp.int32, (c, 128), 1)
    heads = [slice(h * RW_HD, (h + 1) * RW_HD) for h in range(RW_HEADS)]
    base = RW_IN
    inv_hd = 1.0 / RW_HD

    def seg_sum(x):
        return jnp.concatenate(
            [_mmc2(x[:, p * 128:(p + 1) * 128], bones) for p in range(x.shape[1] // 128)], axis=1)

    def prep_pieces(b, d):
        def shift():
            pc = p_ref[b, :, 0:RW_IN]
            row = lax.broadcasted_iota(jnp.int32, (c, RW_IN), 0)
            prev = jnp.where(row == 0, cpc_ref[b, 7:8, :], pltpu.roll(pc, 1, 0))
            cpc_ref[b] = pc[c - 8:c, :]
            pcs = pc + (prev - pc) * mu_ref[...]
            d.update(r=pcs[:, 0:RW_W], k=pcs[:, RW_W:2 * RW_W], v=pcs[:, 2 * RW_W:3 * RW_W],
                     lo=pcs[:, 3 * RW_W:3 * RW_W + 128], glo=pcs[:, 3 * RW_W + 128:RW_IN])

        def lora():
            lo = d["lo"]
            wa = _mm(jnp.where(lane < RW_DECAY_LORA, jnp.tanh(lo), lo), wa2_ref[...])
            log_w = -_softplus(-(w0_ref[...] + wa[:, 0:RW_W])) - 0.5
            d["lw"] = -jnp.exp(log_w)
            d["a"] = _sigmoid(a0_ref[...] + wa[:, RW_W:2 * RW_W])
            d["gate"] = _mm(_sigmoid(d["glo"]), g2_ref[...])

        def keys():
            kks = d["k"] * kks_ref[...]
            kkn = kks * lax.rsqrt(seg_sum(kks * kks) + EPS)
            d["k2"] = d["k"] * (1.0 + (d["a"] - 1.0) * kas_ref[...])
            d["kkn"] = kkn
            d["beta"] = kkn * d["a"]
            d["c_inc"] = _cmm3(tril, d["lw"])

        def decays():
            c_inc, k2, beta = d["c_inc"], d["k2"], d["beta"]
            c_last = c_inc[c - 1:c, :]
            e_neg = jnp.exp(-c_inc)
            e_rest = jnp.exp(c_last - c_inc)
            d.update(ah=-d["kkn"] * jnp.exp(c_inc - d["lw"]), rh=d["r"] * jnp.exp(c_inc), bt=beta * e_neg,
                     kt=k2 * e_neg, bg=beta * e_rest, kg=k2 * e_rest, gc=jnp.exp(c_last),
                     ar=[], ls=[], ak=[], rbk=[], gcol=[], bgkg=[])

        def gram(h):
            def run():
                sl = heads[h]
                ar = jnp.concatenate([d["ah"][:, sl], d["rh"][:, sl]], axis=0).astype(BF)
                gm = _mm_nt(ar, jnp.concatenate([d["bt"][:, sl], d["kt"][:, sl]], axis=0))
                d["ar"].append(ar)
                d["ls"].append(jnp.where(strict, gm[0:c, 0:c], 0.0))
                d["ak"].append(jnp.where(ak_mask, gm[0:c, :], 0.0))
                d["rbk"].append(jnp.where(rbk_mask, gm[c:2 * c, :], 0.0))
                d["gcol"].append(jnp.sum(eye_f * d["gc"][:, sl], axis=1, keepdims=True))
                d["bgkg"].append(jnp.concatenate([d["bg"][:, sl], d["kg"][:, sl]], axis=0).astype(BF))
            return run

        def conv():
            x = p_ref[b, :, base:base + GDN_CONV_CH]
            rowc = lax.broadcasted_iota(jnp.int32, (c, GDN_CONV_CH), 0)
            xm1, xm2, xm3 = _shift_rows(x, ccv_ref.at[b], slice(None), GDN_CONV - 1, rowc)
            ccv_ref[b] = x[c - 8:c, :]
            d["qkv"] = _silu(xm3 * cw_ref[0:1, :] + xm2 * cw_ref[1:2, :] + xm1 * cw_ref[2:3, :] + x * cw_ref[3:4, :])

        def gates():
            sc = p_ref[b, :, base + GDN_CONV_CH + GDN_W:base + GDN_IN_PAD]
            g_all = -jnp.exp(alog_ref[...]) * _softplus(sc + dtb_ref[...])
            d["b_all"] = _sigmoid(sc)
            g_b = jnp.concatenate(
                [jnp.broadcast_to(g_all[:, h:h + 1], (c, 128)) for h in range(GDN_HEADS)], axis=1)
            d["cg_all"] = _cmm3(tril, g_b)
            d["gd"] = []

        def gdn_head(h):
            def run():
                hs = slice(h * 128, (h + 1) * 128)
                qkv = d["qkv"]
                q = qkv[:, hs]
                kd = qkv[:, GDN_W + h * 128:GDN_W + (h + 1) * 128]
                vd = qkv[:, 2 * GDN_W + h * 128:2 * GDN_W + (h + 1) * 128]
                q = q * lax.rsqrt(jnp.sum(q * q, axis=-1, keepdims=True) + EPS) * (GDN_DK ** -0.5)
                kd = kd * lax.rsqrt(jnp.sum(kd * kd, axis=-1, keepdims=True) + EPS)
                b_b = jnp.broadcast_to(d["b_all"][:, GDN_HEADS + h:GDN_HEADS + h + 1], (c, 128))
                cg = d["cg_all"][:, hs]
                cg_row = jnp.sum(jnp.where(eye, cg[:, 0:c], 0.0), axis=0, keepdims=True)
                decay = jnp.exp(jnp.where(incl, cg[:, 0:c] - cg_row, -jnp.inf))
                eg = jnp.exp(cg)
                cl = cg[c - 1:c, :]
                d["gd"].append(dict(
                    l=-jnp.where(strict, b_b[:, 0:c] * _mm_nt(kd, kd) * decay, 0.0),
                    rhs=jnp.concatenate([b_b * vd, b_b * kd * eg], axis=1), qk=_mm_nt(q, kd) * decay,
                    q_in=q * eg, k_out=kd * jnp.exp(cl - cg), sd=jnp.exp(cl)))
            return run

        return ([shift, lora, keys, decays] + [gram(h) for h in range(RW_HEADS)]
                + [conv, gates] + [gdn_head(h) for h in range(GDN_HEADS)])

    def inverse_pieces(rows, ds, out):
        st = {}

        def start():
            st["l"] = [l for b in rows for l in ds[b]["ls"]] + [g["l"] for b in rows for g in ds[b]["gd"]]
            st["p"] = [eye_f + l for l in st["l"]]

        def square():
            st["l"] = [_mm(l, l) for l in st["l"]]

        def extend():
            st["p"] = [p + _mm(l, p) for l, p in zip(st["l"], st["p"])]

        def finish():
            out["tinv"] = st["p"]

        levels = []
        span = 2
        while span < c:
            levels += [square, extend]
            span *= 2
        return [start] + levels + [finish]

    def state_pieces(rows, ds, inv):
        units = [(b, h) for b in rows for h in range(RW_HEADS)]
        gunits = [(b, h) for b in rows for h in range(GDN_HEADS)]
        st = {}

        def rw_read():
            st["h0"] = [hrw_ref[b, h] for b, h in units]
            st["hr"] = [_mm(ds[b]["ar"][h], st["h0"][u]) for u, (b, h) in enumerate(units)]
            st["vs"] = [ds[b]["v"][:, heads[h]] for b, h in units]

        def rw_mix():
            vs = st["vs"]
            st["x"] = [st["hr"][u][0:c, :] + _mm(ds[b]["ak"][h], jnp.concatenate([vs[u], vs[u]], axis=0))
                       for u, (b, h) in enumerate(units)]

        def rw_solve():
            us = [_mm(inv["tinv"][u], st["x"][u]) for u in range(len(units))]
            st["uv"] = [jnp.concatenate([us[u], st["vs"][u]], axis=0).astype(BF) for u in range(len(units))]

        def rw_out():
            st["ys"] = [st["hr"][u][c:2 * c, :] + _mm(ds[b]["rbk"][h], st["uv"][u])
                        for u, (b, h) in enumerate(units)]
            for u, (b, h) in enumerate(units):
                hrw_ref[b, h] = st["h0"][u] * ds[b]["gcol"][h] + _mm_tn(ds[b]["bgkg"][h], st["uv"][u])

        def rw_finish(k, b):
            def run():
                d = ds[b]
                y = jnp.concatenate(st["ys"][k * RW_HEADS:(k + 1) * RW_HEADS], axis=1)
                yc = y - seg_sum(y) * inv_hd
                yn = yc * lax.rsqrt(seg_sum(yc * yc) * inv_hd + RW_LNX_EPS) * lnw_ref[...] + lnb_ref[...]
                out = (yn + seg_sum(d["r"] * d["k2"] * rk_ref[...]) * d["v"]) * d["gate"]
                y_ref[b, :, 0:RW_W] = out.astype(y_ref.dtype)
            return run

        def gd_solve():
            n_rw = len(units)
            sols = [_mm(inv["tinv"][n_rw + u], ds[b]["gd"][h]["rhs"]) for u, (b, h) in enumerate(gunits)]
            st["s0"] = [sgd_ref[b, h] for b, h in gunits]
            st["v_new"] = [sols[u][:, 0:128] - _mm(sols[u][:, 128:256], st["s0"][u]) for u in range(len(gunits))]

        def gd_out(u, b, h):
            def run():
                g = ds[b]["gd"][h]
                s0, v_new = st["s0"][u], st["v_new"][u]
                o = _mm(g["q_in"], s0) + _mm(g["qk"], v_new)
                sgd_ref[b, h] = s0 * g["sd"] + _mm_tn(g["k_out"], v_new)
                og = p_ref[b, :, base + GDN_CONV_CH + h * 128:base + GDN_CONV_CH + (h + 1) * 128]
                out = _rms(o, ggain_ref[...]) * _silu(og)
                y_ref[b, :, RW_W + h * 128:RW_W + (h + 1) * 128] = out.astype(y_ref.dtype)
            return run

        return ([rw_read, rw_mix, rw_solve, rw_out] + [rw_finish(k, b) for k, b in enumerate(rows)]
                + [gd_solve] + [gd_out(u, b, h) for u, (b, h) in enumerate(gunits)])

    group = min(ODD_GROUP, nb)
    groups = [list(range(g, g + group)) for g in range(0, nb, group)]
    ds = [dict() for _ in range(nb)]
    invs = [dict() for _ in groups]
    stage1 = [_merge_pieces(*[prep_pieces(b, ds[b]) for b in rows]) for rows in groups]
    stage2 = [inverse_pieces(rows, ds, invs[g]) for g, rows in enumerate(groups)]
    stage3 = [state_pieces(rows, ds, invs[g]) for g, rows in enumerate(groups)]
    for slot in range(len(groups) + 2):
        active = []
        if slot < len(groups):
            active.append(stage1[slot])
        if 0 <= slot - 1 < len(groups):
            active.append(stage2[slot - 1])
        if 0 <= slot - 2 < len(groups):
            active.append(stage3[slot - 2])
        for piece in _merge_pieces(*active):
            piece()


def _odd_mixer(p, mu, w0, a0, wa2, g2, kks, kas, rk, lnw, lnb, conv_w, alog, dtb, ggain):
    b, lp, n = p.shape
    nc = lp // CHUNK
    nb = ODD_NB if b % ODD_NB == 0 else 1
    t = np.arange(CHUNK)
    tril = jnp.asarray(t[None, :] <= t[:, None], BF)
    l = np.arange(128)
    bones = jnp.asarray(l[:, None] // RW_HD == l[None, :] // RW_HD, BF)
    row = lambda a: a.reshape(1, -1)
    consts = [row(mu), row(w0), row(a0), wa2, g2, row(kks), row(kas), row(rk), row(lnw), row(lnb),
              conv_w, row(alog), row(dtb), row(ggain), tril, bones]
    const2 = lambda i, j: (0, 0)
    return pl.pallas_call(
        _odd_body,
        grid=(b // nb, nc),
        in_specs=[pl.BlockSpec((nb, CHUNK, n), lambda i, j: (i, j, 0))]
        + [pl.BlockSpec(a.shape, const2) for a in consts],
        out_specs=pl.BlockSpec((nb, CHUNK, RW_W + GDN_W), lambda i, j: (i, j, 0)),
        out_shape=jax.ShapeDtypeStruct((b, lp, RW_W + GDN_W), BF),
        scratch_shapes=[
            pltpu.VMEM((nb, RW_HEADS, RW_HD, RW_HD), F32),
            pltpu.VMEM((nb, GDN_HEADS, GDN_DK, GDN_DK), F32),
            pltpu.VMEM((nb, 8, RW_IN), F32),
            pltpu.VMEM((nb, 8, GDN_CONV_CH), F32),
        ],
        compiler_params=pltpu.CompilerParams(
            dimension_semantics=("arbitrary", "arbitrary"), vmem_limit_bytes=VMEM_LIMIT),
        name="odd_mixer",
    )(p, *consts)


def _row_tile(lp):
    best = None
    for tm in range(16, min(lp, 704) + 1, 16):
        if lp % tm == 0:
            best = tm
    assert best is not None, "sequence length must be a multiple of 16"
    return best


def kernel(x, meta_tokens, norm_gains, w_in_even, w_out_even, hg_lb_logits, hg_norm_gain, w_in_odd, w_out_odd, rw_mu, rw_w0, rw_w2, rw_a0, rw_a2, rw_g2, rw_kk_scale, rw_ka_scale, rw_rk, rw_lnx_w, rw_lnx_b, gdn_conv_w, gdn_a_log, gdn_dt_bias, gdn_norm_gain, ffn_w_up, ffn_conv_w, ffn_conv_b, ffn_w_down):
    bsz, seq, d = x.shape
    depth = norm_gains.shape[0]
    l = N_META + seq
    pad = (-l) % CHUNK
    lp = l + pad
    tm = _row_tile(lp)
    lead = pad + N_META

    half = RET_DK // 2
    pos = (jnp.arange(lp, dtype=jnp.int32) - pad).astype(F32)
    inv = ROPE_BASE ** (-jnp.arange(half, dtype=F32) / half)
    ang = pos[:, None] * inv[None, :]
    cos_t = jnp.concatenate([jnp.cos(ang), jnp.cos(ang)], axis=1)
    sin_t = jnp.concatenate([-jnp.sin(ang), jnp.sin(ang)], axis=1)

    h = None
    for layer in range(depth):
        g = norm_gains[layer]
        i = layer // 2
        if layer % 2 == 0:
            w_in = w_in_even[i].astype(BF)
        else:
            w_in = jnp.pad(w_in_odd[i], ((0, 0), (0, ODD_IN_PAD - w_in_odd.shape[2]))).astype(BF)
        if layer == 0:
            p, h = _embed_proj(x, meta_tokens.astype(x.dtype), g[0], w_in, tm, lead)
        else:
            p = _norm_proj(h, g[0], w_in, tm)
        if layer % 2 == 0:
            y = _even_mixer(p, cos_t, sin_t, hg_lb_logits, hg_norm_gain[i], layer)
            w_out = w_out_even[i]
        else:
            wa2 = jnp.zeros((RW_DECAY_LORA + RW_AAA_LORA, 2 * RW_W), F32)
            wa2 = wa2.at[:RW_DECAY_LORA, :RW_W].set(rw_w2[i]).at[RW_DECAY_LORA:, RW_W:].set(rw_a2[i])
            lane_pad = lambda a: jnp.pad(a, (0, LANES - a.shape[0]))
            y = _odd_mixer(p, rw_mu[i], rw_w0[i], rw_a0[i], wa2.astype(BF), rw_g2[i].astype(BF),
                           rw_kk_scale[i], rw_ka_scale[i], rw_rk[i].reshape(-1), rw_lnx_w[i], rw_lnx_b[i],
                           gdn_conv_w[i], lane_pad(gdn_a_log[i]), lane_pad(gdn_dt_bias[i]), gdn_norm_gain[i])
            w_out = w_out_odd[i]
        ffn_args = (y, w_out.astype(BF), g[1], h, g[2], ffn_w_up[layer].astype(BF), ffn_conv_w[layer],
                    ffn_conv_b[layer], ffn_w_down[layer].astype(BF), g[3])
        if layer + 1 < depth:
            h = _out_ffn(*ffn_args, tm, pad)
        else:
            h = _out_ffn_tail(*ffn_args, _row_tile(seq), lead)
    return h
```

```python
import functools
import math

import numpy as np
import jax
import jax.numpy as jnp
from jax import lax
from jax.experimental import pallas as pl
from jax.experimental.pallas import tpu as pltpu

F32 = jnp.float32
BF = jnp.bfloat16

D_MODEL = 1024
CHUNK = 64
N_META = 16
EPS = 1e-6
ROPE_BASE = 10000.0

RET_HEADS = D_MODEL // 256
RET_DK = 128
HG_HEADS = D_MODEL // 256
HG_DK = 128
RW_HEADS = D_MODEL // 128
RW_HD = 64
RW_DECAY_LORA = 64
RW_AAA_LORA = 64
RW_GATE_LORA = 128
RW_LNX_EPS = 64e-5
GDN_HEADS = D_MODEL // 256
GDN_DK = 128
GDN_CONV = 4
D_FF = 128 * ((8 * D_MODEL // 3 + 127) // 128)
FFN_CONV = 3

RET_W = RET_HEADS * RET_DK
HG_W = HG_HEADS * HG_DK
EVEN_IN = 4 * RET_W + 4 * HG_W
RW_W = RW_HEADS * RW_HD
RW_IN = 3 * RW_W + RW_DECAY_LORA + RW_AAA_LORA + RW_GATE_LORA
GDN_W = GDN_HEADS * GDN_DK
GDN_CONV_CH = 3 * GDN_W
GDN_IN = GDN_CONV_CH + GDN_W + 2 * GDN_HEADS
LANES = 128
GDN_IN_PAD = LANES * ((GDN_IN + LANES - 1) // LANES)
ODD_IN_PAD = RW_IN + GDN_IN_PAD
HG_LEVELS = (32, 16, 8, 4, 2, 1)
VMEM_LIMIT = 56 * 1024 * 1024
FFN_TILE = 256
EVEN_NB = 4
ODD_NB = 4
ODD_GROUP = 2


def _dot(a, b):
    return jnp.dot(a, b, preferred_element_type=F32)


def _mm(a, b):
    return _dot(a.astype(BF), b.astype(BF))


def _mm_nt(a, b):
    return lax.dot_general(a.astype(BF), b.astype(BF), (((1,), (1,)), ((), ())), preferred_element_type=F32)


def _mm_tn(a, b):
    return lax.dot_general(a.astype(BF), b.astype(BF), (((0,), (0,)), ((), ())), preferred_element_type=F32)


def _split2(x):
    hi = x.astype(BF)
    lo = (x - hi.astype(F32)).astype(BF)
    return hi, lo


def _split3(x):
    hi = x.astype(BF)
    r1 = x - hi.astype(F32)
    mid = r1.astype(BF)
    lo = (r1 - mid.astype(F32)).astype(BF)
    return hi, mid, lo


def _cmm3(c, x):
    hi, mid, lo = _split3(x)
    return _dot(c, hi) + _dot(c, mid) + _dot(c, lo)


def _cmm2(c, x):
    hi, lo = _split2(x)
    return _dot(c, hi) + _dot(c, lo)


def _sigmoid(x):
    return 1.0 / (1.0 + jnp.exp(-x))


def _silu(x):
    return x * _sigmoid(x)


def _softplus(x):
    return jnp.maximum(x, 0.0) + jnp.log1p(jnp.exp(-jnp.abs(x)))


def _rms(x, g):
    return x * lax.rsqrt(jnp.mean(x * x, axis=-1, keepdims=True) + EPS) * g


def _norm_proj_body(h_ref, g_ref, w_ref, o_ref):
    u = _rms(h_ref[...], g_ref[...])
    o_ref[...] = _dot(u.astype(BF), w_ref[...])


def _norm_proj(h, gain, w, tm):
    b, lp, d = h.shape
    n = w.shape[1]
    return pl.pallas_call(
        _norm_proj_body,
        grid=(b, lp // tm),
        in_specs=[
            pl.BlockSpec((None, tm, d), lambda i, j: (i, j, 0)),
            pl.BlockSpec((1, d), lambda i, j: (0, 0)),
            pl.BlockSpec((d, n), lambda i, j: (0, 0), pipeline_mode=pl.Buffered(1)),
        ],
        out_specs=pl.BlockSpec((None, tm, n), lambda i, j: (i, j, 0)),
        out_shape=jax.ShapeDtypeStruct((b, lp, n), F32),
        compiler_params=pltpu.CompilerParams(
            dimension_semantics=("arbitrary", "arbitrary"), vmem_limit_bytes=VMEM_LIMIT),
        name="norm_proj",
    )(h, gain.reshape(1, d), w)


def _embed_proj_body(x_ref, meta_ref, g_ref, w_ref, p_ref, h_ref, *, lead):
    xb = x_ref[0]
    tm, d = xb.shape
    n_zero = lead - meta_ref.shape[0]
    first = jnp.concatenate([jnp.zeros((n_zero, d), F32), meta_ref[...], xb[0:tm - lead, :]], axis=0)
    blk = jnp.where(pl.program_id(1) == 0, first, xb)
    h_ref[...] = blk
    p_ref[...] = _dot(_rms(blk, g_ref[...]).astype(BF), w_ref[...])


def _embed_proj(x, meta, gain, w, tm, lead):
    b, s, d = x.shape
    lp = s + lead
    n = w.shape[1]
    assert lead % 8 == 0 and meta.shape[0] % 8 == 0 and lead < tm and lp % tm == 0
    x_rows = lambda i, j: (i, pl.multiple_of(jnp.maximum(tm * j - lead, 0), 8), 0)
    return pl.pallas_call(
        functools.partial(_embed_proj_body, lead=lead),
        grid=(b, lp // tm),
        in_specs=[
            pl.BlockSpec((pl.Element(1), pl.Element(tm), pl.Element(d)), x_rows),
            pl.BlockSpec(meta.shape, lambda i, j: (0, 0)),
            pl.BlockSpec((1, d), lambda i, j: (0, 0)),
            pl.BlockSpec((d, n), lambda i, j: (0, 0), pipeline_mode=pl.Buffered(1)),
        ],
        out_specs=[pl.BlockSpec((None, tm, n), lambda i, j: (i, j, 0)),
                   pl.BlockSpec((None, tm, d), lambda i, j: (i, j, 0))],
        out_shape=[jax.ShapeDtypeStruct((b, lp, n), F32), jax.ShapeDtypeStruct((b, lp, d), F32)],
        compiler_params=pltpu.CompilerParams(
            dimension_semantics=("arbitrary", "arbitrary"), vmem_limit_bytes=VMEM_LIMIT),
        name="embed_proj",
    )(x, meta, gain.reshape(1, d), w)


def _shift_rows(a, carry_ref, cols, n_back, row):
    out = []
    for s in range(1, n_back + 1):
        sh = pltpu.roll(a, s, 0)
        for r in range(s):
            sh = jnp.where(row == r, carry_ref[8 - s + r:9 - s + r, cols], sh)
        out.append(sh)
    return out


def _merge_pieces(*lists):
    total = max(len(l) for l in lists)
    keyed = []
    for li, l in enumerate(lists):
        for k, piece in enumerate(l):
            keyed.append(((k + 0.5) * total / len(l), li, k, piece))
    keyed.sort(key=lambda t: t[:3])
    return [t[3] for t in keyed]


def _ffn_pieces(make_x, shift, g2_ref, wup_ref, cw_ref, cb_ref, wdn_ref, g3_ref, emit):
    n_tiles = D_FF // FFN_TILE
    st = {}

    def up(c):
        return [_dot(st["u"], wup_ref[:, part * D_FF + c * FFN_TILE:part * D_FF + (c + 1) * FFN_TILE])
                for part in range(2)]

    def head():
        st["x"] = make_x()
        st["u"] = _rms(st["x"], g2_ref[...]).astype(BF)
        st["acts"] = []

    def lead_in():
        st["pending"] = [up(c) for c in range(min(2, n_tiles))]

    def tile(c):
        def run():
            cur = st["pending"].pop(0)
            if c + 2 < n_tiles:
                st["pending"].append(up(c + 2))
            z = []
            for part in range(2):
                cols = slice(part * D_FF + c * FFN_TILE, part * D_FF + (c + 1) * FFN_TILE)
                a = cur[part]
                am1, am2 = shift(a, cols)
                z.append(am2 * cw_ref[0:1, cols] + am1 * cw_ref[1:2, cols] + a * cw_ref[2:3, cols] + cb_ref[:, cols])
            st["acts"].append((_silu(z[0]) * z[1]).astype(BF))
        return run

    def tail():
        acc = _dot(jnp.concatenate(st["acts"], axis=1), wdn_ref[...])
        emit(st["x"] + _rms(acc, g3_ref[...]))

    return [head, lead_in] + [tile(c) for c in range(n_tiles)] + [tail]


def _ffn_body(y_ref, wout_ref, g1_ref, h_ref, g2_ref, wup_ref, cw_ref, cb_ref, wdn_ref, g3_ref, o_ref, carry_ref,
              *, tm, pad):
    @pl.when(pl.program_id(1) == 0)
    def _():
        carry_ref[...] = jnp.zeros_like(carry_ref)

    half = tm // 2
    lists = []
    for r0 in (0, half):
        rows = slice(r0, r0 + half)
        t = pl.program_id(1) * tm + r0 + lax.broadcasted_iota(jnp.int32, (half, D_MODEL), 0)
        row = lax.broadcasted_iota(jnp.int32, (half, FFN_TILE), 0)

        def make_x(rows=rows, t=t):
            return jnp.where(t >= pad, h_ref[rows, :] + _rms(_dot(y_ref[rows, :], wout_ref[...]), g1_ref[...]), 0.0)

        def shift(a, cols, row=row):
            am1, am2 = _shift_rows(a, carry_ref, cols, FFN_CONV - 1, row)
            carry_ref[:, cols] = a[half - 8:half, :]
            return am1, am2

        def emit(out, rows=rows, t=t):
            o_ref[rows, :] = jnp.where(t >= pad, out, 0.0)

        lists.append(_ffn_pieces(make_x, shift, g2_ref, wup_ref, cw_ref, cb_ref, wdn_ref, g3_ref, emit))
    for piece in _merge_pieces(*lists):
        piece()


def _ffn_tail_body(y_ref, wout_ref, g1_ref, h_ref, g2_ref, wup_ref, cw_ref, cb_ref, wdn_ref, g3_ref, o_ref, *,
                   halo):
    half = o_ref.shape[0] // 2
    shift = lambda a, cols: (pltpu.roll(a, 1, 0), pltpu.roll(a, 2, 0))
    lists = []
    for r0 in (0, half):
        def make_x(r0=r0):
            rows = slice(r0, r0 + half + halo)
            return h_ref[0, rows, :] + _rms(_dot(y_ref[0, rows, :], wout_ref[...]), g1_ref[...])

        def emit(out, r0=r0):
            o_ref[r0:r0 + half, :] = out[halo:, :]

        lists.append(_ffn_pieces(make_x, shift, g2_ref, wup_ref, cw_ref, cb_ref, wdn_ref, g3_ref, emit))
    for piece in _merge_pieces(*lists):
        piece()


def _ffn_weight_specs(k, d, ff2):
    const = lambda i, j: (0, 0)
    return dict(
        w_out=pl.BlockSpec((k, d), const, pipeline_mode=pl.Buffered(1)), gain=pl.BlockSpec((1, d), const),
        w_up=pl.BlockSpec((d, ff2), const, pipeline_mode=pl.Buffered(1)),
        conv_w=pl.BlockSpec((FFN_CONV, ff2), const), conv_b=pl.BlockSpec((1, ff2), const),
        w_down=pl.BlockSpec((ff2 // 2, d), const, pipeline_mode=pl.Buffered(1)))


def _out_ffn(y, w_out, g1, h, g2, w_up, conv_w, conv_b, w_down, g3, tm, pad):
    b, lp, d = h.shape
    k = y.shape[2]
    ff2 = w_up.shape[1]
    assert lp % tm == 0 and tm % 16 == 0
    ws = _ffn_weight_specs(k, d, ff2)
    rows = lambda width: pl.BlockSpec((None, tm, width), lambda i, j: (i, j, 0))
    return pl.pallas_call(
        functools.partial(_ffn_body, tm=tm, pad=pad),
        grid=(b, lp // tm),
        in_specs=[rows(k), ws["w_out"], ws["gain"], rows(d), ws["gain"], ws["w_up"], ws["conv_w"], ws["conv_b"],
                  ws["w_down"], ws["gain"]],
        out_specs=rows(d),
        out_shape=jax.ShapeDtypeStruct((b, lp, d), F32),
        scratch_shapes=[pltpu.VMEM((8, ff2), F32)],
        compiler_params=pltpu.CompilerParams(
            dimension_semantics=("arbitrary", "arbitrary"), vmem_limit_bytes=VMEM_LIMIT),
        name="conv_ffn",
    )(y, w_out, g1.reshape(1, d), h, g2.reshape(1, d), w_up, conv_w, conv_b.reshape(1, ff2), w_down,
      g3.reshape(1, d))


def _out_ffn_tail(y, w_out, g1, h, g2, w_up, conv_w, conv_b, w_down, g3, tm, lead):
    b, lp, d = h.shape
    seq = lp - lead
    k = y.shape[2]
    ff2 = w_up.shape[1]
    halo = 8
    assert seq % tm == 0 and tm % 16 == 0 and lead >= halo and lead % 8 == 0
    ws = _ffn_weight_specs(k, d, ff2)
    rows = lambda width: pl.BlockSpec(
        (pl.Element(1), pl.Element(tm + halo), pl.Element(width)),
        lambda i, j: (i, pl.multiple_of(lead - halo + tm * j, 8), 0))
    return pl.pallas_call(
        functools.partial(_ffn_tail_body, halo=halo),
        grid=(b, seq // tm),
        in_specs=[rows(k), ws["w_out"], ws["gain"], rows(d), ws["gain"], ws["w_up"], ws["conv_w"], ws["conv_b"],
                  ws["w_down"], ws["gain"]],
        out_specs=pl.BlockSpec((None, tm, d), lambda i, j: (i, j, 0)),
        out_shape=jax.ShapeDtypeStruct((b, seq, d), F32),
        compiler_params=pltpu.CompilerParams(
            dimension_semantics=("arbitrary", "arbitrary"), vmem_limit_bytes=VMEM_LIMIT),
        name="conv_ffn_tail",
    )(y, w_out, g1.reshape(1, d), h, g2.reshape(1, d), w_up, conv_w, conv_b.reshape(1, ff2), w_down,
      g3.reshape(1, d))


def _hg_consts():
    c = CHUNK
    t = np.arange(c)
    m = np.zeros((2 + len(HG_LEVELS), c, c), np.float32)
    m[0] = t[None, :] <= t[:, None]
    m[1] = t[None, :] > t[:, None]
    masks = np.zeros((len(HG_LEVELS) + 1, c, c), np.float32)
    ii, jj = t[:, None], t[None, :]
    for li, s in enumerate(HG_LEVELS):
        for i in range(c):
            mid = (i // (2 * s)) * (2 * s) + s - 1
            if i % (2 * s) >= s:
                m[2 + li, i, mid + 1:i + 1] = 1.0
            else:
                m[2 + li, i, i + 1:mid + 1] = 1.0
        masks[li] = (ii // (2 * s) == jj // (2 * s)) & (ii % (2 * s) >= s) & (jj % (2 * s) < s)
    masks[-1] = ii == jj
    return m.reshape(-1, c), masks


def _even_body(p_ref, cos_ref, sin_ref, lbl_ref, hgain_ref, mall_ref, masks_ref, y_ref, sret_ref, shg_ref, *,
               layer):
    @pl.when(pl.program_id(1) == 0)
    def _():
        sret_ref[...] = jnp.zeros_like(sret_ref)
        shg_ref[...] = jnp.zeros_like(shg_ref)

    c = CHUNK
    cosv = cos_ref[...]
    sinv = sin_ref[...]
    ii = lax.broadcasted_iota(jnp.int32, (c, c), 0)
    jj = lax.broadcasted_iota(jnp.int32, (c, c), 1)
    diff = (ii - jj).astype(F32)
    rowi = lax.broadcasted_iota(jnp.int32, (c, RET_DK), 0).astype(F32)

    nb = p_ref.shape[0]
    col = lambda base, h: slice(base + h * 128, base + (h + 1) * 128)
    units = [(b, h) for b in range(nb) for h in range(RET_HEADS)]
    nu = len(units)
    lgs = [math.log1p(-(2.0 ** (-5.0 - h))) for _, h in units]
    qs = [p_ref[b, :, col(0, h)] for b, h in units]
    ks = [p_ref[b, :, col(RET_W, h)] for b, h in units]
    vs = [p_ref[b, :, col(2 * RET_W, h)] for b, h in units]
    qs = [q * cosv + pltpu.roll(q, RET_DK // 2, 1) * sinv for q in qs]
    ks = [(k * cosv + pltpu.roll(k, RET_DK // 2, 1) * sinv) * (RET_DK ** -0.5) for k in ks]
    ss = [sret_ref[b, h] for b, h in units]
    att = [_mm_nt(qs[u], ks[u]) * jnp.where(diff >= 0, jnp.exp(lgs[u] * jnp.maximum(diff, 0.0)), 0.0)
           for u in range(nu)]
    inter = [_mm(qs[u] * jnp.exp(lgs[u] * (rowi + 1.0)), ss[u]) for u in range(nu)]
    kv = [_mm_tn(ks[u] * jnp.exp(lgs[u] * (c - 1.0 - rowi)), vs[u]) for u in range(nu)]
    outs = [_mm(att[u], vs[u]) + inter[u] for u in range(nu)]
    for u, (b, h) in enumerate(units):
        sret_ref[b, h] = ss[u] * math.exp(lgs[u] * c) + kv[u]
        o = outs[u]
        xc = o - jnp.mean(o, axis=-1, keepdims=True)
        oa = xc * lax.rsqrt(jnp.mean(xc * xc, axis=-1, keepdims=True) + EPS)
        y_ref[b, :, col(0, h)] = (oa * _silu(p_ref[b, :, col(3 * RET_W, h)])).astype(y_ref.dtype)

    lgt = lbl_ref[...]
    mx = jnp.max(lgt, axis=0, keepdims=True)
    ex = jnp.exp(lgt - mx)
    sm = ex / jnp.sum(ex, axis=0, keepdims=True)
    lb_all = jnp.sum(sm[0:layer + 1, :], axis=0, keepdims=True)
    base = 4 * RET_W
    mall = mall_ref[...]
    nl = len(HG_LEVELS)
    units = [(b, h) for b in range(nb) for h in range(HG_HEADS)]
    nu = len(units)
    lbs = [lb_all[:, col(0, h)] for _, h in units]
    qs = [p_ref[b, :, col(base, h)] for b, h in units]
    fbs = [p_ref[b, :, col(base + HG_W, h)] for b, h in units]
    vs = [p_ref[b, :, col(base + 2 * HG_W, h)] for b, h in units]
    ks = [(1.0 - lbs[u]) * _sigmoid(-fbs[u]) for u in range(nu)]
    xs = [jnp.exp(_cmm2(mall, jnp.log(lbs[u] + (1.0 - lbs[u]) * _sigmoid(fbs[u])))) for u in range(nu)]
    sts = [shg_ref[b, h] for b, h in units]
    att = [masks_ref[nl] * _mm_nt(qs[u], ks[u]) for u in range(nu)]
    for li in range(nl):
        lv = slice((2 + li) * c, (3 + li) * c)
        att = [att[u] + masks_ref[li] * _mm_nt(qs[u] * xs[u][lv, :], ks[u] * xs[u][lv, :]) for u in range(nu)]
    inter = [_mm_nt(qs[u] * xs[u][0:c, :], sts[u]) for u in range(nu)]
    kv = [_mm_tn(vs[u], ks[u] * xs[u][c:2 * c, :]) for u in range(nu)]
    outs = [_mm(att[u], vs[u]) + inter[u] for u in range(nu)]
    for u, (b, h) in enumerate(units):
        shg_ref[b, h] = sts[u] * xs[u][c - 1:c, :] + kv[u]
        ob = _rms(outs[u], hgain_ref[:, col(0, h)])
        y_ref[b, :, col(RET_W, h)] = (ob * _silu(p_ref[b, :, col(base + 3 * HG_W, h)])).astype(y_ref.dtype)


def _even_mixer(p, cos_t, sin_t, lb_logits, hg_gain, layer):
    b, lp, n = p.shape
    nc = lp // CHUNK
    nb = EVEN_NB if b % EVEN_NB == 0 else 1
    mall, masks = _hg_consts()
    const2 = lambda i, j: (0, 0)
    return pl.pallas_call(
        functools.partial(_even_body, layer=layer),
        grid=(b // nb, nc),
        in_specs=[
            pl.BlockSpec((nb, CHUNK, n), lambda i, j: (i, j, 0)),
            pl.BlockSpec((CHUNK, RET_DK), lambda i, j: (j, 0)),
            pl.BlockSpec((CHUNK, RET_DK), lambda i, j: (j, 0)),
            pl.BlockSpec(lb_logits.shape, const2),
            pl.BlockSpec((1, HG_W), const2),
            pl.BlockSpec(mall.shape, const2),
            pl.BlockSpec(masks.shape, lambda i, j: (0, 0, 0)),
        ],
        out_specs=pl.BlockSpec((nb, CHUNK, RET_W + HG_W), lambda i, j: (i, j, 0)),
        out_shape=jax.ShapeDtypeStruct((b, lp, RET_W + HG_W), BF),
        scratch_shapes=[pltpu.VMEM((nb, RET_HEADS, RET_DK, RET_DK), F32),
                        pltpu.VMEM((nb, HG_HEADS, HG_DK, HG_DK), F32)],
        compiler_params=pltpu.CompilerParams(
            dimension_semantics=("arbitrary", "arbitrary"), vmem_limit_bytes=VMEM_LIMIT),
        name="even_mixer",
    )(p, cos_t, sin_t, lb_logits, hg_gain.reshape(1, HG_W), jnp.asarray(mall, BF), jnp.asarray(masks))


def _odd_body(p_ref, mu_ref, w0_ref, a0_ref, wa2_ref, g2_ref, kks_ref, kas_ref, rk_ref, lnw_ref, lnb_ref,
              cw_ref, alog_ref, dtb_ref, ggain_ref, tril_ref, bones_ref, y_ref,
              hrw_ref, sgd_ref, cpc_ref, ccv_ref):
    @pl.when(pl.program_id(1) == 0)
    def _():
        hrw_ref[...] = jnp.zeros_like(hrw_ref)
        sgd_ref[...] = jnp.zeros_like(sgd_ref)
        cpc_ref[...] = jnp.zeros_like(cpc_ref)
        ccv_ref[...] = jnp.zeros_like(ccv_ref)

    nb = p_ref.shape[0]
    c = CHUNK
    tril = tril_ref[...]
    bones = bones_ref[...]
    ii = lax.broadcasted_iota(jnp.int32, (c, c), 0)
    jj = lax.broadcasted_iota(jnp.int32, (c, c), 1)
    strict = ii > jj
    incl = ii >= jj
    eye = ii == jj
    eye_f = eye.astype(F32)
    ii2 = lax.broadcasted_iota(jnp.int32, (c, 2 * c), 0)
    lane2 = lax.broadcasted_iota(jnp.int32, (c, 2 * c), 1)
    jj2 = jnp.bitwise_and(lane2, c - 1)
    ak_mask = (ii2 > jj2) & (lane2 >= c)
    rbk_mask = ii2 >= jj2
    lane = lax.broadcasted_iota(jnp.int32, (c, 128), 1)
    heads = [slice(h * RW_HD, (h + 1) * RW_HD) for h in range(RW_HEADS)]
    base = RW_IN
    inv_hd = 1.0 / RW_HD

    def seg_sum(x):
        return jnp.concatenate(
            [_mm(x[:, p * 128:(p + 1) * 128], bones) for p in range(x.shape[1] // 128)], axis=1)

    def prep_pieces(b, d):
        def shift():
            pc = p_ref[b, :, 0:RW_IN]
            row = lax.broadcasted_iota(jnp.int32, (c, RW_IN), 0)
            prev = jnp.where(row == 0, cpc_ref[b, 7:8, :], pltpu.roll(pc, 1, 0))
            cpc_ref[b] = pc[c - 8:c, :]
            pcs = pc + (prev - pc) * mu_ref[...]
            d.update(r=pcs[:, 0:RW_W], k=pcs[:, RW_W:2 * RW_W], v=pcs[:, 2 * RW_W:3 * RW_W],
                     lo=pcs[:, 3 * RW_W:3 * RW_W + 128], glo=pcs[:, 3 * RW_W + 128:RW_IN])

        def lora():
            lo = d["lo"]
            wa = _mm(jnp.where(lane < RW_DECAY_LORA, jnp.tanh(lo), lo), wa2_ref[...])
            log_w = -_softplus(-(w0_ref[...] + wa[:, 0:RW_W])) - 0.5
            d["lw"] = -jnp.exp(log_w)
            d["a"] = _sigmoid(a0_ref[...] + wa[:, RW_W:2 * RW_W])
            d["gate"] = _mm(_sigmoid(d["glo"]), g2_ref[...])

        def keys():
            kks = d["k"] * kks_ref[...]
            kkn = kks * lax.rsqrt(seg_sum(kks * kks) + EPS)
            d["k2"] = d["k"] * (1.0 + (d["a"] - 1.0) * kas_ref[...])
            d["kkn"] = kkn
            d["beta"] = kkn * d["a"]
            d["c_inc"] = _cmm3(tril, d["lw"])

        def decays():
            c_inc, k2, beta = d["c_inc"], d["k2"], d["beta"]
            c_last = c_inc[c - 1:c, :]
            e_neg = jnp.exp(-c_inc)
            e_rest = jnp.exp(c_last - c_inc)
            d.update(ah=-d["kkn"] * jnp.exp(c_inc - d["lw"]), rh=d["r"] * jnp.exp(c_inc), bt=beta * e_neg,
                     kt=k2 * e_neg, bg=beta * e_rest, kg=k2 * e_rest, gc=jnp.exp(c_last),
                     ar=[], ls=[], ak=[], rbk=[], gcol=[], bgkg=[])

        def gram(h):
            def run():
                sl = heads[h]
                ar = jnp.concatenate([d["ah"][:, sl], d["rh"][:, sl]], axis=0).astype(BF)
                gm = _mm_nt(ar, jnp.concatenate([d["bt"][:, sl], d["kt"][:, sl]], axis=0))
                d["ar"].append(ar)
                d["ls"].append(jnp.where(strict, gm[0:c, 0:c], 0.0))
                d["ak"].append(jnp.where(ak_mask, gm[0:c, :], 0.0).astype(BF))
                d["rbk"].append(jnp.where(rbk_mask, gm[c:2 * c, :], 0.0).astype(BF))
                d["gcol"].append(jnp.sum(eye_f * d["gc"][:, sl], axis=1, keepdims=True))
                d["bgkg"].append(jnp.concatenate([d["bg"][:, sl], d["kg"][:, sl]], axis=0).astype(BF))
            return run

        def conv():
            x = p_ref[b, :, base:base + GDN_CONV_CH]
            rowc = lax.broadcasted_iota(jnp.int32, (c, GDN_CONV_CH), 0)
            xm1, xm2, xm3 = _shift_rows(x, ccv_ref.at[b], slice(None), GDN_CONV - 1, rowc)
            ccv_ref[b] = x[c - 8:c, :]
            d["qkv"] = _silu(xm3 * cw_ref[0:1, :] + xm2 * cw_ref[1:2, :] + xm1 * cw_ref[2:3, :] + x * cw_ref[3:4, :])

        def gates():
            sc = p_ref[b, :, base + GDN_CONV_CH + GDN_W:base + GDN_IN_PAD]
            g_all = -jnp.exp(alog_ref[...]) * _softplus(sc + dtb_ref[...])
            d["b_all"] = _sigmoid(sc)
            g_b = jnp.concatenate(
                [jnp.broadcast_to(g_all[:, h:h + 1], (c, 128)) for h in range(GDN_HEADS)], axis=1)
            d["cg_all"] = _cmm3(tril, g_b)
            d["gd"] = []

        def gdn_head(h):
            def run():
                hs = slice(h * 128, (h + 1) * 128)
                qkv = d["qkv"]
                q = qkv[:, hs]
                kd = qkv[:, GDN_W + h * 128:GDN_W + (h + 1) * 128]
                vd = qkv[:, 2 * GDN_W + h * 128:2 * GDN_W + (h + 1) * 128]
                q = q * lax.rsqrt(jnp.sum(q * q, axis=-1, keepdims=True) + EPS) * (GDN_DK ** -0.5)
                kd = kd * lax.rsqrt(jnp.sum(kd * kd, axis=-1, keepdims=True) + EPS)
                b_b = jnp.broadcast_to(d["b_all"][:, GDN_HEADS + h:GDN_HEADS + h + 1], (c, 128))
                cg = d["cg_all"][:, hs]
                cg_row = jnp.sum(jnp.where(eye, cg[:, 0:c], 0.0), axis=0, keepdims=True)
                decay = jnp.exp(jnp.where(incl, cg[:, 0:c] - cg_row, -jnp.inf))
                eg = jnp.exp(cg)
                cl = cg[c - 1:c, :]
                d["gd"].append(dict(
                    l=-jnp.where(strict, b_b[:, 0:c] * _mm_nt(kd, kd) * decay, 0.0),
                    rhs=jnp.concatenate([b_b * vd, b_b * kd * eg], axis=1).astype(BF),
                    qk=(_mm_nt(q, kd) * decay).astype(BF), q_in=(q * eg).astype(BF),
                    k_out=(kd * jnp.exp(cl - cg)).astype(BF), sd=jnp.exp(cl)))
            return run

        return ([shift, lora, keys, decays] + [gram(h) for h in range(RW_HEADS)]
                + [conv, gates] + [gdn_head(h) for h in range(GDN_HEADS)])

    def inverse_pieces(rows, ds, out):
        st = {}

        def start():
            ls = [l for b in rows for l in ds[b]["ls"]] + [g["l"] for b in rows for g in ds[b]["gd"]]
            st["p"] = [eye_f + l for l in ls]
            st["l"] = [l.astype(BF) for l in ls]

        def square():
            st["l"] = [_mm(l, l).astype(BF) for l in st["l"]]

        def extend():
            st["p"] = [p + _mm(l, p) for l, p in zip(st["l"], st["p"])]

        def finish():
            out["tinv"] = st["p"]

        levels = []
        span = 2
        while span < c:
            levels += [square, extend]
            span *= 2
        return [start] + levels + [finish]

    def state_pieces(rows, ds, inv):
        units = [(b, h) for b in rows for h in range(RW_HEADS)]
        gunits = [(b, h) for b in rows for h in range(GDN_HEADS)]
        st = {}

        def rw_read():
            st["h0"] = [hrw_ref[b, h] for b, h in units]
            st["hr"] = [_mm(ds[b]["ar"][h], st["h0"][u]) for u, (b, h) in enumerate(units)]
            st["vs"] = [ds[b]["v"][:, heads[h]].astype(BF) for b, h in units]

        def rw_mix():
            vs = st["vs"]
            st["x"] = [st["hr"][u][0:c, :] + _mm(ds[b]["ak"][h], jnp.concatenate([vs[u], vs[u]], axis=0))
                       for u, (b, h) in enumerate(units)]

        def rw_solve():
            us = [_mm(inv["tinv"][u], st["x"][u]) for u in range(len(units))]
            st["uv"] = [jnp.concatenate([us[u].astype(BF), st["vs"][u]], axis=0) for u in range(len(units))]

        def rw_out():
            st["ys"] = [st["hr"][u][c:2 * c, :] + _mm(ds[b]["rbk"][h], st["uv"][u])
                        for u, (b, h) in enumerate(units)]
            for u, (b, h) in enumerate(units):
                hrw_ref[b, h] = st["h0"][u] * ds[b]["gcol"][h] + _mm_tn(ds[b]["bgkg"][h], st["uv"][u])

        def rw_finish(k, b):
            def run():
                d = ds[b]
                y = jnp.concatenate(st["ys"][k * RW_HEADS:(k + 1) * RW_HEADS], axis=1)
                yc = y - seg_sum(y) * inv_hd
                yn = yc * lax.rsqrt(seg_sum(yc * yc) * inv_hd + RW_LNX_EPS) * lnw_ref[...] + lnb_ref[...]
                out = (yn + seg_sum(d["r"] * d["k2"] * rk_ref[...]) * d["v"]) * d["gate"]
                y_ref[b, :, 0:RW_W] = out.astype(y_ref.dtype)
            return run

        def gd_solve():
            n_rw = len(units)
            sols = [_mm(inv["tinv"][n_rw + u], ds[b]["gd"][h]["rhs"]) for u, (b, h) in enumerate(gunits)]
            st["s0"] = [sgd_ref[b, h] for b, h in gunits]
            st["v_new"] = [sols[u][:, 0:128] - _mm(sols[u][:, 128:256], st["s0"][u]) for u in range(len(gunits))]

        def gd_out(u, b, h):
            def run():
                g = ds[b]["gd"][h]
                s0, v_new = st["s0"][u], st["v_new"][u]
                o = _mm(g["q_in"], s0) + _mm(g["qk"], v_new)
                sgd_ref[b, h] = s0 * g["sd"] + _mm_tn(g["k_out"], v_new)
                og = p_ref[b, :, base + GDN_CONV_CH + h * 128:base + GDN_CONV_CH + (h + 1) * 128]
                out = _rms(o, ggain_ref[...]) * _silu(og)
                y_ref[b, :, RW_W + h * 128:RW_W + (h + 1) * 128] = out.astype(y_ref.dtype)
            return run

        return ([rw_read, rw_mix, rw_solve, rw_out] + [rw_finish(k, b) for k, b in enumerate(rows)]
                + [gd_solve] + [gd_out(u, b, h) for u, (b, h) in enumerate(gunits)])

    group = min(ODD_GROUP, nb)
    groups = [list(range(g, g + group)) for g in range(0, nb, group)]
    ds = [dict() for _ in range(nb)]
    invs = [dict() for _ in groups]
    stage1 = [_merge_pieces(*[prep_pieces(b, ds[b]) for b in rows]) for rows in groups]
    stage2 = [inverse_pieces(rows, ds, invs[g]) for g, rows in enumerate(groups)]
    stage3 = [state_pieces(rows, ds, invs[g]) for g, rows in enumerate(groups)]
    for slot in range(len(groups) + 2):
        active = []
        if slot < len(groups):
            active.append(stage1[slot])
        if 0 <= slot - 1 < len(groups):
            active.append(stage2[slot - 1])
        if 0 <= slot - 2 < len(groups):
            active.append(stage3[slot - 2])
        for piece in _merge_pieces(*active):
            piece()


def _odd_mixer(p, mu, w0, a0, wa2, g2, kks, kas, rk, lnw, lnb, conv_w, alog, dtb, ggain):
    b, lp, n = p.shape
    nc = lp // CHUNK
    nb = ODD_NB if b % ODD_NB == 0 else 1
    t = np.arange(CHUNK)
    tril = jnp.asarray(t[None, :] <= t[:, None], BF)
    l = np.arange(128)
    bones = jnp.asarray(l[:, None] // RW_HD == l[None, :] // RW_HD, BF)
    row = lambda a: a.reshape(1, -1)
    consts = [row(mu), row(w0), row(a0), wa2, g2, row(kks), row(kas), row(rk), row(lnw), row(lnb),
              conv_w, row(alog), row(dtb), row(ggain), tril, bones]
    const2 = lambda i, j: (0, 0)
    return pl.pallas_call(
        _odd_body,
        grid=(b // nb, nc),
        in_specs=[pl.BlockSpec((nb, CHUNK, n), lambda i, j: (i, j, 0))]
        + [pl.BlockSpec(a.shape, const2) for a in consts],
        out_specs=pl.BlockSpec((nb, CHUNK, RW_W + GDN_W), lambda i, j: (i, j, 0)),
        out_shape=jax.ShapeDtypeStruct((b, lp, RW_W + GDN_W), BF),
        scratch_shapes=[
            pltpu.VMEM((nb, RW_HEADS, RW_HD, RW_HD), F32),
            pltpu.VMEM((nb, GDN_HEADS, GDN_DK, GDN_DK), F32),
            pltpu.VMEM((nb, 8, RW_IN), F32),
            pltpu.VMEM((nb, 8, GDN_CONV_CH), F32),
        ],
        compiler_params=pltpu.CompilerParams(
            dimension_semantics=("arbitrary", "arbitrary"), vmem_limit_bytes=VMEM_LIMIT),
        name="odd_mixer",
    )(p, *consts)


def _row_tile(lp):
    best = None
    for tm in range(16, min(lp, 704) + 1, 16):
        if lp % tm == 0:
            best = tm
    assert best is not None, "sequence length must be a multiple of 16"
    return best


def kernel(x, meta_tokens, norm_gains, w_in_even, w_out_even, hg_lb_logits, hg_norm_gain, w_in_odd, w_out_odd, rw_mu, rw_w0, rw_w2, rw_a0, rw_a2, rw_g2, rw_kk_scale, rw_ka_scale, rw_rk, rw_lnx_w, rw_lnx_b, gdn_conv_w, gdn_a_log, gdn_dt_bias, gdn_norm_gain, ffn_w_up, ffn_conv_w, ffn_conv_b, ffn_w_down):
    bsz, seq, d = x.shape
    depth = norm_gains.shape[0]
    l = N_META + seq
    pad = (-l) % CHUNK
    lp = l + pad
    tm = _row_tile(lp)
    lead = pad + N_META

    half = RET_DK // 2
    pos = (jnp.arange(lp, dtype=jnp.int32) - pad).astype(F32)
    inv = ROPE_BASE ** (-jnp.arange(half, dtype=F32) / half)
    ang = pos[:, None] * inv[None, :]
    cos_t = jnp.concatenate([jnp.cos(ang), jnp.cos(ang)], axis=1)
    sin_t = jnp.concatenate([-jnp.sin(ang), jnp.sin(ang)], axis=1)

    h = None
    for layer in range(depth):
        g = norm_gains[layer]
        i = layer // 2
        if layer % 2 == 0:
            w_in = w_in_even[i].astype(BF)
        else:
            w_in = jnp.pad(w_in_odd[i], ((0, 0), (0, ODD_IN_PAD - w_in_odd.shape[2]))).astype(BF)
        if layer == 0:
            p, h = _embed_proj(x, meta_tokens.astype(x.dtype), g[0], w_in, tm, lead)
        else:
            p = _norm_proj(h, g[0], w_in, tm)
        if layer % 2 == 0:
            y = _even_mixer(p, cos_t, sin_t, hg_lb_logits, hg_norm_gain[i], layer)
            w_out = w_out_even[i]
        else:
            wa2 = jnp.zeros((RW_DECAY_LORA + RW_AAA_LORA, 2 * RW_W), F32)
            wa2 = wa2.at[:RW_DECAY_LORA, :RW_W].set(rw_w2[i]).at[RW_DECAY_LORA:, RW_W:].set(rw_a2[i])
            lane_pad = lambda a: jnp.pad(a, (0, LANES - a.shape[0]))
            y = _odd_mixer(p, rw_mu[i], rw_w0[i], rw_a0[i], wa2.astype(BF), rw_g2[i].astype(BF),
                           rw_kk_scale[i], rw_ka_scale[i], rw_rk[i].reshape(-1), rw_lnx_w[i], rw_lnx_b[i],
                           gdn_conv_w[i], lane_pad(gdn_a_log[i]), lane_pad(gdn_dt_bias[i]), gdn_norm_gain[i])
            w_out = w_out_odd[i]
        ffn_args = (y, w_out.astype(BF), g[1], h, g[2], ffn_w_up[layer].astype(BF), ffn_conv_w[layer],
                    ffn_conv_b[layer], ffn_w_down[layer].astype(BF), g[3])
        if layer + 1 < depth:
            h = _out_ffn(*ffn_args, tm, pad)
        else:
            h = _out_ffn_tail(*ffn_args, _row_tile(seq), lead)
    return h
```

```python
import functools
import math

import numpy as np
import jax
import jax.numpy as jnp
from jax import lax
from jax.experimental import pallas as pl
from jax.experimental.pallas import tpu as pltpu

F32 = jnp.float32
BF = jnp.bfloat16

D_MODEL = 1024
CHUNK = 64
N_META = 16
EPS = 1e-6
ROPE_BASE = 10000.0

RET_HEADS = D_MODEL // 256
RET_DK = 128
HG_HEADS = D_MODEL // 256
HG_DK = 128
RW_HEADS = D_MODEL // 128
RW_HD = 64
RW_DECAY_LORA = 64
RW_AAA_LORA = 64
RW_GATE_LORA = 128
RW_LNX_EPS = 64e-5
GDN_HEADS = D_MODEL // 256
GDN_DK = 128
GDN_CONV = 4
D_FF = 128 * ((8 * D_MODEL // 3 + 127) // 128)
FFN_CONV = 3

RET_W = RET_HEADS * RET_DK
HG_W = HG_HEADS * HG_DK
EVEN_IN = 4 * RET_W + 4 * HG_W
RW_W = RW_HEADS * RW_HD
RW_IN = 3 * RW_W + RW_DECAY_LORA + RW_AAA_LORA + RW_GATE_LORA
GDN_W = GDN_HEADS * GDN_DK
GDN_CONV_CH = 3 * GDN_W
GDN_IN = GDN_CONV_CH + GDN_W + 2 * GDN_HEADS
LANES = 128
GDN_IN_PAD = LANES * ((GDN_IN + LANES - 1) // LANES)
ODD_IN_PAD = RW_IN + GDN_IN_PAD
HG_LEVELS = (32, 16, 8, 4, 2, 1)
HG_ROW_SPLIT = 8
VMEM_LIMIT = 56 * 1024 * 1024
FFN_TILE = 256
EVEN_NB = 8
ODD_NB = 4
ODD_GROUP = 2


def _dot(a, b):
    return jnp.dot(a, b, preferred_element_type=F32)


def _mm(a, b):
    return _dot(a.astype(BF), b.astype(BF))


def _mm_nt(a, b):
    return lax.dot_general(a.astype(BF), b.astype(BF), (((1,), (1,)), ((), ())), preferred_element_type=F32)


def _mm_tn(a, b):
    return lax.dot_general(a.astype(BF), b.astype(BF), (((0,), (0,)), ((), ())), preferred_element_type=F32)


def _split2(x):
    hi = x.astype(BF)
    lo = (x - hi.astype(F32)).astype(BF)
    return hi, lo


def _split3(x):
    hi = x.astype(BF)
    r1 = x - hi.astype(F32)
    mid = r1.astype(BF)
    lo = (r1 - mid.astype(F32)).astype(BF)
    return hi, mid, lo


def _cmm3(c, x):
    hi, mid, lo = _split3(x)
    return _dot(c, hi) + _dot(c, mid) + _dot(c, lo)


def _cmm2(c, x):
    hi, lo = _split2(x)
    return _dot(c, hi) + _dot(c, lo)


def _sigmoid(x):
    return 1.0 / (1.0 + jnp.exp(-x))


def _silu(x):
    return x * _sigmoid(x)


def _softplus(x):
    return jnp.maximum(x, 0.0) + jnp.log1p(jnp.exp(-jnp.abs(x)))


def _rms(x, g):
    return x * lax.rsqrt(jnp.mean(x * x, axis=-1, keepdims=True) + EPS) * g


def _norm_proj_body(h_ref, g_ref, w_ref, o_ref):
    u = _rms(h_ref[...], g_ref[...])
    o_ref[...] = _dot(u.astype(BF), w_ref[...])


def _norm_proj(h, gain, w, tm):
    b, lp, d = h.shape
    n = w.shape[1]
    return pl.pallas_call(
        _norm_proj_body,
        grid=(b, lp // tm),
        in_specs=[
            pl.BlockSpec((None, tm, d), lambda i, j: (i, j, 0)),
            pl.BlockSpec((1, d), lambda i, j: (0, 0)),
            pl.BlockSpec((d, n), lambda i, j: (0, 0), pipeline_mode=pl.Buffered(1)),
        ],
        out_specs=pl.BlockSpec((None, tm, n), lambda i, j: (i, j, 0)),
        out_shape=jax.ShapeDtypeStruct((b, lp, n), F32),
        compiler_params=pltpu.CompilerParams(
            dimension_semantics=("arbitrary", "arbitrary"), vmem_limit_bytes=VMEM_LIMIT),
        name="norm_proj",
    )(h, gain.reshape(1, d), w)


def _embed_proj_body(x_ref, meta_ref, g_ref, w_ref, p_ref, h_ref, *, lead):
    xb = x_ref[0]
    tm, d = xb.shape
    n_zero = lead - meta_ref.shape[0]
    first = jnp.concatenate([jnp.zeros((n_zero, d), F32), meta_ref[...], xb[0:tm - lead, :]], axis=0)
    blk = jnp.where(pl.program_id(1) == 0, first, xb)
    h_ref[...] = blk
    p_ref[...] = _dot(_rms(blk, g_ref[...]).astype(BF), w_ref[...])


def _embed_proj(x, meta, gain, w, tm, lead):
    b, s, d = x.shape
    lp = s + lead
    n = w.shape[1]
    assert lead % 8 == 0 and meta.shape[0] % 8 == 0 and lead < tm and lp % tm == 0
    x_rows = lambda i, j: (i, pl.multiple_of(jnp.maximum(tm * j - lead, 0), 8), 0)
    return pl.pallas_call(
        functools.partial(_embed_proj_body, lead=lead),
        grid=(b, lp // tm),
        in_specs=[
            pl.BlockSpec((pl.Element(1), pl.Element(tm), pl.Element(d)), x_rows),
            pl.BlockSpec(meta.shape, lambda i, j: (0, 0)),
            pl.BlockSpec((1, d), lambda i, j: (0, 0)),
            pl.BlockSpec((d, n), lambda i, j: (0, 0), pipeline_mode=pl.Buffered(1)),
        ],
        out_specs=[pl.BlockSpec((None, tm, n), lambda i, j: (i, j, 0)),
                   pl.BlockSpec((None, tm, d), lambda i, j: (i, j, 0))],
        out_shape=[jax.ShapeDtypeStruct((b, lp, n), F32), jax.ShapeDtypeStruct((b, lp, d), F32)],
        compiler_params=pltpu.CompilerParams(
            dimension_semantics=("arbitrary", "arbitrary"), vmem_limit_bytes=VMEM_LIMIT),
        name="embed_proj",
    )(x, meta, gain.reshape(1, d), w)


def _shift_rows(a, carry_ref, cols, n_back, row):
    out = []
    for s in range(1, n_back + 1):
        sh = pltpu.roll(a, s, 0)
        for r in range(s):
            sh = jnp.where(row == r, carry_ref[8 - s + r:9 - s + r, cols], sh)
        out.append(sh)
    return out


def _merge_pieces(*lists):
    total = max(len(l) for l in lists)
    keyed = []
    for li, l in enumerate(lists):
        for k, piece in enumerate(l):
            keyed.append(((k + 0.5) * total / len(l), li, k, piece))
    keyed.sort(key=lambda t: t[:3])
    return [t[3] for t in keyed]


def _ffn_pieces(make_x, shift, g2_ref, wup_ref, cw_ref, cb_ref, wdn_ref, g3_ref, emit):
    n_tiles = D_FF // FFN_TILE
    st = {}

    def up(c):
        return [_dot(st["u"], wup_ref[:, part * D_FF + c * FFN_TILE:part * D_FF + (c + 1) * FFN_TILE])
                for part in range(2)]

    def head():
        st["x"] = make_x()
        st["u"] = _rms(st["x"], g2_ref[...]).astype(BF)
        st["acts"] = []

    def lead_in():
        st["pending"] = [up(c) for c in range(min(2, n_tiles))]

    def tile(c):
        def run():
            cur = st["pending"].pop(0)
            if c + 2 < n_tiles:
                st["pending"].append(up(c + 2))
            z = []
            for part in range(2):
                cols = slice(part * D_FF + c * FFN_TILE, part * D_FF + (c + 1) * FFN_TILE)
                a = cur[part]
                am1, am2 = shift(a, cols)
                z.append(am2 * cw_ref[0:1, cols] + am1 * cw_ref[1:2, cols] + a * cw_ref[2:3, cols] + cb_ref[:, cols])
            st["acts"].append((_silu(z[0]) * z[1]).astype(BF))
        return run

    def tail():
        acc = _dot(jnp.concatenate(st["acts"], axis=1), wdn_ref[...])
        emit(st["x"] + _rms(acc, g3_ref[...]))

    return [head, lead_in] + [tile(c) for c in range(n_tiles)] + [tail]


def _ffn_body(y_ref, wout_ref, g1_ref, h_ref, g2_ref, wup_ref, cw_ref, cb_ref, wdn_ref, g3_ref, o_ref, carry_ref,
              *, tm, pad):
    @pl.when(pl.program_id(1) == 0)
    def _():
        carry_ref[...] = jnp.zeros_like(carry_ref)

    half = tm // 2
    lists = []
    for r0 in (0, half):
        rows = slice(r0, r0 + half)
        t = pl.program_id(1) * tm + r0 + lax.broadcasted_iota(jnp.int32, (half, D_MODEL), 0)
        row = lax.broadcasted_iota(jnp.int32, (half, FFN_TILE), 0)

        def make_x(rows=rows, t=t):
            return jnp.where(t >= pad, h_ref[rows, :] + _rms(_dot(y_ref[rows, :], wout_ref[...]), g1_ref[...]), 0.0)

        def shift(a, cols, row=row):
            am1, am2 = _shift_rows(a, carry_ref, cols, FFN_CONV - 1, row)
            carry_ref[:, cols] = a[half - 8:half, :]
            return am1, am2

        def emit(out, rows=rows, t=t):
            o_ref[rows, :] = jnp.where(t >= pad, out, 0.0)

        lists.append(_ffn_pieces(make_x, shift, g2_ref, wup_ref, cw_ref, cb_ref, wdn_ref, g3_ref, emit))
    for piece in _merge_pieces(*lists):
        piece()


def _ffn_tail_body(y_ref, wout_ref, g1_ref, h_ref, g2_ref, wup_ref, cw_ref, cb_ref, wdn_ref, g3_ref, o_ref, *,
                   halo):
    half = o_ref.shape[0] // 2
    shift = lambda a, cols: (pltpu.roll(a, 1, 0), pltpu.roll(a, 2, 0))
    lists = []
    for r0 in (0, half):
        def make_x(r0=r0):
            rows = slice(r0, r0 + half + halo)
            return h_ref[0, rows, :] + _rms(_dot(y_ref[0, rows, :], wout_ref[...]), g1_ref[...])

        def emit(out, r0=r0):
            o_ref[r0:r0 + half, :] = out[halo:, :]

        lists.append(_ffn_pieces(make_x, shift, g2_ref, wup_ref, cw_ref, cb_ref, wdn_ref, g3_ref, emit))
    for piece in _merge_pieces(*lists):
        piece()


def _ffn_weight_specs(k, d, ff2):
    const = lambda i, j: (0, 0)
    return dict(
        w_out=pl.BlockSpec((k, d), const, pipeline_mode=pl.Buffered(1)), gain=pl.BlockSpec((1, d), const),
        w_up=pl.BlockSpec((d, ff2), const, pipeline_mode=pl.Buffered(1)),
        conv_w=pl.BlockSpec((FFN_CONV, ff2), const), conv_b=pl.BlockSpec((1, ff2), const),
        w_down=pl.BlockSpec((ff2 // 2, d), const, pipeline_mode=pl.Buffered(1)))


def _out_ffn(y, w_out, g1, h, g2, w_up, conv_w, conv_b, w_down, g3, tm, pad):
    b, lp, d = h.shape
    k = y.shape[2]
    ff2 = w_up.shape[1]
    assert lp % tm == 0 and tm % 16 == 0
    ws = _ffn_weight_specs(k, d, ff2)
    rows = lambda width: pl.BlockSpec((None, tm, width), lambda i, j: (i, j, 0))
    return pl.pallas_call(
        functools.partial(_ffn_body, tm=tm, pad=pad),
        grid=(b, lp // tm),
        in_specs=[rows(k), ws["w_out"], ws["gain"], rows(d), ws["gain"], ws["w_up"], ws["conv_w"], ws["conv_b"],
                  ws["w_down"], ws["gain"]],
        out_specs=rows(d),
        out_shape=jax.ShapeDtypeStruct((b, lp, d), F32),
        scratch_shapes=[pltpu.VMEM((8, ff2), F32)],
        compiler_params=pltpu.CompilerParams(
            dimension_semantics=("arbitrary", "arbitrary"), vmem_limit_bytes=VMEM_LIMIT),
        name="conv_ffn",
    )(y, w_out, g1.reshape(1, d), h, g2.reshape(1, d), w_up, conv_w, conv_b.reshape(1, ff2), w_down,
      g3.reshape(1, d))


def _out_ffn_tail(y, w_out, g1, h, g2, w_up, conv_w, conv_b, w_down, g3, tm, lead):
    b, lp, d = h.shape
    seq = lp - lead
    k = y.shape[2]
    ff2 = w_up.shape[1]
    halo = 8
    assert seq % tm == 0 and tm % 16 == 0 and lead >= halo and lead % 8 == 0
    ws = _ffn_weight_specs(k, d, ff2)
    rows = lambda width: pl.BlockSpec(
        (pl.Element(1), pl.Element(tm + halo), pl.Element(width)),
        lambda i, j: (i, pl.multiple_of(lead - halo + tm * j, 8), 0))
    return pl.pallas_call(
        functools.partial(_ffn_tail_body, halo=halo),
        grid=(b, seq // tm),
        in_specs=[rows(k), ws["w_out"], ws["gain"], rows(d), ws["gain"], ws["w_up"], ws["conv_w"], ws["conv_b"],
                  ws["w_down"], ws["gain"]],
        out_specs=pl.BlockSpec((None, tm, d), lambda i, j: (i, j, 0)),
        out_shape=jax.ShapeDtypeStruct((b, seq, d), F32),
        compiler_params=pltpu.CompilerParams(
            dimension_semantics=("arbitrary", "arbitrary"), vmem_limit_bytes=VMEM_LIMIT),
        name="conv_ffn_tail",
    )(y, w_out, g1.reshape(1, d), h, g2.reshape(1, d), w_up, conv_w, conv_b.reshape(1, ff2), w_down,
      g3.reshape(1, d))


def _hg_consts():
    c = CHUNK
    t = np.arange(c)
    small = [s for s in HG_LEVELS if s < HG_ROW_SPLIT]
    m = np.zeros((2 + len(small), c, c), np.float32)
    m[0] = t[None, :] <= t[:, None]
    m[1] = t[None, :] > t[:, None]
    masks = np.zeros((len(HG_LEVELS) + 1, c, c), np.float32)
    ii, jj = t[:, None], t[None, :]
    for li, s in enumerate(HG_LEVELS):
        if s < HG_ROW_SPLIT:
            for i in range(c):
                mid = (i // (2 * s)) * (2 * s) + s - 1
                if i % (2 * s) >= s:
                    m[2 + small.index(s), i, mid + 1:i + 1] = 1.0
                else:
                    m[2 + small.index(s), i, i + 1:mid + 1] = 1.0
        masks[li] = (ii // (2 * s) == jj // (2 * s)) & (ii % (2 * s) >= s) & (jj % (2 * s) < s)
    masks[-1] = ii == jj
    return m.reshape(-1, c), masks


def _even_body(p_ref, cos_ref, sin_ref, lbl_ref, hgain_ref, mall_ref, masks_ref, y_ref, sret_ref, shg_ref, *,
               layer):
    @pl.when(pl.program_id(1) == 0)
    def _():
        sret_ref[...] = jnp.zeros_like(sret_ref)
        shg_ref[...] = jnp.zeros_like(shg_ref)

    c = CHUNK
    cosv = cos_ref[...]
    sinv = sin_ref[...]
    ii = lax.broadcasted_iota(jnp.int32, (c, c), 0)
    jj = lax.broadcasted_iota(jnp.int32, (c, c), 1)
    diff = (ii - jj).astype(F32)
    rowi = lax.broadcasted_iota(jnp.int32, (c, RET_DK), 0).astype(F32)

    nb = p_ref.shape[0]
    col = lambda base, h: slice(base + h * 128, base + (h + 1) * 128)
    units = [(b, h) for b in range(nb) for h in range(RET_HEADS)]
    nu = len(units)
    lgs = [math.log1p(-(2.0 ** (-5.0 - h))) for _, h in units]
    qs = [p_ref[b, :, col(0, h)] for b, h in units]
    ks = [p_ref[b, :, col(RET_W, h)] for b, h in units]
    vs = [p_ref[b, :, col(2 * RET_W, h)] for b, h in units]
    qs = [q * cosv + pltpu.roll(q, RET_DK // 2, 1) * sinv for q in qs]
    ks = [(k * cosv + pltpu.roll(k, RET_DK // 2, 1) * sinv) * (RET_DK ** -0.5) for k in ks]
    ss = [sret_ref[b, h] for b, h in units]
    att = [_mm_nt(qs[u], ks[u]) * jnp.where(diff >= 0, jnp.exp(lgs[u] * jnp.maximum(diff, 0.0)), 0.0)
           for u in range(nu)]
    inter = [_mm(qs[u] * jnp.exp(lgs[u] * (rowi + 1.0)), ss[u]) for u in range(nu)]
    kv = [_mm_tn(ks[u] * jnp.exp(lgs[u] * (c - 1.0 - rowi)), vs[u]) for u in range(nu)]
    outs = [_mm(att[u], vs[u]) + inter[u] for u in range(nu)]
    for u, (b, h) in enumerate(units):
        sret_ref[b, h] = ss[u] * math.exp(lgs[u] * c) + kv[u]
        o = outs[u]
        xc = o - jnp.mean(o, axis=-1, keepdims=True)
        oa = xc * lax.rsqrt(jnp.mean(xc * xc, axis=-1, keepdims=True) + EPS)
        y_ref[b, :, col(0, h)] = (oa * _silu(p_ref[b, :, col(3 * RET_W, h)])).astype(y_ref.dtype)

    lgt = lbl_ref[...]
    mx = jnp.max(lgt, axis=0, keepdims=True)
    ex = jnp.exp(lgt - mx)
    sm = ex / jnp.sum(ex, axis=0, keepdims=True)
    lb_all = jnp.sum(sm[0:layer + 1, :], axis=0, keepdims=True)
    base = 4 * RET_W
    mall = mall_ref[...]
    nl = len(HG_LEVELS)
    units = [(b, h) for b in range(nb) for h in range(HG_HEADS)]
    nu = len(units)
    lbs = [lb_all[:, col(0, h)] for _, h in units]
    qs = [p_ref[b, :, col(base, h)] for b, h in units]
    fbs = [p_ref[b, :, col(base + HG_W, h)] for b, h in units]
    vs = [p_ref[b, :, col(base + 2 * HG_W, h)] for b, h in units]
    ks = [(1.0 - lbs[u]) * _sigmoid(-fbs[u]) for u in range(nu)]
    es = [_cmm2(mall, jnp.log(lbs[u] + (1.0 - lbs[u]) * _sigmoid(fbs[u]))) for u in range(nu)]
    xs = [jnp.exp(e) for e in es]

    def level_factor(u, li):
        s = HG_LEVELS[li]
        if s < HG_ROW_SPLIT:
            k = 2 + [t for t in HG_LEVELS if t < HG_ROW_SPLIT].index(s)
            return xs[u][k * c:(k + 1) * c, :]
        cb = es[u][0:c, :]
        parts = []
        for p0 in range(0, c, 2 * s):
            mid = cb[p0 + s - 1:p0 + s, :]
            parts += [mid - cb[p0:p0 + s, :], cb[p0 + s:p0 + 2 * s, :] - mid]
        return jnp.exp(jnp.concatenate(parts, axis=0))

    sts = [shg_ref[b, h] for b, h in units]
    att = [masks_ref[nl] * _mm_nt(qs[u], ks[u]) for u in range(nu)]
    for li in range(nl):
        fac = [level_factor(u, li) for u in range(nu)]
        att = [att[u] + masks_ref[li] * _mm_nt(qs[u] * fac[u], ks[u] * fac[u]) for u in range(nu)]
    inter = [_mm_nt(qs[u] * xs[u][0:c, :], sts[u]) for u in range(nu)]
    kv = [_mm_tn(vs[u], ks[u] * xs[u][c:2 * c, :]) for u in range(nu)]
    outs = [_mm(att[u], vs[u]) + inter[u] for u in range(nu)]
    for u, (b, h) in enumerate(units):
        shg_ref[b, h] = sts[u] * xs[u][c - 1:c, :] + kv[u]
        ob = _rms(outs[u], hgain_ref[:, col(0, h)])
        y_ref[b, :, col(RET_W, h)] = (ob * _silu(p_ref[b, :, col(base + 3 * HG_W, h)])).astype(y_ref.dtype)


def _even_mixer(p, cos_t, sin_t, lb_logits, hg_gain, layer):
    b, lp, n = p.shape
    nc = lp // CHUNK
    nb = EVEN_NB if b % EVEN_NB == 0 else 1
    mall, masks = _hg_consts()
    const2 = lambda i, j: (0, 0)
    return pl.pallas_call(
        functools.partial(_even_body, layer=layer),
        grid=(b // nb, nc),
        in_specs=[
            pl.BlockSpec((nb, CHUNK, n), lambda i, j: (i, j, 0)),
            pl.BlockSpec((CHUNK, RET_DK), lambda i, j: (j, 0)),
            pl.BlockSpec((CHUNK, RET_DK), lambda i, j: (j, 0)),
            pl.BlockSpec(lb_logits.shape, const2),
            pl.BlockSpec((1, HG_W), const2),
            pl.BlockSpec(mall.shape, const2),
            pl.BlockSpec(masks.shape, lambda i, j: (0, 0, 0)),
        ],
        out_specs=pl.BlockSpec((nb, CHUNK, RET_W + HG_W), lambda i, j: (i, j, 0)),
        out_shape=jax.ShapeDtypeStruct((b, lp, RET_W + HG_W), BF),
        scratch_shapes=[pltpu.VMEM((nb, RET_HEADS, RET_DK, RET_DK), F32),
                        pltpu.VMEM((nb, HG_HEADS, HG_DK, HG_DK), F32)],
        compiler_params=pltpu.CompilerParams(
            dimension_semantics=("arbitrary", "arbitrary"), vmem_limit_bytes=VMEM_LIMIT),
        name="even_mixer",
    )(p, cos_t, sin_t, lb_logits, hg_gain.reshape(1, HG_W), jnp.asarray(mall, BF), jnp.asarray(masks))


def _odd_body(p_ref, mu_ref, w0_ref, a0_ref, wa2_ref, g2_ref, kks_ref, kas_ref, rk_ref, lnw_ref, lnb_ref,
              cw_ref, alog_ref, dtb_ref, ggain_ref, tril_ref, bones_ref, y_ref,
              hrw_ref, sgd_ref, cpc_ref, ccv_ref):
    @pl.when(pl.program_id(1) == 0)
    def _():
        hrw_ref[...] = jnp.zeros_like(hrw_ref)
        sgd_ref[...] = jnp.zeros_like(sgd_ref)
        cpc_ref[...] = jnp.zeros_like(cpc_ref)
        ccv_ref[...] = jnp.zeros_like(ccv_ref)

    nb = p_ref.shape[0]
    c = CHUNK
    tril = tril_ref[...]
    bones = bones_ref[...]
    ii = lax.broadcasted_iota(jnp.int32, (c, c), 0)
    jj = lax.broadcasted_iota(jnp.int32, (c, c), 1)
    strict = ii > jj
    incl = ii >= jj
    eye = ii == jj
    eye_f = eye.astype(F32)
    ii2 = lax.broadcasted_iota(jnp.int32, (c, 2 * c), 0)
    lane2 = lax.broadcasted_iota(jnp.int32, (c, 2 * c), 1)
    jj2 = jnp.bitwise_and(lane2, c - 1)
    ak_mask = (ii2 > jj2) & (lane2 >= c)
    rbk_mask = ii2 >= jj2
    lane = lax.broadcasted_iota(jnp.int32, (c, 128), 1)
    heads = [slice(h * RW_HD, (h + 1) * RW_HD) for h in range(RW_HEADS)]
    base = RW_IN
    inv_hd = 1.0 / RW_HD

    def seg_sum(x):
        return jnp.concatenate(
            [_mm(x[:, p * 128:(p + 1) * 128], bones) for p in range(x.shape[1] // 128)], axis=1)

    def prep_pieces(b, d):
        def shift():
            pc = p_ref[b, :, 0:RW_IN]
            row = lax.broadcasted_iota(jnp.int32, (c, RW_IN), 0)
            prev = jnp.where(row == 0, cpc_ref[b, 7:8, :], pltpu.roll(pc, 1, 0))
            cpc_ref[b] = pc[c - 8:c, :]
            pcs = pc + (prev - pc) * mu_ref[...]
            d.update(r=pcs[:, 0:RW_W], k=pcs[:, RW_W:2 * RW_W], v=pcs[:, 2 * RW_W:3 * RW_W],
                     lo=pcs[:, 3 * RW_W:3 * RW_W + 128], glo=pcs[:, 3 * RW_W + 128:RW_IN])

        def lora():
            lo = d["lo"]
            wa = _mm(jnp.where(lane < RW_DECAY_LORA, jnp.tanh(lo), lo), wa2_ref[...])
            log_w = -_softplus(-(w0_ref[...] + wa[:, 0:RW_W])) - 0.5
            d["lw"] = -jnp.exp(log_w)
            d["a"] = _sigmoid(a0_ref[...] + wa[:, RW_W:2 * RW_W])
            d["gate"] = _mm(_sigmoid(d["glo"]), g2_ref[...])

        def keys():
            kks = d["k"] * kks_ref[...]
            kkn = kks * lax.rsqrt(seg_sum(kks * kks) + EPS)
            d["k2"] = d["k"] * (1.0 + (d["a"] - 1.0) * kas_ref[...])
            d["kkn"] = kkn
            d["beta"] = kkn * d["a"]
            d["c_inc"] = _cmm3(tril, d["lw"])

        def decays():
            c_inc, k2, beta = d["c_inc"], d["k2"], d["beta"]
            c_last = c_inc[c - 1:c, :]
            e_neg = jnp.exp(-c_inc)
            e_rest = jnp.exp(c_last - c_inc)
            d.update(ah=-d["kkn"] * jnp.exp(c_inc - d["lw"]), rh=d["r"] * jnp.exp(c_inc), bt=beta * e_neg,
                     kt=k2 * e_neg, bg=beta * e_rest, kg=k2 * e_rest, gc=jnp.exp(c_last),
                     ar=[], ls=[], ak=[], rbk=[], gcol=[], bgkg=[])

        def gram(h):
            def run():
                sl = heads[h]
                ar = jnp.concatenate([d["ah"][:, sl], d["rh"][:, sl]], axis=0).astype(BF)
                gm = _mm_nt(ar, jnp.concatenate([d["bt"][:, sl], d["kt"][:, sl]], axis=0))
                d["ar"].append(ar)
                d["ls"].append(jnp.where(strict, gm[0:c, 0:c], 0.0))
                d["ak"].append(jnp.where(ak_mask, gm[0:c, :], 0.0).astype(BF))
                d["rbk"].append(jnp.where(rbk_mask, gm[c:2 * c, :], 0.0).astype(BF))
                d["gcol"].append(jnp.sum(eye_f * d["gc"][:, sl], axis=1, keepdims=True))
                d["bgkg"].append(jnp.concatenate([d["bg"][:, sl], d["kg"][:, sl]], axis=0).astype(BF))
            return run

        def conv():
            x = p_ref[b, :, base:base + GDN_CONV_CH]
            rowc = lax.broadcasted_iota(jnp.int32, (c, GDN_CONV_CH), 0)
            xm1, xm2, xm3 = _shift_rows(x, ccv_ref.at[b], slice(None), GDN_CONV - 1, rowc)
            ccv_ref[b] = x[c - 8:c, :]
            d["qkv"] = _silu(xm3 * cw_ref[0:1, :] + xm2 * cw_ref[1:2, :] + xm1 * cw_ref[2:3, :] + x * cw_ref[3:4, :])

        def gates():
            sc = p_ref[b, :, base + GDN_CONV_CH + GDN_W:base + GDN_IN_PAD]
            g_all = -jnp.exp(alog_ref[...]) * _softplus(sc + dtb_ref[...])
            d["b_all"] = _sigmoid(sc)
            g_b = jnp.concatenate(
                [jnp.broadcast_to(g_all[:, h:h + 1], (c, 128)) for h in range(GDN_HEADS)], axis=1)
            d["cg_all"] = _cmm3(tril, g_b)
            d["gd"] = []

        def gdn_head(h):
            def run():
                hs = slice(h * 128, (h + 1) * 128)
                qkv = d["qkv"]
                q = qkv[:, hs]
                kd = qkv[:, GDN_W + h * 128:GDN_W + (h + 1) * 128]
                vd = qkv[:, 2 * GDN_W + h * 128:2 * GDN_W + (h + 1) * 128]
                q = q * lax.rsqrt(jnp.sum(q * q, axis=-1, keepdims=True) + EPS) * (GDN_DK ** -0.5)
                kd = kd * lax.rsqrt(jnp.sum(kd * kd, axis=-1, keepdims=True) + EPS)
                b_b = jnp.broadcast_to(d["b_all"][:, GDN_HEADS + h:GDN_HEADS + h + 1], (c, 128))
                cg = d["cg_all"][:, hs]
                cg_row = jnp.sum(jnp.where(eye, cg[:, 0:c], 0.0), axis=0, keepdims=True)
                decay = jnp.exp(jnp.where(incl, cg[:, 0:c] - cg_row, -jnp.inf))
                eg = jnp.exp(cg)
                cl = cg[c - 1:c, :]
                d["gd"].append(dict(
                    l=-jnp.where(strict, b_b[:, 0:c] * _mm_nt(kd, kd) * decay, 0.0),
                    rhs=jnp.concatenate([b_b * vd, b_b * kd * eg], axis=1).astype(BF),
                    qk=(_mm_nt(q, kd) * decay).astype(BF), q_in=(q * eg).astype(BF),
                    k_out=(kd * jnp.exp(cl - cg)).astype(BF), sd=jnp.exp(cl)))
            return run

        return ([shift, lora, keys, decays] + [gram(h) for h in range(RW_HEADS)]
                + [conv, gates] + [gdn_head(h) for h in range(GDN_HEADS)])

    def inverse_pieces(rows, ds, out):
        st = {}

        def start():
            ls = [l for b in rows for l in ds[b]["ls"]] + [g["l"] for b in rows for g in ds[b]["gd"]]
            st["p"] = [eye_f + l for l in ls]
            st["l"] = [l.astype(BF) for l in ls]

        def square():
            st["l"] = [_mm(l, l).astype(BF) for l in st["l"]]

        def extend():
            st["p"] = [p + _mm(l, p) for l, p in zip(st["l"], st["p"])]

        def finish():
            out["tinv"] = st["p"]

        levels = []
        span = 2
        while span < c:
            levels += [square, extend]
            span *= 2
        return [start] + levels + [finish]

    def state_pieces(rows, ds, inv):
        units = [(b, h) for b in rows for h in range(RW_HEADS)]
        gunits = [(b, h) for b in rows for h in range(GDN_HEADS)]
        st = {}

        def rw_read():
            st["h0"] = [hrw_ref[b, h] for b, h in units]
            st["hr"] = [_mm(ds[b]["ar"][h], st["h0"][u]) for u, (b, h) in enumerate(units)]
            st["vs"] = [ds[b]["v"][:, heads[h]].astype(BF) for b, h in units]

        def rw_mix():
            vs = st["vs"]
            st["x"] = [st["hr"][u][0:c, :] + _mm(ds[b]["ak"][h], jnp.concatenate([vs[u], vs[u]], axis=0))
                       for u, (b, h) in enumerate(units)]

        def rw_solve():
            us = [_mm(inv["tinv"][u], st["x"][u]) for u in range(len(units))]
            st["uv"] = [jnp.concatenate([us[u].astype(BF), st["vs"][u]], axis=0) for u in range(len(units))]

        def rw_out():
            st["ys"] = [st["hr"][u][c:2 * c, :] + _mm(ds[b]["rbk"][h], st["uv"][u])
                        for u, (b, h) in enumerate(units)]
            for u, (b, h) in enumerate(units):
                hrw_ref[b, h] = st["h0"][u] * ds[b]["gcol"][h] + _mm_tn(ds[b]["bgkg"][h], st["uv"][u])

        def rw_finish(k, b):
            def run():
                d = ds[b]
                y = jnp.concatenate(st["ys"][k * RW_HEADS:(k + 1) * RW_HEADS], axis=1)
                yc = y - seg_sum(y) * inv_hd
                yn = yc * lax.rsqrt(seg_sum(yc * yc) * inv_hd + RW_LNX_EPS) * lnw_ref[...] + lnb_ref[...]
                out = (yn + seg_sum(d["r"] * d["k2"] * rk_ref[...]) * d["v"]) * d["gate"]
                y_ref[b, :, 0:RW_W] = out.astype(y_ref.dtype)
            return run

        def gd_solve():
            n_rw = len(units)
            sols = [_mm(inv["tinv"][n_rw + u], ds[b]["gd"][h]["rhs"]) for u, (b, h) in enumerate(gunits)]
            st["s0"] = [sgd_ref[b, h] for b, h in gunits]
            st["v_new"] = [sols[u][:, 0:128] - _mm(sols[u][:, 128:256], st["s0"][u]) for u in range(len(gunits))]

        def gd_out(u, b, h):
            def run():
                g = ds[b]["gd"][h]
                s0, v_new = st["s0"][u], st["v_new"][u]
                o = _mm(g["q_in"], s0) + _mm(g["qk"], v_new)
                sgd_ref[b, h] = s0 * g["sd"] + _mm_tn(g["k_out"], v_new)
                og = p_ref[b, :, base + GDN_CONV_CH + h * 128:base + GDN_CONV_CH + (h + 1) * 128]
                out = _rms(o, ggain_ref[...]) * _silu(og)
                y_ref[b, :, RW_W + h * 128:RW_W + (h + 1) * 128] = out.astype(y_ref.dtype)
            return run

        return ([rw_read, rw_mix, rw_solve, rw_out] + [rw_finish(k, b) for k, b in enumerate(rows)]
                + [gd_solve] + [gd_out(u, b, h) for u, (b, h) in enumerate(gunits)])

    group = min(ODD_GROUP, nb)
    groups = [list(range(g, g + group)) for g in range(0, nb, group)]
    ds = [dict() for _ in range(nb)]
    invs = [dict() for _ in groups]
    stage1 = [_merge_pieces(*[prep_pieces(b, ds[b]) for b in rows]) for rows in groups]
    stage2 = [inverse_pieces(rows, ds, invs[g]) for g, rows in enumerate(groups)]
    stage3 = [state_pieces(rows, ds, invs[g]) for g, rows in enumerate(groups)]
    for slot in range(len(groups) + 2):
        active = []
        if slot < len(groups):
            active.append(stage1[slot])
        if 0 <= slot - 1 < len(groups):
            active.append(stage2[slot - 1])
        if 0 <= slot - 2 < len(groups):
            active.append(stage3[slot - 2])
        for piece in _merge_pieces(*active):
            piece()


def _odd_mixer(p, mu, w0, a0, wa2, g2, kks, kas, rk, lnw, lnb, conv_w, alog, dtb, ggain):
    b, lp, n = p.shape
    nc = lp // CHUNK
    nb = ODD_NB if b % ODD_NB == 0 else 1
    t = np.arange(CHUNK)
    tril = jnp.asarray(t[None, :] <= t[:, None], BF)
    l = np.arange(128)
    bones = jnp.asarray(l[:, None] // RW_HD == l[None, :] // RW_HD, BF)
    row = lambda a: a.reshape(1, -1)
    consts = [row(mu), row(w0), row(a0), wa2, g2, row(kks), row(kas), row(rk), row(lnw), row(lnb),
              conv_w, row(alog), row(dtb), row(ggain), tril, bones]
    const2 = lambda i, j: (0, 0)
    return pl.pallas_call(
        _odd_body,
        grid=(b // nb, nc),
        in_specs=[pl.BlockSpec((nb, CHUNK, n), lambda i, j: (i, j, 0))]
        + [pl.BlockSpec(a.shape, const2) for a in consts],
        out_specs=pl.BlockSpec((nb, CHUNK, RW_W + GDN_W), lambda i, j: (i, j, 0)),
        out_shape=jax.ShapeDtypeStruct((b, lp, RW_W + GDN_W), BF),
        scratch_shapes=[
            pltpu.VMEM((nb, RW_HEADS, RW_HD, RW_HD), F32),
            pltpu.VMEM((nb, GDN_HEADS, GDN_DK, GDN_DK), F32),
            pltpu.VMEM((nb, 8, RW_IN), F32),
            pltpu.VMEM((nb, 8, GDN_CONV_CH), F32),
        ],
        compiler_params=pltpu.CompilerParams(
            dimension_semantics=("arbitrary", "arbitrary"), vmem_limit_bytes=VMEM_LIMIT),
        name="odd_mixer",
    )(p, *consts)


def _row_tile(lp):
    best = None
    for tm in range(16, min(lp, 704) + 1, 16):
        if lp % tm == 0:
            best = tm
    assert best is not None, "sequence length must be a multiple of 16"
    return best


def kernel(x, meta_tokens, norm_gains, w_in_even, w_out_even, hg_lb_logits, hg_norm_gain, w_in_odd, w_out_odd, rw_mu, rw_w0, rw_w2, rw_a0, rw_a2, rw_g2, rw_kk_scale, rw_ka_scale, rw_rk, rw_lnx_w, rw_lnx_b, gdn_conv_w, gdn_a_log, gdn_dt_bias, gdn_norm_gain, ffn_w_up, ffn_conv_w, ffn_conv_b, ffn_w_down):
    bsz, seq, d = x.shape
    depth = norm_gains.shape[0]
    l = N_META + seq
    pad = (-l) % CHUNK
    lp = l + pad
    tm = _row_tile(lp)
    lead = pad + N_META

    half = RET_DK // 2
    pos = (jnp.arange(lp, dtype=jnp.int32) - pad).astype(F32)
    inv = ROPE_BASE ** (-jnp.arange(half, dtype=F32) / half)
    ang = pos[:, None] * inv[None, :]
    cos_t = jnp.concatenate([jnp.cos(ang), jnp.cos(ang)], axis=1)
    sin_t = jnp.concatenate([-jnp.sin(ang), jnp.sin(ang)], axis=1)

    h = None
    for layer in range(depth):
        g = norm_gains[layer]
        i = layer // 2
        if layer % 2 == 0:
            w_in = w_in_even[i].astype(BF)
        else:
            w_in = jnp.pad(w_in_odd[i], ((0, 0), (0, ODD_IN_PAD - w_in_odd.shape[2]))).astype(BF)
        if layer == 0:
            p, h = _embed_proj(x, meta_tokens.astype(x.dtype), g[0], w_in, tm, lead)
        else:
            p = _norm_proj(h, g[0], w_in, tm)
        if layer % 2 == 0:
            y = _even_mixer(p, cos_t, sin_t, hg_lb_logits, hg_norm_gain[i], layer)
            w_out = w_out_even[i]
        else:
            wa2 = jnp.zeros((RW_DECAY_LORA + RW_AAA_LORA, 2 * RW_W), F32)
            wa2 = wa2.at[:RW_DECAY_LORA, :RW_W].set(rw_w2[i]).at[RW_DECAY_LORA:, RW_W:].set(rw_a2[i])
            lane_pad = lambda a: jnp.pad(a, (0, LANES - a.shape[0]))
            y = _odd_mixer(p, rw_mu[i], rw_w0[i], rw_a0[i], wa2.astype(BF), rw_g2[i].astype(BF),
                           rw_kk_scale[i], rw_ka_scale[i], rw_rk[i].reshape(-1), rw_lnx_w[i], rw_lnx_b[i],
                           gdn_conv_w[i], lane_pad(gdn_a_log[i]), lane_pad(gdn_dt_bias[i]), gdn_norm_gain[i])
            w_out = w_out_odd[i]
        ffn_args = (y, w_out.astype(BF), g[1], h, g[2], ffn_w_up[layer].astype(BF), ffn_conv_w[layer],
                    ffn_conv_b[layer], ffn_w_down[layer].astype(BF), g[3])
        if layer + 1 < depth:
            h = _out_ffn(*ffn_args, tm, pad)
        else:
            h = _out_ffn_tail(*ffn_args, _row_tile(seq), lead)
    return h
```

```python
import functools
import math

import numpy as np
import jax
import jax.numpy as jnp
from jax import lax
from jax.experimental import pallas as pl
from jax.experimental.pallas import tpu as pltpu

F32 = jnp.float32
BF = jnp.bfloat16

D_MODEL = 1024
CHUNK = 64
N_META = 16
EPS = 1e-6
ROPE_BASE = 10000.0

RET_HEADS = D_MODEL // 256
RET_DK = 128
HG_HEADS = D_MODEL // 256
HG_DK = 128
RW_HEADS = D_MODEL // 128
RW_HD = 64
RW_DECAY_LORA = 64
RW_AAA_LORA = 64
RW_GATE_LORA = 128
RW_LNX_EPS = 64e-5
GDN_HEADS = D_MODEL // 256
GDN_DK = 128
GDN_CONV = 4
D_FF = 128 * ((8 * D_MODEL // 3 + 127) // 128)
FFN_CONV = 3

RET_W = RET_HEADS * RET_DK
HG_W = HG_HEADS * HG_DK
RW_W = RW_HEADS * RW_HD
RW_IN = 3 * RW_W + RW_DECAY_LORA + RW_AAA_LORA + RW_GATE_LORA
GDN_W = GDN_HEADS * GDN_DK
GDN_CONV_CH = 3 * GDN_W
GDN_IN = GDN_CONV_CH + GDN_W + 2 * GDN_HEADS
LANES = 128
GDN_IN_PAD = LANES * ((GDN_IN + LANES - 1) // LANES)
ODD_IN_PAD = RW_IN + GDN_IN_PAD
HG_LEVELS = (32, 16, 8, 4, 2, 1)
HG_ROW_SPLIT = 8
VMEM_LIMIT = 56 * 1024 * 1024
FFN_TILE = 256
EVEN_NB = 8
ODD_NB = 4
ODD_GROUP = 2


def _dot(a, b):
    return jnp.dot(a, b, preferred_element_type=F32)


def _mm(a, b):
    return _dot(a.astype(BF), b.astype(BF))


def _mm_nt(a, b):
    return lax.dot_general(a.astype(BF), b.astype(BF), (((1,), (1,)), ((), ())), preferred_element_type=F32)


def _mm_tn(a, b):
    return lax.dot_general(a.astype(BF), b.astype(BF), (((0,), (0,)), ((), ())), preferred_element_type=F32)


def _split2(x):
    hi = x.astype(BF)
    lo = (x - hi.astype(F32)).astype(BF)
    return hi, lo


def _split3(x):
    hi = x.astype(BF)
    r1 = x - hi.astype(F32)
    mid = r1.astype(BF)
    lo = (r1 - mid.astype(F32)).astype(BF)
    return hi, mid, lo


def _cmm3(c, x):
    hi, mid, lo = _split3(x)
    return _dot(c, hi) + _dot(c, mid) + _dot(c, lo)


def _cmm2(c, x):
    hi, lo = _split2(x)
    return _dot(c, hi) + _dot(c, lo)


def _sigmoid(x):
    return 1.0 / (1.0 + jnp.exp(-x))


def _silu(x):
    return x * _sigmoid(x)


def _softplus(x):
    return jnp.maximum(x, 0.0) + jnp.log1p(jnp.exp(-jnp.abs(x)))


def _rms(x, g):
    return x * lax.rsqrt(jnp.mean(x * x, axis=-1, keepdims=True) + EPS) * g


def _norm_proj_body(h_ref, g_ref, w_ref, o_ref):
    u = _rms(h_ref[...], g_ref[...])
    o_ref[...] = _dot(u.astype(BF), w_ref[...])


def _norm_proj(h, gain, w, tm):
    b, lp, d = h.shape
    n = w.shape[1]
    return pl.pallas_call(
        _norm_proj_body,
        grid=(b, lp // tm),
        in_specs=[
            pl.BlockSpec((None, tm, d), lambda i, j: (i, j, 0)),
            pl.BlockSpec((1, d), lambda i, j: (0, 0)),
            pl.BlockSpec((d, n), lambda i, j: (0, 0), pipeline_mode=pl.Buffered(1)),
        ],
        out_specs=pl.BlockSpec((None, tm, n), lambda i, j: (i, j, 0)),
        out_shape=jax.ShapeDtypeStruct((b, lp, n), F32),
        compiler_params=pltpu.CompilerParams(
            dimension_semantics=("arbitrary", "arbitrary"), vmem_limit_bytes=VMEM_LIMIT),
        name="norm_proj",
    )(h, gain.reshape(1, d), w)


def _embed_proj_body(x_ref, meta_ref, g_ref, w_ref, p_ref, h_ref, *, lead):
    xb = x_ref[0]
    tm, d = xb.shape
    n_zero = lead - meta_ref.shape[0]
    first = jnp.concatenate([jnp.zeros((n_zero, d), F32), meta_ref[...], xb[0:tm - lead, :]], axis=0)
    blk = jnp.where(pl.program_id(1) == 0, first, xb)
    h_ref[...] = blk
    p_ref[...] = _dot(_rms(blk, g_ref[...]).astype(BF), w_ref[...])


def _embed_proj(x, meta, gain, w, tm, lead):
    b, s, d = x.shape
    lp = s + lead
    n = w.shape[1]
    assert lead % 8 == 0 and meta.shape[0] % 8 == 0 and lead < tm and lp % tm == 0
    x_rows = lambda i, j: (i, pl.multiple_of(jnp.maximum(tm * j - lead, 0), 8), 0)
    return pl.pallas_call(
        functools.partial(_embed_proj_body, lead=lead),
        grid=(b, lp // tm),
        in_specs=[
            pl.BlockSpec((pl.Element(1), pl.Element(tm), pl.Element(d)), x_rows),
            pl.BlockSpec(meta.shape, lambda i, j: (0, 0)),
            pl.BlockSpec((1, d), lambda i, j: (0, 0)),
            pl.BlockSpec((d, n), lambda i, j: (0, 0), pipeline_mode=pl.Buffered(1)),
        ],
        out_specs=[pl.BlockSpec((None, tm, n), lambda i, j: (i, j, 0)),
                   pl.BlockSpec((None, tm, d), lambda i, j: (i, j, 0))],
        out_shape=[jax.ShapeDtypeStruct((b, lp, n), F32), jax.ShapeDtypeStruct((b, lp, d), F32)],
        compiler_params=pltpu.CompilerParams(
            dimension_semantics=("arbitrary", "arbitrary"), vmem_limit_bytes=VMEM_LIMIT),
        name="embed_proj",
    )(x, meta, gain.reshape(1, d), w)


def _shift_rows(a, carry_ref, cols, n_back, row):
    out = []
    for s in range(1, n_back + 1):
        sh = pltpu.roll(a, s, 0)
        for r in range(s):
            sh = jnp.where(row == r, carry_ref[8 - s + r:9 - s + r, cols], sh)
        out.append(sh)
    return out


def _merge_pieces(*lists):
    total = max(len(l) for l in lists)
    keyed = []
    for li, l in enumerate(lists):
        for k, piece in enumerate(l):
            keyed.append(((k + 0.5) * total / len(l), li, k, piece))
    keyed.sort(key=lambda t: t[:3])
    return [t[3] for t in keyed]


def _ffn_pieces(make_x, shift, g2_ref, wup_ref, cw_ref, cb_ref, wdn_ref, g3_ref, emit):
    n_tiles = D_FF // FFN_TILE
    st = {}

    def up(c):
        return [_dot(st["u"], wup_ref[:, part * D_FF + c * FFN_TILE:part * D_FF + (c + 1) * FFN_TILE])
                for part in range(2)]

    def head():
        st["x"] = make_x()
        st["u"] = _rms(st["x"], g2_ref[...]).astype(BF)
        st["acts"] = []

    def lead_in():
        st["pending"] = [up(c) for c in range(min(2, n_tiles))]

    def tile(c):
        def run():
            cur = st["pending"].pop(0)
            if c + 2 < n_tiles:
                st["pending"].append(up(c + 2))
            z = []
            for part in range(2):
                cols = slice(part * D_FF + c * FFN_TILE, part * D_FF + (c + 1) * FFN_TILE)
                a = cur[part]
                am1, am2 = shift(a, cols)
                z.append(am2 * cw_ref[0:1, cols] + am1 * cw_ref[1:2, cols] + a * cw_ref[2:3, cols] + cb_ref[:, cols])
            st["acts"].append((_silu(z[0]) * z[1]).astype(BF))
        return run

    def tail():
        acc = _dot(jnp.concatenate(st["acts"], axis=1), wdn_ref[...])
        emit(st["x"] + _rms(acc, g3_ref[...]))

    return [head, lead_in] + [tile(c) for c in range(n_tiles)] + [tail]


def _ffn_body(y_ref, wout_ref, g1_ref, h_ref, g2_ref, wup_ref, cw_ref, cb_ref, wdn_ref, g3_ref, o_ref, carry_ref,
              *, tm, pad):
    @pl.when(pl.program_id(1) == 0)
    def _():
        carry_ref[...] = jnp.zeros_like(carry_ref)

    half = tm // 2
    lists = []
    for r0 in (0, half):
        rows = slice(r0, r0 + half)
        t = pl.program_id(1) * tm + r0 + lax.broadcasted_iota(jnp.int32, (half, D_MODEL), 0)
        row = lax.broadcasted_iota(jnp.int32, (half, FFN_TILE), 0)

        def make_x(rows=rows, t=t):
            return jnp.where(t >= pad, h_ref[rows, :] + _rms(_dot(y_ref[rows, :], wout_ref[...]), g1_ref[...]), 0.0)

        def shift(a, cols, row=row):
            am1, am2 = _shift_rows(a, carry_ref, cols, FFN_CONV - 1, row)
            carry_ref[:, cols] = a[half - 8:half, :]
            return am1, am2

        def emit(out, rows=rows, t=t):
            o_ref[rows, :] = jnp.where(t >= pad, out, 0.0)

        lists.append(_ffn_pieces(make_x, shift, g2_ref, wup_ref, cw_ref, cb_ref, wdn_ref, g3_ref, emit))
    for piece in _merge_pieces(*lists):
        piece()


def _ffn_tail_body(y_ref, wout_ref, g1_ref, h_ref, g2_ref, wup_ref, cw_ref, cb_ref, wdn_ref, g3_ref, o_ref, *,
                   halo):
    half = o_ref.shape[0] // 2
    shift = lambda a, cols: (pltpu.roll(a, 1, 0), pltpu.roll(a, 2, 0))
    lists = []
    for r0 in (0, half):
        def make_x(r0=r0):
            rows = slice(r0, r0 + half + halo)
            return h_ref[0, rows, :] + _rms(_dot(y_ref[0, rows, :], wout_ref[...]), g1_ref[...])

        def emit(out, r0=r0):
            o_ref[r0:r0 + half, :] = out[halo:, :]

        lists.append(_ffn_pieces(make_x, shift, g2_ref, wup_ref, cw_ref, cb_ref, wdn_ref, g3_ref, emit))
    for piece in _merge_pieces(*lists):
        piece()


def _ffn_weight_specs(k, d, ff2):
    const = lambda i, j: (0, 0)
    return dict(
        w_out=pl.BlockSpec((k, d), const, pipeline_mode=pl.Buffered(1)), gain=pl.BlockSpec((1, d), const),
        w_up=pl.BlockSpec((d, ff2), const, pipeline_mode=pl.Buffered(1)),
        conv_w=pl.BlockSpec((FFN_CONV, ff2), const), conv_b=pl.BlockSpec((1, ff2), const),
        w_down=pl.BlockSpec((ff2 // 2, d), const, pipeline_mode=pl.Buffered(1)))


def _out_ffn(y, w_out, g1, h, g2, w_up, conv_w, conv_b, w_down, g3, tm, pad):
    b, lp, d = h.shape
    k = y.shape[2]
    ff2 = w_up.shape[1]
    assert lp % tm == 0 and tm % 16 == 0
    ws = _ffn_weight_specs(k, d, ff2)
    rows = lambda width: pl.BlockSpec((None, tm, width), lambda i, j: (i, j, 0))
    return pl.pallas_call(
        functools.partial(_ffn_body, tm=tm, pad=pad),
        grid=(b, lp // tm),
        in_specs=[rows(k), ws["w_out"], ws["gain"], rows(d), ws["gain"], ws["w_up"], ws["conv_w"], ws["conv_b"],
                  ws["w_down"], ws["gain"]],
        out_specs=rows(d),
        out_shape=jax.ShapeDtypeStruct((b, lp, d), F32),
        scratch_shapes=[pltpu.VMEM((8, ff2), F32)],
        compiler_params=pltpu.CompilerParams(
            dimension_semantics=("arbitrary", "arbitrary"), vmem_limit_bytes=VMEM_LIMIT),
        name="conv_ffn",
    )(y, w_out, g1.reshape(1, d), h, g2.reshape(1, d), w_up, conv_w, conv_b.reshape(1, ff2), w_down,
      g3.reshape(1, d))


def _out_ffn_tail(y, w_out, g1, h, g2, w_up, conv_w, conv_b, w_down, g3, tm, lead):
    b, lp, d = h.shape
    seq = lp - lead
    k = y.shape[2]
    ff2 = w_up.shape[1]
    halo = 8
    assert seq % tm == 0 and tm % 16 == 0 and lead >= halo and lead % 8 == 0
    ws = _ffn_weight_specs(k, d, ff2)
    rows = lambda width: pl.BlockSpec(
        (pl.Element(1), pl.Element(tm + halo), pl.Element(width)),
        lambda i, j: (i, pl.multiple_of(lead - halo + tm * j, 8), 0))
    return pl.pallas_call(
        functools.partial(_ffn_tail_body, halo=halo),
        grid=(b, seq // tm),
        in_specs=[rows(k), ws["w_out"], ws["gain"], rows(d), ws["gain"], ws["w_up"], ws["conv_w"], ws["conv_b"],
                  ws["w_down"], ws["gain"]],
        out_specs=pl.BlockSpec((None, tm, d), lambda i, j: (i, j, 0)),
        out_shape=jax.ShapeDtypeStruct((b, seq, d), F32),
        compiler_params=pltpu.CompilerParams(
            dimension_semantics=("arbitrary", "arbitrary"), vmem_limit_bytes=VMEM_LIMIT),
        name="conv_ffn_tail",
    )(y, w_out, g1.reshape(1, d), h, g2.reshape(1, d), w_up, conv_w, conv_b.reshape(1, ff2), w_down,
      g3.reshape(1, d))


def _hg_consts():
    c = CHUNK
    t = np.arange(c)
    small = [s for s in HG_LEVELS if s < HG_ROW_SPLIT]
    m = np.zeros((2 + len(small), c, c), np.float32)
    m[0] = t[None, :] <= t[:, None]
    m[1] = t[None, :] > t[:, None]
    masks = np.zeros((len(HG_LEVELS) + 1, c, c), np.float32)
    ii, jj = t[:, None], t[None, :]
    for li, s in enumerate(HG_LEVELS):
        if s < HG_ROW_SPLIT:
            for i in range(c):
                mid = (i // (2 * s)) * (2 * s) + s - 1
                if i % (2 * s) >= s:
                    m[2 + small.index(s), i, mid + 1:i + 1] = 1.0
                else:
                    m[2 + small.index(s), i, i + 1:mid + 1] = 1.0
        masks[li] = (ii // (2 * s) == jj // (2 * s)) & (ii % (2 * s) >= s) & (jj % (2 * s) < s)
    masks[-1] = ii == jj
    return m.reshape(-1, c), masks


def _even_body(p_ref, cos_ref, sin_ref, lbl_ref, hgain_ref, mall_ref, masks_ref, y_ref, sret_ref, shg_ref, *,
               layer):
    @pl.when(pl.program_id(1) == 0)
    def _():
        sret_ref[...] = jnp.zeros_like(sret_ref)
        shg_ref[...] = jnp.zeros_like(shg_ref)

    c = CHUNK
    cosv = cos_ref[...]
    sinv = sin_ref[...]
    ii = lax.broadcasted_iota(jnp.int32, (c, c), 0)
    jj = lax.broadcasted_iota(jnp.int32, (c, c), 1)
    diff = (ii - jj).astype(F32)
    rowi = lax.broadcasted_iota(jnp.int32, (c, RET_DK), 0).astype(F32)

    nb = p_ref.shape[0]
    col = lambda base, h: slice(base + h * 128, base + (h + 1) * 128)
    units = [(b, h) for b in range(nb) for h in range(RET_HEADS)]
    nu = len(units)
    lgs = [math.log1p(-(2.0 ** (-5.0 - h))) for _, h in units]
    qs = [p_ref[b, :, col(0, h)] for b, h in units]
    ks = [p_ref[b, :, col(RET_W, h)] for b, h in units]
    vs = [p_ref[b, :, col(2 * RET_W, h)] for b, h in units]
    qs = [q * cosv + pltpu.roll(q, RET_DK // 2, 1) * sinv for q in qs]
    ks = [(k * cosv + pltpu.roll(k, RET_DK // 2, 1) * sinv) * (RET_DK ** -0.5) for k in ks]
    ss = [sret_ref[b, h] for b, h in units]
    att = [_mm_nt(qs[u], ks[u]) * jnp.where(diff >= 0, jnp.exp(lgs[u] * jnp.maximum(diff, 0.0)), 0.0)
           for u in range(nu)]
    inter = [_mm(qs[u] * jnp.exp(lgs[u] * (rowi + 1.0)), ss[u]) for u in range(nu)]
    kv = [_mm_tn(ks[u] * jnp.exp(lgs[u] * (c - 1.0 - rowi)), vs[u]) for u in range(nu)]
    outs = [_mm(att[u], vs[u]) + inter[u] for u in range(nu)]
    for u, (b, h) in enumerate(units):
        sret_ref[b, h] = ss[u] * math.exp(lgs[u] * c) + kv[u]
        o = outs[u]
        xc = o - jnp.mean(o, axis=-1, keepdims=True)
        oa = xc * lax.rsqrt(jnp.mean(xc * xc, axis=-1, keepdims=True) + EPS)
        y_ref[b, :, col(0, h)] = (oa * _silu(p_ref[b, :, col(3 * RET_W, h)])).astype(y_ref.dtype)

    lgt = lbl_ref[...]
    mx = jnp.max(lgt, axis=0, keepdims=True)
    ex = jnp.exp(lgt - mx)
    sm = ex / jnp.sum(ex, axis=0, keepdims=True)
    lb_all = jnp.sum(sm[0:layer + 1, :], axis=0, keepdims=True)
    base = 4 * RET_W
    mall = mall_ref[...]
    nl = len(HG_LEVELS)
    units = [(b, h) for b in range(nb) for h in range(HG_HEADS)]
    nu = len(units)
    lbs = [lb_all[:, col(0, h)] for _, h in units]
    qs = [p_ref[b, :, col(base, h)] for b, h in units]
    fbs = [p_ref[b, :, col(base + HG_W, h)] for b, h in units]
    vs = [p_ref[b, :, col(base + 2 * HG_W, h)] for b, h in units]
    ks = [(1.0 - lbs[u]) * _sigmoid(-fbs[u]) for u in range(nu)]
    es = [_cmm2(mall, jnp.log(lbs[u] + (1.0 - lbs[u]) * _sigmoid(fbs[u]))) for u in range(nu)]
    xs = [jnp.exp(e) for e in es]

    def level_factor(u, li):
        s = HG_LEVELS[li]
        if s < HG_ROW_SPLIT:
            k = 2 + [t for t in HG_LEVELS if t < HG_ROW_SPLIT].index(s)
            return xs[u][k * c:(k + 1) * c, :]
        cb = es[u][0:c, :]
        parts = []
        for p0 in range(0, c, 2 * s):
            mid = cb[p0 + s - 1:p0 + s, :]
            parts += [mid - cb[p0:p0 + s, :], cb[p0 + s:p0 + 2 * s, :] - mid]
        return jnp.exp(jnp.concatenate(parts, axis=0))

    sts = [shg_ref[b, h] for b, h in units]
    att = [masks_ref[nl] * _mm_nt(qs[u], ks[u]) for u in range(nu)]
    for li in range(nl):
        fac = [level_factor(u, li) for u in range(nu)]
        att = [att[u] + masks_ref[li] * _mm_nt(qs[u] * fac[u], ks[u] * fac[u]) for u in range(nu)]
    inter = [_mm_nt(qs[u] * xs[u][0:c, :], sts[u]) for u in range(nu)]
    kv = [_mm_tn(vs[u], ks[u] * xs[u][c:2 * c, :]) for u in range(nu)]
    outs = [_mm(att[u], vs[u]) + inter[u] for u in range(nu)]
    for u, (b, h) in enumerate(units):
        shg_ref[b, h] = sts[u] * xs[u][c - 1:c, :] + kv[u]
        ob = _rms(outs[u], hgain_ref[:, col(0, h)])
        y_ref[b, :, col(RET_W, h)] = (ob * _silu(p_ref[b, :, col(base + 3 * HG_W, h)])).astype(y_ref.dtype)


def _even_mixer(p, cos_t, sin_t, lb_logits, hg_gain, layer):
    b, lp, n = p.shape
    nc = lp // CHUNK
    nb = EVEN_NB if b % EVEN_NB == 0 else 1
    mall, masks = _hg_consts()
    const2 = lambda i, j: (0, 0)
    return pl.pallas_call(
        functools.partial(_even_body, layer=layer),
        grid=(b // nb, nc),
        in_specs=[
            pl.BlockSpec((nb, CHUNK, n), lambda i, j: (i, j, 0)),
            pl.BlockSpec((CHUNK, RET_DK), lambda i, j: (j, 0)),
            pl.BlockSpec((CHUNK, RET_DK), lambda i, j: (j, 0)),
            pl.BlockSpec(lb_logits.shape, const2),
            pl.BlockSpec((1, HG_W), const2),
            pl.BlockSpec(mall.shape, const2),
            pl.BlockSpec(masks.shape, lambda i, j: (0, 0, 0)),
        ],
        out_specs=pl.BlockSpec((nb, CHUNK, RET_W + HG_W), lambda i, j: (i, j, 0)),
        out_shape=jax.ShapeDtypeStruct((b, lp, RET_W + HG_W), BF),
        scratch_shapes=[pltpu.VMEM((nb, RET_HEADS, RET_DK, RET_DK), F32),
                        pltpu.VMEM((nb, HG_HEADS, HG_DK, HG_DK), F32)],
        compiler_params=pltpu.CompilerParams(
            dimension_semantics=("arbitrary", "arbitrary"), vmem_limit_bytes=VMEM_LIMIT),
        name="even_mixer",
    )(p, cos_t, sin_t, lb_logits, hg_gain.reshape(1, HG_W), jnp.asarray(mall, BF), jnp.asarray(masks))


def _odd_body(p_ref, mu_ref, w0_ref, a0_ref, wa2_ref, g2_ref, kks_ref, kas_ref, rk_ref, lnw_ref, lnb_ref,
              cw_ref, alog_ref, dtb_ref, ggain_ref, tril_ref, bones_ref, y_ref,
              hrw_ref, sgd_ref, cpc_ref, ccv_ref):
    @pl.when(pl.program_id(1) == 0)
    def _():
        hrw_ref[...] = jnp.zeros_like(hrw_ref)
        sgd_ref[...] = jnp.zeros_like(sgd_ref)
        cpc_ref[...] = jnp.zeros_like(cpc_ref)
        ccv_ref[...] = jnp.zeros_like(ccv_ref)

    nb = p_ref.shape[0]
    c = CHUNK
    tril = tril_ref[...]
    bones = bones_ref[...]
    ii = lax.broadcasted_iota(jnp.int32, (c, c), 0)
    jj = lax.broadcasted_iota(jnp.int32, (c, c), 1)
    strict = ii > jj
    incl = ii >= jj
    eye = ii == jj
    eye_f = eye.astype(F32)
    ii2 = lax.broadcasted_iota(jnp.int32, (c, 2 * c), 0)
    lane2 = lax.broadcasted_iota(jnp.int32, (c, 2 * c), 1)
    jj2 = jnp.bitwise_and(lane2, c - 1)
    ak_mask = (ii2 > jj2) & (lane2 >= c)
    rbk_mask = ii2 >= jj2
    lane = lax.broadcasted_iota(jnp.int32, (c, 128), 1)
    heads = [slice(h * RW_HD, (h + 1) * RW_HD) for h in range(RW_HEADS)]
    base = RW_IN
    inv_hd = 1.0 / RW_HD

    def seg_sum(x):
        return jnp.concatenate(
            [_mm(x[:, p * 128:(p + 1) * 128], bones) for p in range(x.shape[1] // 128)], axis=1)

    def prep_pieces(b, d):
        def shift():
            pc = p_ref[b, :, 0:RW_IN]
            row = lax.broadcasted_iota(jnp.int32, (c, RW_IN), 0)
            prev = jnp.where(row == 0, cpc_ref[b, 7:8, :], pltpu.roll(pc, 1, 0))
            cpc_ref[b] = pc[c - 8:c, :]
            pcs = pc + (prev - pc) * mu_ref[...]
            d.update(r=pcs[:, 0:RW_W], k=pcs[:, RW_W:2 * RW_W], v=pcs[:, 2 * RW_W:3 * RW_W],
                     lo=pcs[:, 3 * RW_W:3 * RW_W + 128], glo=pcs[:, 3 * RW_W + 128:RW_IN])

        def lora():
            lo = d["lo"]
            wa = _mm(jnp.where(lane < RW_DECAY_LORA, jnp.tanh(lo), lo), wa2_ref[...])
            log_w = -_softplus(-(w0_ref[...] + wa[:, 0:RW_W])) - 0.5
            d["lw"] = -jnp.exp(log_w)
            d["a"] = _sigmoid(a0_ref[...] + wa[:, RW_W:2 * RW_W])
            d["gate"] = _mm(_sigmoid(d["glo"]), g2_ref[...])

        def keys():
            kks = d["k"] * kks_ref[...]
            kkn = kks * lax.rsqrt(seg_sum(kks * kks) + EPS)
            d["k2"] = d["k"] * (1.0 + (d["a"] - 1.0) * kas_ref[...])
            d["kkn"] = kkn
            d["beta"] = kkn * d["a"]
            d["c_inc"] = _cmm2(tril, d["lw"])

        def decays():
            c_inc, k2, beta = d["c_inc"], d["k2"], d["beta"]
            c_last = c_inc[c - 1:c, :]
            e_neg = jnp.exp(-c_inc)
            e_rest = jnp.exp(c_last - c_inc)
            d.update(ah=-d["kkn"] * jnp.exp(c_inc - d["lw"]), rh=d["r"] * jnp.exp(c_inc), bt=beta * e_neg,
                     kt=k2 * e_neg, bg=beta * e_rest, kg=k2 * e_rest, gc=jnp.exp(c_last),
                     ar=[], ls=[], ak=[], rbk=[], gcol=[], bgkg=[])

        def gram(h):
            def run():
                sl = heads[h]
                ar = jnp.concatenate([d["ah"][:, sl], d["rh"][:, sl]], axis=0).astype(BF)
                gm = _mm_nt(ar, jnp.concatenate([d["bt"][:, sl], d["kt"][:, sl]], axis=0))
                d["ar"].append(ar)
                d["ls"].append(jnp.where(strict, gm[0:c, 0:c], 0.0))
                d["ak"].append(jnp.where(ak_mask, gm[0:c, :], 0.0).astype(BF))
                d["rbk"].append(jnp.where(rbk_mask, gm[c:2 * c, :], 0.0).astype(BF))
                d["gcol"].append(jnp.sum(eye_f * d["gc"][:, sl], axis=1, keepdims=True))
                d["bgkg"].append(jnp.concatenate([d["bg"][:, sl], d["kg"][:, sl]], axis=0).astype(BF))
            return run

        def conv():
            x = p_ref[b, :, base:base + GDN_CONV_CH]
            rowc = lax.broadcasted_iota(jnp.int32, (c, GDN_CONV_CH), 0)
            xm1, xm2, xm3 = _shift_rows(x, ccv_ref.at[b], slice(None), GDN_CONV - 1, rowc)
            ccv_ref[b] = x[c - 8:c, :]
            d["qkv"] = _silu(xm3 * cw_ref[0:1, :] + xm2 * cw_ref[1:2, :] + xm1 * cw_ref[2:3, :] + x * cw_ref[3:4, :])

        def gates():
            sc = p_ref[b, :, base + GDN_CONV_CH + GDN_W:base + GDN_IN_PAD]
            g_all = -jnp.exp(alog_ref[...]) * _softplus(sc + dtb_ref[...])
            d["b_all"] = _sigmoid(sc)
            cg = _cmm3(tril, g_all)
            d["cg_all"] = jnp.concatenate(
                [jnp.broadcast_to(cg[:, h:h + 1], (c, 128)) for h in range(GDN_HEADS)], axis=1)
            d["gd"] = []

        def gdn_head(h):
            def run():
                hs = slice(h * 128, (h + 1) * 128)
                qkv = d["qkv"]
                q = qkv[:, hs]
                kd = qkv[:, GDN_W + h * 128:GDN_W + (h + 1) * 128]
                vd = qkv[:, 2 * GDN_W + h * 128:2 * GDN_W + (h + 1) * 128]
                q = q * lax.rsqrt(jnp.sum(q * q, axis=-1, keepdims=True) + EPS) * (GDN_DK ** -0.5)
                kd = kd * lax.rsqrt(jnp.sum(kd * kd, axis=-1, keepdims=True) + EPS)
                b_b = jnp.broadcast_to(d["b_all"][:, GDN_HEADS + h:GDN_HEADS + h + 1], (c, 128))
                cg = d["cg_all"][:, hs]
                cg_row = jnp.sum(jnp.where(eye, cg[:, 0:c], 0.0), axis=0, keepdims=True)
                decay = jnp.exp(jnp.where(incl, cg[:, 0:c] - cg_row, -jnp.inf))
                eg = jnp.exp(cg)
                cl = cg[c - 1:c, :]
                d["gd"].append(dict(
                    l=-jnp.where(strict, b_b[:, 0:c] * _mm_nt(kd, kd) * decay, 0.0),
                    rhs=jnp.concatenate([b_b * vd, b_b * kd * eg], axis=1).astype(BF),
                    qk=(_mm_nt(q, kd) * decay).astype(BF), q_in=(q * eg).astype(BF),
                    k_out=(kd * jnp.exp(cl - cg)).astype(BF), sd=jnp.exp(cl)))
            return run

        return ([shift, lora, keys, decays] + [gram(h) for h in range(RW_HEADS)]
                + [conv, gates] + [gdn_head(h) for h in range(GDN_HEADS)])

    def inverse_pieces(rows, ds, out):
        st = {}

        def start():
            ls = [l for b in rows for l in ds[b]["ls"]] + [g["l"] for b in rows for g in ds[b]["gd"]]
            st["p"] = [eye_f + l for l in ls]
            st["l"] = [l.astype(BF) for l in ls]

        def square():
            st["l"] = [_mm(l, l).astype(BF) for l in st["l"]]

        def extend():
            st["p"] = [p + _mm(l, p) for l, p in zip(st["l"], st["p"])]

        def finish():
            out["tinv"] = st["p"]

        levels = []
        span = 2
        while span < c:
            levels += [square, extend]
            span *= 2
        return [start] + levels + [finish]

    def state_pieces(rows, ds, inv):
        units = [(b, h) for b in rows for h in range(RW_HEADS)]
        gunits = [(b, h) for b in rows for h in range(GDN_HEADS)]
        st = {}

        def rw_read():
            st["h0"] = [hrw_ref[b, h] for b, h in units]
            st["hr"] = [_mm(ds[b]["ar"][h], st["h0"][u]) for u, (b, h) in enumerate(units)]
            st["vs"] = [ds[b]["v"][:, heads[h]].astype(BF) for b, h in units]

        def rw_mix():
            vs = st["vs"]
            st["x"] = [st["hr"][u][0:c, :] + _mm(ds[b]["ak"][h], jnp.concatenate([vs[u], vs[u]], axis=0))
                       for u, (b, h) in enumerate(units)]

        def rw_solve():
            us = [_mm(inv["tinv"][u], st["x"][u]) for u in range(len(units))]
            st["uv"] = [jnp.concatenate([us[u].astype(BF), st["vs"][u]], axis=0) for u in range(len(units))]

        def rw_out():
            st["ys"] = [st["hr"][u][c:2 * c, :] + _mm(ds[b]["rbk"][h], st["uv"][u])
                        for u, (b, h) in enumerate(units)]
            for u, (b, h) in enumerate(units):
                hrw_ref[b, h] = st["h0"][u] * ds[b]["gcol"][h] + _mm_tn(ds[b]["bgkg"][h], st["uv"][u])

        def rw_finish(k, b):
            def run():
                d = ds[b]
                y = jnp.concatenate(st["ys"][k * RW_HEADS:(k + 1) * RW_HEADS], axis=1)
                yc = y - seg_sum(y) * inv_hd
                yn = yc * lax.rsqrt(seg_sum(yc * yc) * inv_hd + RW_LNX_EPS) * lnw_ref[...] + lnb_ref[...]
                out = (yn + seg_sum(d["r"] * d["k2"] * rk_ref[...]) * d["v"]) * d["gate"]
                y_ref[b, :, 0:RW_W] = out.astype(y_ref.dtype)
            return run

        def gd_solve():
            n_rw = len(units)
            sols = [_mm(inv["tinv"][n_rw + u], ds[b]["gd"][h]["rhs"]) for u, (b, h) in enumerate(gunits)]
            st["s0"] = [sgd_ref[b, h] for b, h in gunits]
            st["v_new"] = [sols[u][:, 0:128] - _mm(sols[u][:, 128:256], st["s0"][u]) for u in range(len(gunits))]

        def gd_out(u, b, h):
            def run():
                g = ds[b]["gd"][h]
                s0, v_new = st["s0"][u], st["v_new"][u]
                o = _mm(g["q_in"], s0) + _mm(g["qk"], v_new)
                sgd_ref[b, h] = s0 * g["sd"] + _mm_tn(g["k_out"], v_new)
                og = p_ref[b, :, base + GDN_CONV_CH + h * 128:base + GDN_CONV_CH + (h + 1) * 128]
                out = _rms(o, ggain_ref[...]) * _silu(og)
                y_ref[b, :, RW_W + h * 128:RW_W + (h + 1) * 128] = out.astype(y_ref.dtype)
            return run

        return ([rw_read, rw_mix, rw_solve, rw_out] + [rw_finish(k, b) for k, b in enumerate(rows)]
                + [gd_solve] + [gd_out(u, b, h) for u, (b, h) in enumerate(gunits)])

    group = min(ODD_GROUP, nb)
    groups = [list(range(g, g + group)) for g in range(0, nb, group)]
    ds = [dict() for _ in range(nb)]
    invs = [dict() for _ in groups]
    stage1 = [_merge_pieces(*[prep_pieces(b, ds[b]) for b in rows]) for rows in groups]
    stage2 = [inverse_pieces(rows, ds, invs[g]) for g, rows in enumerate(groups)]
    stage3 = [state_pieces(rows, ds, invs[g]) for g, rows in enumerate(groups)]
    for slot in range(len(groups) + 2):
        active = []
        if slot < len(groups):
            active.append(stage1[slot])
        if 0 <= slot - 1 < len(groups):
            active.append(stage2[slot - 1])
        if 0 <= slot - 2 < len(groups):
            active.append(stage3[slot - 2])
        for piece in _merge_pieces(*active):
            piece()


def _odd_mixer(p, mu, w0, a0, wa2, g2, kks, kas, rk, lnw, lnb, conv_w, alog, dtb, ggain):
    b, lp, n = p.shape
    nc = lp // CHUNK
    nb = ODD_NB if b % ODD_NB == 0 else 1
    t = np.arange(CHUNK)
    tril = jnp.asarray(t[None, :] <= t[:, None], BF)
    l = np.arange(128)
    bones = jnp.asarray(l[:, None] // RW_HD == l[None, :] // RW_HD, BF)
    row = lambda a: a.reshape(1, -1)
    consts = [row(mu), row(w0), row(a0), wa2, g2, row(kks), row(kas), row(rk), row(lnw), row(lnb),
              conv_w, row(alog), row(dtb), row(ggain), tril, bones]
    const2 = lambda i, j: (0, 0)
    return pl.pallas_call(
        _odd_body,
        grid=(b // nb, nc),
        in_specs=[pl.BlockSpec((nb, CHUNK, n), lambda i, j: (i, j, 0))]
        + [pl.BlockSpec(a.shape, const2) for a in consts],
        out_specs=pl.BlockSpec((nb, CHUNK, RW_W + GDN_W), lambda i, j: (i, j, 0)),
        out_shape=jax.ShapeDtypeStruct((b, lp, RW_W + GDN_W), BF),
        scratch_shapes=[
            pltpu.VMEM((nb, RW_HEADS, RW_HD, RW_HD), F32),
            pltpu.VMEM((nb, GDN_HEADS, GDN_DK, GDN_DK), F32),
            pltpu.VMEM((nb, 8, RW_IN), F32),
            pltpu.VMEM((nb, 8, GDN_CONV_CH), F32),
        ],
        compiler_params=pltpu.CompilerParams(
            dimension_semantics=("arbitrary", "arbitrary"), vmem_limit_bytes=VMEM_LIMIT),
        name="odd_mixer",
    )(p, *consts)


def _row_tile(lp):
    best = None
    for tm in range(16, min(lp, 704) + 1, 16):
        if lp % tm == 0:
            best = tm
    assert best is not None, "sequence length must be a multiple of 16"
    return best


def kernel(x, meta_tokens, norm_gains, w_in_even, w_out_even, hg_lb_logits, hg_norm_gain, w_in_odd, w_out_odd, rw_mu, rw_w0, rw_w2, rw_a0, rw_a2, rw_g2, rw_kk_scale, rw_ka_scale, rw_rk, rw_lnx_w, rw_lnx_b, gdn_conv_w, gdn_a_log, gdn_dt_bias, gdn_norm_gain, ffn_w_up, ffn_conv_w, ffn_conv_b, ffn_w_down):
    bsz, seq, d = x.shape
    depth = norm_gains.shape[0]
    l = N_META + seq
    pad = (-l) % CHUNK
    lp = l + pad
    tm = _row_tile(lp)
    lead = pad + N_META

    half = RET_DK // 2
    pos = (jnp.arange(lp, dtype=jnp.int32) - pad).astype(F32)
    inv = ROPE_BASE ** (-jnp.arange(half, dtype=F32) / half)
    ang = pos[:, None] * inv[None, :]
    cos_t = jnp.concatenate([jnp.cos(ang), jnp.cos(ang)], axis=1)
    sin_t = jnp.concatenate([-jnp.sin(ang), jnp.sin(ang)], axis=1)

    h = None
    for layer in range(depth):
        g = norm_gains[layer]
        i = layer // 2
        if layer % 2 == 0:
            w_in = w_in_even[i].astype(BF)
        else:
            w_in = jnp.pad(w_in_odd[i], ((0, 0), (0, ODD_IN_PAD - w_in_odd.shape[2]))).astype(BF)
        if layer == 0:
            p, h = _embed_proj(x, meta_tokens.astype(x.dtype), g[0], w_in, tm, lead)
        else:
            p = _norm_proj(h, g[0], w_in, tm)
        if layer % 2 == 0:
            y = _even_mixer(p, cos_t, sin_t, hg_lb_logits, hg_norm_gain[i], layer)
            w_out = w_out_even[i]
        else:
            wa2 = jnp.zeros((RW_DECAY_LORA + RW_AAA_LORA, 2 * RW_W), F32)
            wa2 = wa2.at[:RW_DECAY_LORA, :RW_W].set(rw_w2[i]).at[RW_DECAY_LORA:, RW_W:].set(rw_a2[i])
            lane_pad = lambda a: jnp.pad(a, (0, LANES - a.shape[0]))
            y = _odd_mixer(p, rw_mu[i], rw_w0[i], rw_a0[i], wa2.astype(BF), rw_g2[i].astype(BF),
                           rw_kk_scale[i], rw_ka_scale[i], rw_rk[i].reshape(-1), rw_lnx_w[i], rw_lnx_b[i],
                           gdn_conv_w[i], lane_pad(gdn_a_log[i]), lane_pad(gdn_dt_bias[i]), gdn_norm_gain[i])
            w_out = w_out_odd[i]
        ffn_args = (y, w_out.astype(BF), g[1], h, g[2], ffn_w_up[layer].astype(BF), ffn_conv_w[layer],
                    ffn_conv_b[layer], ffn_w_down[layer].astype(BF), g[3])
        if layer + 1 < depth:
            h = _out_ffn(*ffn_args, tm, pad)
        else:
            h = _out_ffn_tail(*ffn_args, _row_tile(seq), lead)
    return h
```

```python
import functools
import math

import numpy as np
import jax
import jax.numpy as jnp
from jax import lax
from jax.experimental import pallas as pl
from jax.experimental.pallas import tpu as pltpu

F32 = jnp.float32
BF = jnp.bfloat16

D_MODEL = 1024
CHUNK = 64
N_META = 16
EPS = 1e-6
ROPE_BASE = 10000.0

RET_HEADS = D_MODEL // 256
RET_DK = 128
HG_HEADS = D_MODEL // 256
HG_DK = 128
RW_HEADS = D_MODEL // 128
RW_HD = 64
RW_DECAY_LORA = 64
RW_AAA_LORA = 64
RW_GATE_LORA = 128
RW_LNX_EPS = 64e-5
GDN_HEADS = D_MODEL // 256
GDN_DK = 128
GDN_CONV = 4
D_FF = 128 * ((8 * D_MODEL // 3 + 127) // 128)
FFN_CONV = 3

RET_W = RET_HEADS * RET_DK
HG_W = HG_HEADS * HG_DK
RW_W = RW_HEADS * RW_HD
RW_IN = 3 * RW_W + RW_DECAY_LORA + RW_AAA_LORA + RW_GATE_LORA
GDN_W = GDN_HEADS * GDN_DK
GDN_CONV_CH = 3 * GDN_W
GDN_IN = GDN_CONV_CH + GDN_W + 2 * GDN_HEADS
LANES = 128
GDN_IN_PAD = LANES * ((GDN_IN + LANES - 1) // LANES)
ODD_IN_PAD = RW_IN + GDN_IN_PAD
HG_LEVELS = (32, 16, 8, 4, 2, 1)
HG_ROW_SPLIT = 8
VMEM_LIMIT = 56 * 1024 * 1024
FFN_TILE = 256
EVEN_NB = 8
ODD_NB = 4
ODD_GROUP = 2


def _dot(a, b):
    return jnp.dot(a, b, preferred_element_type=F32)


def _mm(a, b):
    return _dot(a.astype(BF), b.astype(BF))


def _mm_nt(a, b):
    return lax.dot_general(a.astype(BF), b.astype(BF), (((1,), (1,)), ((), ())), preferred_element_type=F32)


def _mm_tn(a, b):
    return lax.dot_general(a.astype(BF), b.astype(BF), (((0,), (0,)), ((), ())), preferred_element_type=F32)


def _split2(x):
    hi = x.astype(BF)
    lo = (x - hi.astype(F32)).astype(BF)
    return hi, lo


def _split3(x):
    hi = x.astype(BF)
    r1 = x - hi.astype(F32)
    mid = r1.astype(BF)
    lo = (r1 - mid.astype(F32)).astype(BF)
    return hi, mid, lo


def _cmm3(c, x):
    hi, mid, lo = _split3(x)
    return _dot(c, hi) + _dot(c, mid) + _dot(c, lo)


def _cmm2(c, x):
    hi, lo = _split2(x)
    return _dot(c, hi) + _dot(c, lo)


def _sigmoid(x):
    return 1.0 / (1.0 + jnp.exp(-x))


def _silu(x):
    return x * _sigmoid(x)


def _softplus(x):
    return jnp.maximum(x, 0.0) + jnp.log1p(jnp.exp(-jnp.abs(x)))


def _rms(x, g):
    return x * lax.rsqrt(jnp.mean(x * x, axis=-1, keepdims=True) + EPS) * g


def _norm_proj_body(h_ref, g_ref, w_ref, o_ref):
    u = _rms(h_ref[...], g_ref[...])
    o_ref[...] = _dot(u.astype(BF), w_ref[...])


def _norm_proj(h, gain, w, tm):
    b, lp, d = h.shape
    n = w.shape[1]
    return pl.pallas_call(
        _norm_proj_body,
        grid=(b, lp // tm),
        in_specs=[
            pl.BlockSpec((None, tm, d), lambda i, j: (i, j, 0)),
            pl.BlockSpec((1, d), lambda i, j: (0, 0)),
            pl.BlockSpec((d, n), lambda i, j: (0, 0), pipeline_mode=pl.Buffered(1)),
        ],
        out_specs=pl.BlockSpec((None, tm, n), lambda i, j: (i, j, 0)),
        out_shape=jax.ShapeDtypeStruct((b, lp, n), F32),
        compiler_params=pltpu.CompilerParams(
            dimension_semantics=("arbitrary", "arbitrary"), vmem_limit_bytes=VMEM_LIMIT),
        name="norm_proj",
    )(h, gain.reshape(1, d), w)


def _embed_proj_body(x_ref, meta_ref, g_ref, w_ref, p_ref, h_ref, *, lead):
    xb = x_ref[0]
    tm, d = xb.shape
    n_zero = lead - meta_ref.shape[0]
    first = jnp.concatenate([jnp.zeros((n_zero, d), F32), meta_ref[...], xb[0:tm - lead, :]], axis=0)
    blk = jnp.where(pl.program_id(1) == 0, first, xb)
    h_ref[...] = blk
    p_ref[...] = _dot(_rms(blk, g_ref[...]).astype(BF), w_ref[...])


def _embed_proj(x, meta, gain, w, tm, lead):
    b, s, d = x.shape
    lp = s + lead
    n = w.shape[1]
    assert lead % 8 == 0 and meta.shape[0] % 8 == 0 and lead < tm and lp % tm == 0
    x_rows = lambda i, j: (i, pl.multiple_of(jnp.maximum(tm * j - lead, 0), 8), 0)
    return pl.pallas_call(
        functools.partial(_embed_proj_body, lead=lead),
        grid=(b, lp // tm),
        in_specs=[
            pl.BlockSpec((pl.Element(1), pl.Element(tm), pl.Element(d)), x_rows),
            pl.BlockSpec(meta.shape, lambda i, j: (0, 0)),
            pl.BlockSpec((1, d), lambda i, j: (0, 0)),
            pl.BlockSpec((d, n), lambda i, j: (0, 0), pipeline_mode=pl.Buffered(1)),
        ],
        out_specs=[pl.BlockSpec((None, tm, n), lambda i, j: (i, j, 0)),
                   pl.BlockSpec((None, tm, d), lambda i, j: (i, j, 0))],
        out_shape=[jax.ShapeDtypeStruct((b, lp, n), F32), jax.ShapeDtypeStruct((b, lp, d), F32)],
        compiler_params=pltpu.CompilerParams(
            dimension_semantics=("arbitrary", "arbitrary"), vmem_limit_bytes=VMEM_LIMIT),
        name="embed_proj",
    )(x, meta, gain.reshape(1, d), w)


def _shift_rows(a, carry_ref, cols, n_back, row):
    out = []
    for s in range(1, n_back + 1):
        sh = pltpu.roll(a, s, 0)
        for r in range(s):
            sh = jnp.where(row == r, carry_ref[8 - s + r:9 - s + r, cols], sh)
        out.append(sh)
    return out


def _merge_pieces(*lists):
    total = max(len(l) for l in lists)
    keyed = []
    for li, l in enumerate(lists):
        for k, piece in enumerate(l):
            keyed.append(((k + 0.5) * total / len(l), li, k, piece))
    keyed.sort(key=lambda t: t[:3])
    return [t[3] for t in keyed]


def _ffn_pieces(make_x, shift, g2_ref, wup_ref, cw_ref, cb_ref, wdn_ref, g3_ref, emit):
    n_tiles = D_FF // FFN_TILE
    st = {}

    def up(c):
        return [_dot(st["u"], wup_ref[:, part * D_FF + c * FFN_TILE:part * D_FF + (c + 1) * FFN_TILE])
                for part in range(2)]

    def head():
        st["x"] = make_x()
        st["u"] = _rms(st["x"], g2_ref[...]).astype(BF)
        st["acts"] = []

    def lead_in():
        st["pending"] = [up(c) for c in range(min(2, n_tiles))]

    def tile(c):
        def run():
            cur = st["pending"].pop(0)
            if c + 2 < n_tiles:
                st["pending"].append(up(c + 2))
            z = []
            for part in range(2):
                cols = slice(part * D_FF + c * FFN_TILE, part * D_FF + (c + 1) * FFN_TILE)
                a = cur[part]
                am1, am2 = shift(a, cols)
                z.append(am2 * cw_ref[0:1, cols] + am1 * cw_ref[1:2, cols] + a * cw_ref[2:3, cols] + cb_ref[:, cols])
            st["acts"].append((_silu(z[0]) * z[1]).astype(BF))
        return run

    def tail():
        acc = _dot(jnp.concatenate(st["acts"], axis=1), wdn_ref[...])
        emit(st["x"] + _rms(acc, g3_ref[...]))

    return [head, lead_in] + [tile(c) for c in range(n_tiles)] + [tail]


def _ffn_body(y_ref, wout_ref, g1_ref, h_ref, g2_ref, wup_ref, cw_ref, cb_ref, wdn_ref, g3_ref, o_ref, carry_ref,
              *, tm, pad):
    @pl.when(pl.program_id(1) == 0)
    def _():
        carry_ref[...] = jnp.zeros_like(carry_ref)

    half = tm // 2
    lists = []
    for r0 in (0, half):
        rows = slice(r0, r0 + half)
        t = pl.program_id(1) * tm + r0 + lax.broadcasted_iota(jnp.int32, (half, D_MODEL), 0)
        row = lax.broadcasted_iota(jnp.int32, (half, FFN_TILE), 0)

        def make_x(rows=rows, t=t):
            return jnp.where(t >= pad, h_ref[rows, :] + _rms(_dot(y_ref[rows, :], wout_ref[...]), g1_ref[...]), 0.0)

        def shift(a, cols, row=row):
            am1, am2 = _shift_rows(a, carry_ref, cols, FFN_CONV - 1, row)
            carry_ref[:, cols] = a[half - 8:half, :]
            return am1, am2

        def emit(out, rows=rows, t=t):
            o_ref[rows, :] = jnp.where(t >= pad, out, 0.0)

        lists.append(_ffn_pieces(make_x, shift, g2_ref, wup_ref, cw_ref, cb_ref, wdn_ref, g3_ref, emit))
    for piece in _merge_pieces(*lists):
        piece()


def _ffn_tail_body(y_ref, wout_ref, g1_ref, h_ref, g2_ref, wup_ref, cw_ref, cb_ref, wdn_ref, g3_ref, o_ref, *,
                   halo):
    half = o_ref.shape[0] // 2
    shift = lambda a, cols: (pltpu.roll(a, 1, 0), pltpu.roll(a, 2, 0))
    lists = []
    for r0 in (0, half):
        def make_x(r0=r0):
            rows = slice(r0, r0 + half + halo)
            return h_ref[0, rows, :] + _rms(_dot(y_ref[0, rows, :], wout_ref[...]), g1_ref[...])

        def emit(out, r0=r0):
            o_ref[r0:r0 + half, :] = out[halo:, :]

        lists.append(_ffn_pieces(make_x, shift, g2_ref, wup_ref, cw_ref, cb_ref, wdn_ref, g3_ref, emit))
    for piece in _merge_pieces(*lists):
        piece()


def _ffn_weight_specs(k, d, ff2):
    const = lambda i, j: (0, 0)
    return dict(
        w_out=pl.BlockSpec((k, d), const, pipeline_mode=pl.Buffered(1)), gain=pl.BlockSpec((1, d), const),
        w_up=pl.BlockSpec((d, ff2), const, pipeline_mode=pl.Buffered(1)),
        conv_w=pl.BlockSpec((FFN_CONV, ff2), const), conv_b=pl.BlockSpec((1, ff2), const),
        w_down=pl.BlockSpec((ff2 // 2, d), const, pipeline_mode=pl.Buffered(1)))


def _out_ffn(y, w_out, g1, h, g2, w_up, conv_w, conv_b, w_down, g3, tm, pad):
    b, lp, d = h.shape
    k = y.shape[2]
    ff2 = w_up.shape[1]
    assert lp % tm == 0 and tm % 16 == 0
    ws = _ffn_weight_specs(k, d, ff2)
    rows = lambda width: pl.BlockSpec((None, tm, width), lambda i, j: (i, j, 0))
    return pl.pallas_call(
        functools.partial(_ffn_body, tm=tm, pad=pad),
        grid=(b, lp // tm),
        in_specs=[rows(k), ws["w_out"], ws["gain"], rows(d), ws["gain"], ws["w_up"], ws["conv_w"], ws["conv_b"],
                  ws["w_down"], ws["gain"]],
        out_specs=rows(d),
        out_shape=jax.ShapeDtypeStruct((b, lp, d), F32),
        scratch_shapes=[pltpu.VMEM((8, ff2), F32)],
        compiler_params=pltpu.CompilerParams(
            dimension_semantics=("arbitrary", "arbitrary"), vmem_limit_bytes=VMEM_LIMIT),
        name="conv_ffn",
    )(y, w_out, g1.reshape(1, d), h, g2.reshape(1, d), w_up, conv_w, conv_b.reshape(1, ff2), w_down,
      g3.reshape(1, d))


def _out_ffn_tail(y, w_out, g1, h, g2, w_up, conv_w, conv_b, w_down, g3, tm, lead):
    b, lp, d = h.shape
    seq = lp - lead
    k = y.shape[2]
    ff2 = w_up.shape[1]
    halo = 8
    assert seq % tm == 0 and tm % 16 == 0 and lead >= halo and lead % 8 == 0
    ws = _ffn_weight_specs(k, d, ff2)
    rows = lambda width: pl.BlockSpec(
        (pl.Element(1), pl.Element(tm + halo), pl.Element(width)),
        lambda i, j: (i, pl.multiple_of(lead - halo + tm * j, 8), 0))
    return pl.pallas_call(
        functools.partial(_ffn_tail_body, halo=halo),
        grid=(b, seq // tm),
        in_specs=[rows(k), ws["w_out"], ws["gain"], rows(d), ws["gain"], ws["w_up"], ws["conv_w"], ws["conv_b"],
                  ws["w_down"], ws["gain"]],
        out_specs=pl.BlockSpec((None, tm, d), lambda i, j: (i, j, 0)),
        out_shape=jax.ShapeDtypeStruct((b, seq, d), F32),
        compiler_params=pltpu.CompilerParams(
            dimension_semantics=("arbitrary", "arbitrary"), vmem_limit_bytes=VMEM_LIMIT),
        name="conv_ffn_tail",
    )(y, w_out, g1.reshape(1, d), h, g2.reshape(1, d), w_up, conv_w, conv_b.reshape(1, ff2), w_down,
      g3.reshape(1, d))


def _hg_consts():
    c = CHUNK
    t = np.arange(c)
    small = [s for s in HG_LEVELS if s < HG_ROW_SPLIT]
    m = np.zeros((1 + len(small), c, c), np.float32)
    m[0] = t[None, :] <= t[:, None]
    masks = np.zeros((len(HG_LEVELS) + 1, c, c), np.float32)
    ii, jj = t[:, None], t[None, :]
    for li, s in enumerate(HG_LEVELS):
        if s < HG_ROW_SPLIT:
            for i in range(c):
                mid = (i // (2 * s)) * (2 * s) + s - 1
                if i % (2 * s) >= s:
                    m[1 + small.index(s), i, mid + 1:i + 1] = 1.0
                else:
                    m[1 + small.index(s), i, i + 1:mid + 1] = 1.0
        masks[li] = (ii // (2 * s) == jj // (2 * s)) & (ii % (2 * s) >= s) & (jj % (2 * s) < s)
    masks[-1] = ii == jj
    return m.reshape(-1, c), masks


def _even_body(p_ref, cos_ref, sin_ref, lbl_ref, hgain_ref, mall_ref, masks_ref, y_ref, sret_ref, shg_ref, *,
               layer):
    @pl.when(pl.program_id(1) == 0)
    def _():
        sret_ref[...] = jnp.zeros_like(sret_ref)
        shg_ref[...] = jnp.zeros_like(shg_ref)

    c = CHUNK
    cosv = cos_ref[...]
    sinv = sin_ref[...]
    ii = lax.broadcasted_iota(jnp.int32, (c, c), 0)
    jj = lax.broadcasted_iota(jnp.int32, (c, c), 1)
    diff = (ii - jj).astype(F32)
    rowi = lax.broadcasted_iota(jnp.int32, (c, RET_DK), 0).astype(F32)

    nb = p_ref.shape[0]
    col = lambda base, h: slice(base + h * 128, base + (h + 1) * 128)
    units = [(b, h) for b in range(nb) for h in range(RET_HEADS)]
    nu = len(units)
    lgs = [math.log1p(-(2.0 ** (-5.0 - h))) for _, h in units]
    qs = [p_ref[b, :, col(0, h)] for b, h in units]
    ks = [p_ref[b, :, col(RET_W, h)] for b, h in units]
    vs = [p_ref[b, :, col(2 * RET_W, h)] for b, h in units]
    qs = [q * cosv + pltpu.roll(q, RET_DK // 2, 1) * sinv for q in qs]
    ks = [(k * cosv + pltpu.roll(k, RET_DK // 2, 1) * sinv) * (RET_DK ** -0.5) for k in ks]
    ss = [sret_ref[b, h] for b, h in units]
    att = [_mm_nt(qs[u], ks[u]) * jnp.where(diff >= 0, jnp.exp(lgs[u] * jnp.maximum(diff, 0.0)), 0.0)
           for u in range(nu)]
    inter = [_mm(qs[u] * jnp.exp(lgs[u] * (rowi + 1.0)), ss[u]) for u in range(nu)]
    kv = [_mm_tn(ks[u] * jnp.exp(lgs[u] * (c - 1.0 - rowi)), vs[u]) for u in range(nu)]
    outs = [_mm(att[u], vs[u]) + inter[u] for u in range(nu)]
    for u, (b, h) in enumerate(units):
        sret_ref[b, h] = ss[u] * math.exp(lgs[u] * c) + kv[u]
        o = outs[u]
        xc = o - jnp.mean(o, axis=-1, keepdims=True)
        oa = xc * lax.rsqrt(jnp.mean(xc * xc, axis=-1, keepdims=True) + EPS)
        y_ref[b, :, col(0, h)] = (oa * _silu(p_ref[b, :, col(3 * RET_W, h)])).astype(y_ref.dtype)

    lgt = lbl_ref[...]
    mx = jnp.max(lgt, axis=0, keepdims=True)
    ex = jnp.exp(lgt - mx)
    sm = ex / jnp.sum(ex, axis=0, keepdims=True)
    lb_all = jnp.sum(sm[0:layer + 1, :], axis=0, keepdims=True)
    base = 4 * RET_W
    mall = mall_ref[...]
    nl = len(HG_LEVELS)
    units = [(b, h) for b in range(nb) for h in range(HG_HEADS)]
    nu = len(units)
    lbs = [lb_all[:, col(0, h)] for _, h in units]
    qs = [p_ref[b, :, col(base, h)] for b, h in units]
    fbs = [p_ref[b, :, col(base + HG_W, h)] for b, h in units]
    vs = [p_ref[b, :, col(base + 2 * HG_W, h)] for b, h in units]
    ks = [(1.0 - lbs[u]) * _sigmoid(-fbs[u]) for u in range(nu)]
    es = [_cmm2(mall, jnp.log(lbs[u] + (1.0 - lbs[u]) * _sigmoid(fbs[u]))) for u in range(nu)]
    xs = [jnp.exp(e) for e in es]
    rest = [jnp.exp(e[c - 1:c, :] - e[0:c, :]) for e in es]

    def level_factor(u, li):
        s = HG_LEVELS[li]
        if s < HG_ROW_SPLIT:
            k = 1 + [t for t in HG_LEVELS if t < HG_ROW_SPLIT].index(s)
            return xs[u][k * c:(k + 1) * c, :]
        cb = es[u][0:c, :]
        parts = []
        for p0 in range(0, c, 2 * s):
            mid = cb[p0 + s - 1:p0 + s, :]
            parts += [mid - cb[p0:p0 + s, :], cb[p0 + s:p0 + 2 * s, :] - mid]
        return jnp.exp(jnp.concatenate(parts, axis=0))

    sts = [shg_ref[b, h] for b, h in units]
    att = [masks_ref[nl] * _mm_nt(qs[u], ks[u]) for u in range(nu)]
    for li in range(nl):
        fac = [level_factor(u, li) for u in range(nu)]
        att = [att[u] + masks_ref[li] * _mm_nt(qs[u] * fac[u], ks[u] * fac[u]) for u in range(nu)]
    inter = [_mm_nt(qs[u] * xs[u][0:c, :], sts[u]) for u in range(nu)]
    kv = [_mm_tn(vs[u], ks[u] * rest[u]) for u in range(nu)]
    outs = [_mm(att[u], vs[u]) + inter[u] for u in range(nu)]
    for u, (b, h) in enumerate(units):
        shg_ref[b, h] = sts[u] * xs[u][c - 1:c, :] + kv[u]
        ob = _rms(outs[u], hgain_ref[:, col(0, h)])
        y_ref[b, :, col(RET_W, h)] = (ob * _silu(p_ref[b, :, col(base + 3 * HG_W, h)])).astype(y_ref.dtype)


def _even_mixer(p, cos_t, sin_t, lb_logits, hg_gain, layer):
    b, lp, n = p.shape
    nc = lp // CHUNK
    nb = EVEN_NB if b % EVEN_NB == 0 else 1
    mall, masks = _hg_consts()
    const2 = lambda i, j: (0, 0)
    return pl.pallas_call(
        functools.partial(_even_body, layer=layer),
        grid=(b // nb, nc),
        in_specs=[
            pl.BlockSpec((nb, CHUNK, n), lambda i, j: (i, j, 0)),
            pl.BlockSpec((CHUNK, RET_DK), lambda i, j: (j, 0)),
            pl.BlockSpec((CHUNK, RET_DK), lambda i, j: (j, 0)),
            pl.BlockSpec(lb_logits.shape, const2),
            pl.BlockSpec((1, HG_W), const2),
            pl.BlockSpec(mall.shape, const2),
            pl.BlockSpec(masks.shape, lambda i, j: (0, 0, 0)),
        ],
        out_specs=pl.BlockSpec((nb, CHUNK, RET_W + HG_W), lambda i, j: (i, j, 0)),
        out_shape=jax.ShapeDtypeStruct((b, lp, RET_W + HG_W), BF),
        scratch_shapes=[pltpu.VMEM((nb, RET_HEADS, RET_DK, RET_DK), F32),
                        pltpu.VMEM((nb, HG_HEADS, HG_DK, HG_DK), F32)],
        compiler_params=pltpu.CompilerParams(
            dimension_semantics=("arbitrary", "arbitrary"), vmem_limit_bytes=VMEM_LIMIT),
        name="even_mixer",
    )(p, cos_t, sin_t, lb_logits, hg_gain.reshape(1, HG_W), jnp.asarray(mall, BF), jnp.asarray(masks))


def _odd_body(p_ref, mu_ref, w0_ref, a0_ref, wa2_ref, g2_ref, kks_ref, kas_ref, rk_ref, lnw_ref, lnb_ref,
              cw_ref, alog_ref, dtb_ref, ggain_ref, tril_ref, bones_ref, y_ref,
              hrw_ref, sgd_ref, cpc_ref, ccv_ref):
    @pl.when(pl.program_id(1) == 0)
    def _():
        hrw_ref[...] = jnp.zeros_like(hrw_ref)
        sgd_ref[...] = jnp.zeros_like(sgd_ref)
        cpc_ref[...] = jnp.zeros_like(cpc_ref)
        ccv_ref[...] = jnp.zeros_like(ccv_ref)

    nb = p_ref.shape[0]
    c = CHUNK
    tril = tril_ref[...]
    bones = bones_ref[...]
    ii = lax.broadcasted_iota(jnp.int32, (c, c), 0)
    jj = lax.broadcasted_iota(jnp.int32, (c, c), 1)
    strict = ii > jj
    incl = ii >= jj
    eye = ii == jj
    eye_f = eye.astype(F32)
    ii2 = lax.broadcasted_iota(jnp.int32, (c, 2 * c), 0)
    lane2 = lax.broadcasted_iota(jnp.int32, (c, 2 * c), 1)
    jj2 = jnp.bitwise_and(lane2, c - 1)
    ak_mask = (ii2 > jj2) & (lane2 >= c)
    rbk_mask = ii2 >= jj2
    lane = lax.broadcasted_iota(jnp.int32, (c, 128), 1)
    heads = [slice(h * RW_HD, (h + 1) * RW_HD) for h in range(RW_HEADS)]
    base = RW_IN
    inv_hd = 1.0 / RW_HD

    def seg_sum(x):
        return jnp.concatenate(
            [_mm(x[:, p * 128:(p + 1) * 128], bones) for p in range(x.shape[1] // 128)], axis=1)

    def prep_pieces(b, d):
        def shift():
            pc = p_ref[b, :, 0:RW_IN]
            row = lax.broadcasted_iota(jnp.int32, (c, RW_IN), 0)
            prev = jnp.where(row == 0, cpc_ref[b, 7:8, :], pltpu.roll(pc, 1, 0))
            cpc_ref[b] = pc[c - 8:c, :]
            pcs = pc + (prev - pc) * mu_ref[...]
            d.update(r=pcs[:, 0:RW_W], k=pcs[:, RW_W:2 * RW_W], v=pcs[:, 2 * RW_W:3 * RW_W],
                     lo=pcs[:, 3 * RW_W:3 * RW_W + 128], glo=pcs[:, 3 * RW_W + 128:RW_IN])

        def lora():
            lo = d["lo"]
            wa = _mm(jnp.where(lane < RW_DECAY_LORA, jnp.tanh(lo), lo), wa2_ref[...])
            log_w = -_softplus(-(w0_ref[...] + wa[:, 0:RW_W])) - 0.5
            d["lw"] = -jnp.exp(log_w)
            d["a"] = _sigmoid(a0_ref[...] + wa[:, RW_W:2 * RW_W])
            d["gate"] = _mm(_sigmoid(d["glo"]), g2_ref[...])

        def keys():
            kks = d["k"] * kks_ref[...]
            kkn = kks * lax.rsqrt(seg_sum(kks * kks) + EPS)
            d["k2"] = d["k"] * (1.0 + (d["a"] - 1.0) * kas_ref[...])
            d["kkn"] = kkn
            d["beta"] = kkn * d["a"]
            d["c_inc"] = _cmm2(tril, d["lw"])

        def decays():
            c_inc, k2, beta = d["c_inc"], d["k2"], d["beta"]
            c_last = c_inc[c - 1:c, :]
            e_neg = jnp.exp(-c_inc)
            e_rest = jnp.exp(c_last - c_inc)
            d.update(ah=-d["kkn"] * jnp.exp(c_inc - d["lw"]), rh=d["r"] * jnp.exp(c_inc), bt=beta * e_neg,
                     kt=k2 * e_neg, bg=beta * e_rest, kg=k2 * e_rest, gc=jnp.exp(c_last),
                     ar=[], ls=[], ak=[], rbk=[], gcol=[], bgkg=[])

        def gram(h):
            def run():
                sl = heads[h]
                ar = jnp.concatenate([d["ah"][:, sl], d["rh"][:, sl]], axis=0).astype(BF)
                gm = _mm_nt(ar, jnp.concatenate([d["bt"][:, sl], d["kt"][:, sl]], axis=0))
                d["ar"].append(ar)
                d["ls"].append(jnp.where(strict, gm[0:c, 0:c], 0.0))
                d["ak"].append(jnp.where(ak_mask, gm[0:c, :], 0.0).astype(BF))
                d["rbk"].append(jnp.where(rbk_mask, gm[c:2 * c, :], 0.0).astype(BF))
                d["gcol"].append(jnp.sum(eye_f * d["gc"][:, sl], axis=1, keepdims=True))
                d["bgkg"].append(jnp.concatenate([d["bg"][:, sl], d["kg"][:, sl]], axis=0).astype(BF))
            return run

        def conv():
            x = p_ref[b, :, base:base + GDN_CONV_CH]
            rowc = lax.broadcasted_iota(jnp.int32, (c, GDN_CONV_CH), 0)
            xm1, xm2, xm3 = _shift_rows(x, ccv_ref.at[b], slice(None), GDN_CONV - 1, rowc)
            ccv_ref[b] = x[c - 8:c, :]
            d["qkv"] = _silu(xm3 * cw_ref[0:1, :] + xm2 * cw_ref[1:2, :] + xm1 * cw_ref[2:3, :] + x * cw_ref[3:4, :])

        def gates():
            sc = p_ref[b, :, base + GDN_CONV_CH + GDN_W:base + GDN_IN_PAD]
            g_all = -jnp.exp(alog_ref[...]) * _softplus(sc + dtb_ref[...])
            d["b_all"] = _sigmoid(sc)
            cg = _cmm3(tril, g_all)
            d["cg_all"] = jnp.concatenate(
                [jnp.broadcast_to(cg[:, h:h + 1], (c, 128)) for h in range(GDN_HEADS)], axis=1)
            d["gd"] = []

        def gdn_head(h):
            def run():
                hs = slice(h * 128, (h + 1) * 128)
                qkv = d["qkv"]
                q = qkv[:, hs]
                kd = qkv[:, GDN_W + h * 128:GDN_W + (h + 1) * 128]
                vd = qkv[:, 2 * GDN_W + h * 128:2 * GDN_W + (h + 1) * 128]
                q = q * lax.rsqrt(jnp.sum(q * q, axis=-1, keepdims=True) + EPS) * (GDN_DK ** -0.5)
                kd = kd * lax.rsqrt(jnp.sum(kd * kd, axis=-1, keepdims=True) + EPS)
                b_b = jnp.broadcast_to(d["b_all"][:, GDN_HEADS + h:GDN_HEADS + h + 1], (c, 128))
                cg = d["cg_all"][:, hs]
                cg_row = jnp.sum(jnp.where(eye, cg[:, 0:c], 0.0), axis=0, keepdims=True)
                decay = jnp.exp(jnp.where(incl, cg[:, 0:c] - cg_row, -jnp.inf))
                eg = jnp.exp(cg)
                cl = cg[c - 1:c, :]
                d["gd"].append(dict(
                    l=-jnp.where(strict, b_b[:, 0:c] * _mm_nt(kd, kd) * decay, 0.0),
                    rhs=jnp.concatenate([b_b * vd, b_b * kd * eg], axis=1).astype(BF),
                    qk=(_mm_nt(q, kd) * decay).astype(BF), q_in=(q * eg).astype(BF),
                    k_out=(kd * jnp.exp(cl - cg)).astype(BF), sd=jnp.exp(cl)))
            return run

        return ([shift, lora, keys, decays] + [gram(h) for h in range(RW_HEADS)]
                + [conv, gates] + [gdn_head(h) for h in range(GDN_HEADS)])

    def inverse_pieces(rows, ds, out):
        st = {}

        def start():
            ls = [l for b in rows for l in ds[b]["ls"]] + [g["l"] for b in rows for g in ds[b]["gd"]]
            st["p"] = [eye_f + l for l in ls]
            st["l"] = [l.astype(BF) for l in ls]

        def square():
            st["l"] = [_mm(l, l).astype(BF) for l in st["l"]]

        def extend():
            st["p"] = [p + _mm(l, p) for l, p in zip(st["l"], st["p"])]

        def finish():
            out["tinv"] = st["p"]

        levels = []
        span = 2
        while span < c:
            levels += [square, extend]
            span *= 2
        return [start] + levels + [finish]

    def state_pieces(rows, ds, inv):
        units = [(b, h) for b in rows for h in range(RW_HEADS)]
        gunits = [(b, h) for b in rows for h in range(GDN_HEADS)]
        st = {}

        def rw_read():
            st["h0"] = [hrw_ref[b, h] for b, h in units]
            st["hr"] = [_mm(ds[b]["ar"][h], st["h0"][u]) for u, (b, h) in enumerate(units)]
            st["vs"] = [ds[b]["v"][:, heads[h]].astype(BF) for b, h in units]

        def rw_mix():
            vs = st["vs"]
            st["x"] = [st["hr"][u][0:c, :] + _mm(ds[b]["ak"][h], jnp.concatenate([vs[u], vs[u]], axis=0))
                       for u, (b, h) in enumerate(units)]

        def rw_solve():
            us = [_mm(inv["tinv"][u], st["x"][u]) for u in range(len(units))]
            st["uv"] = [jnp.concatenate([us[u].astype(BF), st["vs"][u]], axis=0) for u in range(len(units))]

        def rw_out():
            st["ys"] = [st["hr"][u][c:2 * c, :] + _mm(ds[b]["rbk"][h], st["uv"][u])
                        for u, (b, h) in enumerate(units)]
            for u, (b, h) in enumerate(units):
                hrw_ref[b, h] = st["h0"][u] * ds[b]["gcol"][h] + _mm_tn(ds[b]["bgkg"][h], st["uv"][u])

        def rw_finish(k, b):
            def run():
                d = ds[b]
                y = jnp.concatenate(st["ys"][k * RW_HEADS:(k + 1) * RW_HEADS], axis=1)
                yc = y - seg_sum(y) * inv_hd
                yn = yc * lax.rsqrt(seg_sum(yc * yc) * inv_hd + RW_LNX_EPS) * lnw_ref[...] + lnb_ref[...]
                out = (yn + seg_sum(d["r"] * d["k2"] * rk_ref[...]) * d["v"]) * d["gate"]
                y_ref[b, :, 0:RW_W] = out.astype(y_ref.dtype)
            return run

        def gd_solve():
            n_rw = len(units)
            sols = [_mm(inv["tinv"][n_rw + u], ds[b]["gd"][h]["rhs"]) for u, (b, h) in enumerate(gunits)]
            st["s0"] = [sgd_ref[b, h] for b, h in gunits]
            st["v_new"] = [sols[u][:, 0:128] - _mm(sols[u][:, 128:256], st["s0"][u]) for u in range(len(gunits))]

        def gd_out(u, b, h):
            def run():
                g = ds[b]["gd"][h]
                s0, v_new = st["s0"][u], st["v_new"][u]
                o = _mm(g["q_in"], s0) + _mm(g["qk"], v_new)
                sgd_ref[b, h] = s0 * g["sd"] + _mm_tn(g["k_out"], v_new)
                og = p_ref[b, :, base + GDN_CONV_CH + h * 128:base + GDN_CONV_CH + (h + 1) * 128]
                out = _rms(o, ggain_ref[...]) * _silu(og)
                y_ref[b, :, RW_W + h * 128:RW_W + (h + 1) * 128] = out.astype(y_ref.dtype)
            return run

        return ([rw_read, rw_mix, rw_solve, rw_out] + [rw_finish(k, b) for k, b in enumerate(rows)]
                + [gd_solve] + [gd_out(u, b, h) for u, (b, h) in enumerate(gunits)])

    group = min(ODD_GROUP, nb)
    groups = [list(range(g, g + group)) for g in range(0, nb, group)]
    ds = [dict() for _ in range(nb)]
    invs = [dict() for _ in groups]
    stage1 = [_merge_pieces(*[prep_pieces(b, ds[b]) for b in rows]) for rows in groups]
    stage2 = [inverse_pieces(rows, ds, invs[g]) for g, rows in enumerate(groups)]
    stage3 = [state_pieces(rows, ds, invs[g]) for g, rows in enumerate(groups)]
    for slot in range(len(groups) + 2):
        active = []
        if slot < len(groups):
            active.append(stage1[slot])
        if 0 <= slot - 1 < len(groups):
            active.append(stage2[slot - 1])
        if 0 <= slot - 2 < len(groups):
            active.append(stage3[slot - 2])
        for piece in _merge_pieces(*active):
            piece()


def _odd_mixer(p, mu, w0, a0, wa2, g2, kks, kas, rk, lnw, lnb, conv_w, alog, dtb, ggain):
    b, lp, n = p.shape
    nc = lp // CHUNK
    nb = ODD_NB if b % ODD_NB == 0 else 1
    t = np.arange(CHUNK)
    tril = jnp.asarray(t[None, :] <= t[:, None], BF)
    l = np.arange(128)
    bones = jnp.asarray(l[:, None] // RW_HD == l[None, :] // RW_HD, BF)
    row = lambda a: a.reshape(1, -1)
    consts = [row(mu), row(w0), row(a0), wa2, g2, row(kks), row(kas), row(rk), row(lnw), row(lnb),
              conv_w, row(alog), row(dtb), row(ggain), tril, bones]
    const2 = lambda i, j: (0, 0)
    return pl.pallas_call(
        _odd_body,
        grid=(b // nb, nc),
        in_specs=[pl.BlockSpec((nb, CHUNK, n), lambda i, j: (i, j, 0))]
        + [pl.BlockSpec(a.shape, const2) for a in consts],
        out_specs=pl.BlockSpec((nb, CHUNK, RW_W + GDN_W), lambda i, j: (i, j, 0)),
        out_shape=jax.ShapeDtypeStruct((b, lp, RW_W + GDN_W), BF),
        scratch_shapes=[
            pltpu.VMEM((nb, RW_HEADS, RW_HD, RW_HD), F32),
            pltpu.VMEM((nb, GDN_HEADS, GDN_DK, GDN_DK), F32),
            pltpu.VMEM((nb, 8, RW_IN), F32),
            pltpu.VMEM((nb, 8, GDN_CONV_CH), F32),
        ],
        compiler_params=pltpu.CompilerParams(
            dimension_semantics=("arbitrary", "arbitrary"), vmem_limit_bytes=VMEM_LIMIT),
        name="odd_mixer",
    )(p, *consts)


def _row_tile(lp):
    best = None
    for tm in range(16, min(lp, 704) + 1, 16):
        if lp % tm == 0:
            best = tm
    assert best is not None, "sequence length must be a multiple of 16"
    return best


def kernel(x, meta_tokens, norm_gains, w_in_even, w_out_even, hg_lb_logits, hg_norm_gain, w_in_odd, w_out_odd, rw_mu, rw_w0, rw_w2, rw_a0, rw_a2, rw_g2, rw_kk_scale, rw_ka_scale, rw_rk, rw_lnx_w, rw_lnx_b, gdn_conv_w, gdn_a_log, gdn_dt_bias, gdn_norm_gain, ffn_w_up, ffn_conv_w, ffn_conv_b, ffn_w_down):
    bsz, seq, d = x.shape
    depth = norm_gains.shape[0]
    l = N_META + seq
    pad = (-l) % CHUNK
    lp = l + pad
    tm = _row_tile(lp)
    lead = pad + N_META

    half = RET_DK // 2
    pos = (jnp.arange(lp, dtype=jnp.int32) - pad).astype(F32)
    inv = ROPE_BASE ** (-jnp.arange(half, dtype=F32) / half)
    ang = pos[:, None] * inv[None, :]
    cos_t = jnp.concatenate([jnp.cos(ang), jnp.cos(ang)], axis=1)
    sin_t = jnp.concatenate([-jnp.sin(ang), jnp.sin(ang)], axis=1)

    h = None
    for layer in range(depth):
        g = norm_gains[layer]
        i = layer // 2
        if layer % 2 == 0:
            w_in = w_in_even[i].astype(BF)
        else:
            w_in = jnp.pad(w_in_odd[i], ((0, 0), (0, ODD_IN_PAD - w_in_odd.shape[2]))).astype(BF)
        if layer == 0:
            p, h = _embed_proj(x, meta_tokens.astype(x.dtype), g[0], w_in, tm, lead)
        else:
            p = _norm_proj(h, g[0], w_in, tm)
        if layer % 2 == 0:
            y = _even_mixer(p, cos_t, sin_t, hg_lb_logits, hg_norm_gain[i], layer)
            w_out = w_out_even[i]
        else:
            wa2 = jnp.zeros((RW_DECAY_LORA + RW_AAA_LORA, 2 * RW_W), F32)
            wa2 = wa2.at[:RW_DECAY_LORA, :RW_W].set(rw_w2[i]).at[RW_DECAY_LORA:, RW_W:].set(rw_a2[i])
            lane_pad = lambda a: jnp.pad(a, (0, LANES - a.shape[0]))
            y = _odd_mixer(p, rw_mu[i], rw_w0[i], rw_a0[i], wa2.astype(BF), rw_g2[i].astype(BF),
                           rw_kk_scale[i], rw_ka_scale[i], rw_rk[i].reshape(-1), rw_lnx_w[i], rw_lnx_b[i],
                           gdn_conv_w[i], lane_pad(gdn_a_log[i]), lane_pad(gdn_dt_bias[i]), gdn_norm_gain[i])
            w_out = w_out_odd[i]
        ffn_args = (y, w_out.astype(BF), g[1], h, g[2], ffn_w_up[layer].astype(BF), ffn_conv_w[layer],
                    ffn_conv_b[layer], ffn_w_down[layer].astype(BF), g[3])
        if layer + 1 < depth:
            h = _out_ffn(*ffn_args, tm, pad)
        else:
            h = _out_ffn_tail(*ffn_args, _row_tile(seq), lead)
    return h
```

```python
import functools
import math

import numpy as np
import jax
import jax.numpy as jnp
from jax import lax
from jax.experimental import pallas as pl
from jax.experimental.pallas import tpu as pltpu

F32 = jnp.float32
BF = jnp.bfloat16

D_MODEL = 1024
CHUNK = 64
N_META = 16
EPS = 1e-6
ROPE_BASE = 10000.0

RET_HEADS = D_MODEL // 256
RET_DK = 128
HG_HEADS = D_MODEL // 256
HG_DK = 128
RW_HEADS = D_MODEL // 128
RW_HD = 64
RW_DECAY_LORA = 64
RW_AAA_LORA = 64
RW_GATE_LORA = 128
RW_LNX_EPS = 64e-5
GDN_HEADS = D_MODEL // 256
GDN_DK = 128
GDN_CONV = 4
D_FF = 128 * ((8 * D_MODEL // 3 + 127) // 128)
FFN_CONV = 3

RET_W = RET_HEADS * RET_DK
HG_W = HG_HEADS * HG_DK
RW_W = RW_HEADS * RW_HD
RW_IN = 3 * RW_W + RW_DECAY_LORA + RW_AAA_LORA + RW_GATE_LORA
GDN_W = GDN_HEADS * GDN_DK
GDN_CONV_CH = 3 * GDN_W
GDN_IN = GDN_CONV_CH + GDN_W + 2 * GDN_HEADS
LANES = 128
GDN_IN_PAD = LANES * ((GDN_IN + LANES - 1) // LANES)
ODD_IN_PAD = RW_IN + GDN_IN_PAD
HG_LEVELS = (32, 16, 8, 4, 2, 1)
HG_ROW_SPLIT = 8
VMEM_LIMIT = 56 * 1024 * 1024
FFN_TILE = 256
EVEN_NB = 8
ODD_NB = 4
ODD_GROUP = 2


def _dot(a, b):
    return jnp.dot(a, b, preferred_element_type=F32)


def _mm(a, b):
    return _dot(a.astype(BF), b.astype(BF))


def _mm_nt(a, b):
    return lax.dot_general(a.astype(BF), b.astype(BF), (((1,), (1,)), ((), ())), preferred_element_type=F32)


def _mm_tn(a, b):
    return lax.dot_general(a.astype(BF), b.astype(BF), (((0,), (0,)), ((), ())), preferred_element_type=F32)


def _split2(x):
    hi = x.astype(BF)
    lo = (x - hi.astype(F32)).astype(BF)
    return hi, lo


def _split3(x):
    hi = x.astype(BF)
    r1 = x - hi.astype(F32)
    mid = r1.astype(BF)
    lo = (r1 - mid.astype(F32)).astype(BF)
    return hi, mid, lo


def _cmm3(c, x):
    hi, mid, lo = _split3(x)
    return _dot(c, hi) + _dot(c, mid) + _dot(c, lo)


def _cmm2(c, x):
    hi, lo = _split2(x)
    return _dot(c, hi) + _dot(c, lo)


def _sigmoid(x):
    return 1.0 / (1.0 + jnp.exp(-x))


def _silu(x):
    return x * _sigmoid(x)


def _softplus(x):
    return jnp.maximum(x, 0.0) + jnp.log1p(jnp.exp(-jnp.abs(x)))


def _rms(x, g):
    return x * lax.rsqrt(jnp.mean(x * x, axis=-1, keepdims=True) + EPS) * g


def _norm_proj_body(h_ref, g_ref, w_ref, o_ref):
    u = _rms(h_ref[...], g_ref[...])
    o_ref[...] = _dot(u.astype(BF), w_ref[...])


def _norm_proj(h, gain, w, tm):
    b, lp, d = h.shape
    n = w.shape[1]
    return pl.pallas_call(
        _norm_proj_body,
        grid=(b, lp // tm),
        in_specs=[
            pl.BlockSpec((None, tm, d), lambda i, j: (i, j, 0)),
            pl.BlockSpec((1, d), lambda i, j: (0, 0)),
            pl.BlockSpec((d, n), lambda i, j: (0, 0), pipeline_mode=pl.Buffered(1)),
        ],
        out_specs=pl.BlockSpec((None, tm, n), lambda i, j: (i, j, 0)),
        out_shape=jax.ShapeDtypeStruct((b, lp, n), F32),
        compiler_params=pltpu.CompilerParams(
            dimension_semantics=("arbitrary", "arbitrary"), vmem_limit_bytes=VMEM_LIMIT),
        name="norm_proj",
    )(h, gain.reshape(1, d), w)


def _embed_proj_body(x_ref, meta_ref, g_ref, w_ref, p_ref, h_ref, *, lead):
    xb = x_ref[0]
    tm, d = xb.shape
    n_zero = lead - meta_ref.shape[0]
    first = jnp.concatenate([jnp.zeros((n_zero, d), F32), meta_ref[...], xb[0:tm - lead, :]], axis=0)
    blk = jnp.where(pl.program_id(1) == 0, first, xb)
    h_ref[...] = blk
    p_ref[...] = _dot(_rms(blk, g_ref[...]).astype(BF), w_ref[...])


def _embed_proj(x, meta, gain, w, tm, lead):
    b, s, d = x.shape
    lp = s + lead
    n = w.shape[1]
    assert lead % 8 == 0 and meta.shape[0] % 8 == 0 and lead < tm and lp % tm == 0
    x_rows = lambda i, j: (i, pl.multiple_of(jnp.maximum(tm * j - lead, 0), 8), 0)
    return pl.pallas_call(
        functools.partial(_embed_proj_body, lead=lead),
        grid=(b, lp // tm),
        in_specs=[
            pl.BlockSpec((pl.Element(1), pl.Element(tm), pl.Element(d)), x_rows),
            pl.BlockSpec(meta.shape, lambda i, j: (0, 0)),
            pl.BlockSpec((1, d), lambda i, j: (0, 0)),
            pl.BlockSpec((d, n), lambda i, j: (0, 0), pipeline_mode=pl.Buffered(1)),
        ],
        out_specs=[pl.BlockSpec((None, tm, n), lambda i, j: (i, j, 0)),
                   pl.BlockSpec((None, tm, d), lambda i, j: (i, j, 0))],
        out_shape=[jax.ShapeDtypeStruct((b, lp, n), F32), jax.ShapeDtypeStruct((b, lp, d), F32)],
        compiler_params=pltpu.CompilerParams(
            dimension_semantics=("arbitrary", "arbitrary"), vmem_limit_bytes=VMEM_LIMIT),
        name="embed_proj",
    )(x, meta, gain.reshape(1, d), w)


def _shift_rows(a, carry_ref, cols, n_back, row):
    out = []
    for s in range(1, n_back + 1):
        sh = pltpu.roll(a, s, 0)
        for r in range(s):
            sh = jnp.where(row == r, carry_ref[8 - s + r:9 - s + r, cols], sh)
        out.append(sh)
    return out


def _merge_pieces(*lists):
    total = max(len(l) for l in lists)
    keyed = []
    for li, l in enumerate(lists):
        for k, piece in enumerate(l):
            keyed.append(((k + 0.5) * total / len(l), li, k, piece))
    keyed.sort(key=lambda t: t[:3])
    return [t[3] for t in keyed]


def _ffn_pieces(make_x, shift, g2_ref, wup_ref, cw_ref, cb_ref, wdn_ref, g3_ref, emit):
    n_tiles = D_FF // FFN_TILE
    st = {}

    def up(c):
        return [_dot(st["u"], wup_ref[:, part * D_FF + c * FFN_TILE:part * D_FF + (c + 1) * FFN_TILE])
                for part in range(2)]

    def head():
        st["x"] = make_x()
        st["u"] = _rms(st["x"], g2_ref[...]).astype(BF)
        st["acts"] = []

    def lead_in():
        st["pending"] = [up(c) for c in range(min(2, n_tiles))]

    def tile(c):
        def run():
            cur = st["pending"].pop(0)
            if c + 2 < n_tiles:
                st["pending"].append(up(c + 2))
            z = []
            for part in range(2):
                cols = slice(part * D_FF + c * FFN_TILE, part * D_FF + (c + 1) * FFN_TILE)
                a = cur[part]
                am1, am2 = shift(a, cols)
                z.append(am2 * cw_ref[0:1, cols] + am1 * cw_ref[1:2, cols] + a * cw_ref[2:3, cols] + cb_ref[:, cols])
            st["acts"].append((_silu(z[0]) * z[1]).astype(BF))
        return run

    def tail():
        acc = _dot(jnp.concatenate(st["acts"], axis=1), wdn_ref[...])
        emit(st["x"] + _rms(acc, g3_ref[...]))

    return [head, lead_in] + [tile(c) for c in range(n_tiles)] + [tail]


def _ffn_body(y_ref, wout_ref, g1_ref, h_ref, g2_ref, wup_ref, cw_ref, cb_ref, wdn_ref, g3_ref, o_ref, carry_ref,
              *, tm, pad):
    @pl.when(pl.program_id(1) == 0)
    def _():
        carry_ref[...] = jnp.zeros_like(carry_ref)

    half = tm // 2
    lists = []
    for r0 in (0, half):
        rows = slice(r0, r0 + half)
        t = pl.program_id(1) * tm + r0 + lax.broadcasted_iota(jnp.int32, (half, D_MODEL), 0)
        row = lax.broadcasted_iota(jnp.int32, (half, FFN_TILE), 0)

        def make_x(rows=rows, t=t):
            return jnp.where(t >= pad, h_ref[rows, :] + _rms(_dot(y_ref[rows, :], wout_ref[...]), g1_ref[...]), 0.0)

        def shift(a, cols, row=row):
            am1, am2 = _shift_rows(a, carry_ref, cols, FFN_CONV - 1, row)
            carry_ref[:, cols] = a[half - 8:half, :]
            return am1, am2

        def emit(out, rows=rows, t=t):
            o_ref[rows, :] = jnp.where(t >= pad, out, 0.0)

        lists.append(_ffn_pieces(make_x, shift, g2_ref, wup_ref, cw_ref, cb_ref, wdn_ref, g3_ref, emit))
    for piece in _merge_pieces(*lists):
        piece()


def _ffn_tail_body(y_ref, wout_ref, g1_ref, h_ref, g2_ref, wup_ref, cw_ref, cb_ref, wdn_ref, g3_ref, o_ref, *,
                   halo):
    half = o_ref.shape[0] // 2
    shift = lambda a, cols: (pltpu.roll(a, 1, 0), pltpu.roll(a, 2, 0))
    lists = []
    for r0 in (0, half):
        def make_x(r0=r0):
            rows = slice(r0, r0 + half + halo)
            return h_ref[0, rows, :] + _rms(_dot(y_ref[0, rows, :], wout_ref[...]), g1_ref[...])

        def emit(out, r0=r0):
            o_ref[r0:r0 + half, :] = out[halo:, :]

        lists.append(_ffn_pieces(make_x, shift, g2_ref, wup_ref, cw_ref, cb_ref, wdn_ref, g3_ref, emit))
    for piece in _merge_pieces(*lists):
        piece()


def _ffn_weight_specs(k, d, ff2):
    const = lambda i, j: (0, 0)
    return dict(
        w_out=pl.BlockSpec((k, d), const, pipeline_mode=pl.Buffered(1)), gain=pl.BlockSpec((1, d), const),
        w_up=pl.BlockSpec((d, ff2), const, pipeline_mode=pl.Buffered(1)),
        conv_w=pl.BlockSpec((FFN_CONV, ff2), const), conv_b=pl.BlockSpec((1, ff2), const),
        w_down=pl.BlockSpec((ff2 // 2, d), const, pipeline_mode=pl.Buffered(1)))


def _out_ffn(y, w_out, g1, h, g2, w_up, conv_w, conv_b, w_down, g3, tm, pad):
    b, lp, d = h.shape
    k = y.shape[2]
    ff2 = w_up.shape[1]
    assert lp % tm == 0 and tm % 16 == 0
    ws = _ffn_weight_specs(k, d, ff2)
    rows = lambda width: pl.BlockSpec((None, tm, width), lambda i, j: (i, j, 0))
    return pl.pallas_call(
        functools.partial(_ffn_body, tm=tm, pad=pad),
        grid=(b, lp // tm),
        in_specs=[rows(k), ws["w_out"], ws["gain"], rows(d), ws["gain"], ws["w_up"], ws["conv_w"], ws["conv_b"],
                  ws["w_down"], ws["gain"]],
        out_specs=rows(d),
        out_shape=jax.ShapeDtypeStruct((b, lp, d), F32),
        scratch_shapes=[pltpu.VMEM((8, ff2), F32)],
        compiler_params=pltpu.CompilerParams(
            dimension_semantics=("arbitrary", "arbitrary"), vmem_limit_bytes=VMEM_LIMIT),
        name="conv_ffn",
    )(y, w_out, g1.reshape(1, d), h, g2.reshape(1, d), w_up, conv_w, conv_b.reshape(1, ff2), w_down,
      g3.reshape(1, d))


def _out_ffn_tail(y, w_out, g1, h, g2, w_up, conv_w, conv_b, w_down, g3, tm, lead):
    b, lp, d = h.shape
    seq = lp - lead
    k = y.shape[2]
    ff2 = w_up.shape[1]
    halo = 8
    assert seq % tm == 0 and tm % 16 == 0 and lead >= halo and lead % 8 == 0
    ws = _ffn_weight_specs(k, d, ff2)
    rows = lambda width: pl.BlockSpec(
        (pl.Element(1), pl.Element(tm + halo), pl.Element(width)),
        lambda i, j: (i, pl.multiple_of(lead - halo + tm * j, 8), 0))
    return pl.pallas_call(
        functools.partial(_ffn_tail_body, halo=halo),
        grid=(b, seq // tm),
        in_specs=[rows(k), ws["w_out"], ws["gain"], rows(d), ws["gain"], ws["w_up"], ws["conv_w"], ws["conv_b"],
                  ws["w_down"], ws["gain"]],
        out_specs=pl.BlockSpec((None, tm, d), lambda i, j: (i, j, 0)),
        out_shape=jax.ShapeDtypeStruct((b, seq, d), F32),
        compiler_params=pltpu.CompilerParams(
            dimension_semantics=("arbitrary", "arbitrary"), vmem_limit_bytes=VMEM_LIMIT),
        name="conv_ffn_tail",
    )(y, w_out, g1.reshape(1, d), h, g2.reshape(1, d), w_up, conv_w, conv_b.reshape(1, ff2), w_down,
      g3.reshape(1, d))


def _hg_consts():
    c = CHUNK
    t = np.arange(c)
    small = [s for s in HG_LEVELS if s < HG_ROW_SPLIT]
    m = np.zeros((1 + len(small), c, c), np.float32)
    m[0] = t[None, :] <= t[:, None]
    masks = np.zeros((len(HG_LEVELS) + 1, c, c), np.float32)
    ii, jj = t[:, None], t[None, :]
    for li, s in enumerate(HG_LEVELS):
        if s < HG_ROW_SPLIT:
            for i in range(c):
                mid = (i // (2 * s)) * (2 * s) + s - 1
                if i % (2 * s) >= s:
                    m[1 + small.index(s), i, mid + 1:i + 1] = 1.0
                else:
                    m[1 + small.index(s), i, i + 1:mid + 1] = 1.0
        masks[li] = (ii // (2 * s) == jj // (2 * s)) & (ii % (2 * s) >= s) & (jj % (2 * s) < s)
    masks[-1] = ii == jj
    return m.reshape(-1, c), masks


def _even_body(p_ref, cos_ref, sin_ref, lbl_ref, hgain_ref, mall_ref, masks_ref, y_ref, sret_ref, shg_ref, *,
               layer):
    @pl.when(pl.program_id(1) == 0)
    def _():
        sret_ref[...] = jnp.zeros_like(sret_ref)
        shg_ref[...] = jnp.zeros_like(shg_ref)

    c = CHUNK
    cosv = cos_ref[...]
    sinv = sin_ref[...]
    ii = lax.broadcasted_iota(jnp.int32, (c, c), 0)
    jj = lax.broadcasted_iota(jnp.int32, (c, c), 1)
    diff = (ii - jj).astype(F32)
    rowi = lax.broadcasted_iota(jnp.int32, (c, RET_DK), 0).astype(F32)

    nb = p_ref.shape[0]
    col = lambda base, h: slice(base + h * 128, base + (h + 1) * 128)
    units = [(b, h) for b in range(nb) for h in range(RET_HEADS)]
    nu = len(units)
    lgs = [math.log1p(-(2.0 ** (-5.0 - h))) for _, h in units]
    qs = [p_ref[b, :, col(0, h)] for b, h in units]
    ks = [p_ref[b, :, col(RET_W, h)] for b, h in units]
    vs = [p_ref[b, :, col(2 * RET_W, h)] for b, h in units]
    qs = [q * cosv + pltpu.roll(q, RET_DK // 2, 1) * sinv for q in qs]
    ks = [(k * cosv + pltpu.roll(k, RET_DK // 2, 1) * sinv) * (RET_DK ** -0.5) for k in ks]
    ss = [sret_ref[b, h] for b, h in units]
    att = [_mm_nt(qs[u], ks[u]) * jnp.where(diff >= 0, jnp.exp(lgs[u] * jnp.maximum(diff, 0.0)), 0.0)
           for u in range(nu)]
    inter = [_mm(qs[u] * jnp.exp(lgs[u] * (rowi + 1.0)), ss[u]) for u in range(nu)]
    kv = [_mm_tn(ks[u] * jnp.exp(lgs[u] * (c - 1.0 - rowi)), vs[u]) for u in range(nu)]
    outs = [_mm(att[u], vs[u]) + inter[u] for u in range(nu)]
    for u, (b, h) in enumerate(units):
        sret_ref[b, h] = ss[u] * math.exp(lgs[u] * c) + kv[u]
        o = outs[u]
        xc = o - jnp.mean(o, axis=-1, keepdims=True)
        oa = xc * lax.rsqrt(jnp.mean(xc * xc, axis=-1, keepdims=True) + EPS)
        y_ref[b, :, col(0, h)] = (oa * _silu(p_ref[b, :, col(3 * RET_W, h)])).astype(y_ref.dtype)

    lgt = lbl_ref[...]
    mx = jnp.max(lgt, axis=0, keepdims=True)
    ex = jnp.exp(lgt - mx)
    sm = ex / jnp.sum(ex, axis=0, keepdims=True)
    lb_all = jnp.sum(sm[0:layer + 1, :], axis=0, keepdims=True)
    base = 4 * RET_W
    mall = mall_ref[...]
    nl = len(HG_LEVELS)
    units = [(b, h) for b in range(nb) for h in range(HG_HEADS)]
    nu = len(units)
    lbs = [lb_all[:, col(0, h)] for _, h in units]
    qs = [p_ref[b, :, col(base, h)] for b, h in units]
    fbs = [p_ref[b, :, col(base + HG_W, h)] for b, h in units]
    vs = [p_ref[b, :, col(base + 2 * HG_W, h)] for b, h in units]
    ks = [(1.0 - lbs[u]) * _sigmoid(-fbs[u]) for u in range(nu)]
    es = [_cmm2(mall, jnp.log(lbs[u] + (1.0 - lbs[u]) * _sigmoid(fbs[u]))) for u in range(nu)]
    xs = [jnp.exp(e) for e in es]
    rest = [jnp.exp(e[c - 1:c, :] - e[0:c, :]) for e in es]

    def level_factor(u, li):
        s = HG_LEVELS[li]
        if s < HG_ROW_SPLIT:
            k = 1 + [t for t in HG_LEVELS if t < HG_ROW_SPLIT].index(s)
            return xs[u][k * c:(k + 1) * c, :]
        cb = es[u][0:c, :]
        parts = []
        for p0 in range(0, c, 2 * s):
            mid = cb[p0 + s - 1:p0 + s, :]
            parts += [mid - cb[p0:p0 + s, :], cb[p0 + s:p0 + 2 * s, :] - mid]
        return jnp.exp(jnp.concatenate(parts, axis=0))

    sts = [shg_ref[b, h] for b, h in units]
    att = [masks_ref[nl] * _mm_nt(qs[u], ks[u]) for u in range(nu)]
    for li in range(nl):
        fac = [level_factor(u, li) for u in range(nu)]
        att = [att[u] + masks_ref[li] * _mm_nt(qs[u] * fac[u], ks[u] * fac[u]) for u in range(nu)]
    inter = [_mm_nt(qs[u] * xs[u][0:c, :], sts[u]) for u in range(nu)]
    kv = [_mm_tn(vs[u], ks[u] * rest[u]) for u in range(nu)]
    outs = [_mm(att[u], vs[u]) + inter[u] for u in range(nu)]
    for u, (b, h) in enumerate(units):
        shg_ref[b, h] = sts[u] * xs[u][c - 1:c, :] + kv[u]
        ob = _rms(outs[u], hgain_ref[:, col(0, h)])
        y_ref[b, :, col(RET_W, h)] = (ob * _silu(p_ref[b, :, col(base + 3 * HG_W, h)])).astype(y_ref.dtype)


def _even_mixer(p, cos_t, sin_t, lb_logits, hg_gain, layer):
    b, lp, n = p.shape
    nc = lp // CHUNK
    nb = EVEN_NB if b % EVEN_NB == 0 else 1
    mall, masks = _hg_consts()
    const2 = lambda i, j: (0, 0)
    return pl.pallas_call(
        functools.partial(_even_body, layer=layer),
        grid=(b // nb, nc),
        in_specs=[
            pl.BlockSpec((nb, CHUNK, n), lambda i, j: (i, j, 0)),
            pl.BlockSpec((CHUNK, RET_DK), lambda i, j: (j, 0)),
            pl.BlockSpec((CHUNK, RET_DK), lambda i, j: (j, 0)),
            pl.BlockSpec(lb_logits.shape, const2),
            pl.BlockSpec((1, HG_W), const2),
            pl.BlockSpec(mall.shape, const2),
            pl.BlockSpec(masks.shape, lambda i, j: (0, 0, 0)),
        ],
        out_specs=pl.BlockSpec((nb, CHUNK, RET_W + HG_W), lambda i, j: (i, j, 0)),
        out_shape=jax.ShapeDtypeStruct((b, lp, RET_W + HG_W), BF),
        scratch_shapes=[pltpu.VMEM((nb, RET_HEADS, RET_DK, RET_DK), F32),
                        pltpu.VMEM((nb, HG_HEADS, HG_DK, HG_DK), F32)],
        compiler_params=pltpu.CompilerParams(
            dimension_semantics=("arbitrary", "arbitrary"), vmem_limit_bytes=VMEM_LIMIT),
        name="even_mixer",
    )(p, cos_t, sin_t, lb_logits, hg_gain.reshape(1, HG_W), jnp.asarray(mall, BF), jnp.asarray(masks))


def _odd_body(p_ref, mu_ref, w0_ref, a0_ref, wa2_ref, g2_ref, kks_ref, kas_ref, rk_ref, lnw_ref, lnb_ref,
              cw_ref, alog_ref, dtb_ref, ggain_ref, tril_ref, bones_ref, y_ref,
              hrw_ref, sgd_ref, cpc_ref, ccv_ref):
    @pl.when(pl.program_id(1) == 0)
    def _():
        hrw_ref[...] = jnp.zeros_like(hrw_ref)
        sgd_ref[...] = jnp.zeros_like(sgd_ref)
        cpc_ref[...] = jnp.zeros_like(cpc_ref)
        ccv_ref[...] = jnp.zeros_like(ccv_ref)

    nb = p_ref.shape[0]
    c = CHUNK
    tril = tril_ref[...]
    bones = bones_ref[...]
    ii = lax.broadcasted_iota(jnp.int32, (c, c), 0)
    jj = lax.broadcasted_iota(jnp.int32, (c, c), 1)
    strict = ii > jj
    incl = ii >= jj
    eye = ii == jj
    eye_f = eye.astype(F32)
    ii2 = lax.broadcasted_iota(jnp.int32, (c, 2 * c), 0)
    lane2 = lax.broadcasted_iota(jnp.int32, (c, 2 * c), 1)
    jj2 = jnp.bitwise_and(lane2, c - 1)
    ak_mask = (ii2 > jj2) & (lane2 >= c)
    rbk_mask = ii2 >= jj2
    lane = lax.broadcasted_iota(jnp.int32, (c, 128), 1)
    heads = [slice(h * RW_HD, (h + 1) * RW_HD) for h in range(RW_HEADS)]
    base = RW_IN
    inv_hd = 1.0 / RW_HD

    def seg_sum(x):
        return jnp.concatenate(
            [_mm(x[:, p * 128:(p + 1) * 128], bones) for p in range(x.shape[1] // 128)], axis=1)

    def prep_pieces(b, d):
        def shift():
            pc = p_ref[b, :, 0:RW_IN]
            row = lax.broadcasted_iota(jnp.int32, (c, RW_IN), 0)
            prev = jnp.where(row == 0, cpc_ref[b, 7:8, :], pltpu.roll(pc, 1, 0))
            cpc_ref[b] = pc[c - 8:c, :]
            pcs = pc + (prev - pc) * mu_ref[...]
            d.update(r=pcs[:, 0:RW_W], k=pcs[:, RW_W:2 * RW_W], v=pcs[:, 2 * RW_W:3 * RW_W],
                     lo=pcs[:, 3 * RW_W:3 * RW_W + 128], glo=pcs[:, 3 * RW_W + 128:RW_IN])

        def lora():
            lo = d["lo"]
            wa = _mm(jnp.where(lane < RW_DECAY_LORA, jnp.tanh(lo), lo), wa2_ref[...])
            log_w = -_softplus(-(w0_ref[...] + wa[:, 0:RW_W])) - 0.5
            d["lw"] = -jnp.exp(log_w)
            d["a"] = _sigmoid(a0_ref[...] + wa[:, RW_W:2 * RW_W])
            d["gate"] = _mm(_sigmoid(d["glo"]), g2_ref[...])

        def keys():
            kks = d["k"] * kks_ref[...]
            kkn = kks * lax.rsqrt(seg_sum(kks * kks) + EPS)
            d["k2"] = d["k"] * (1.0 + (d["a"] - 1.0) * kas_ref[...])
            d["kkn"] = kkn
            d["beta"] = kkn * d["a"]
            d["c_inc"] = _cmm2(tril, d["lw"])

        def decays():
            c_inc, k2, beta = d["c_inc"], d["k2"], d["beta"]
            c_last = c_inc[c - 1:c, :]
            e_neg = jnp.exp(-c_inc)
            e_rest = jnp.exp(c_last - c_inc)
            d.update(ah=-d["kkn"] * jnp.exp(c_inc - d["lw"]), rh=d["r"] * jnp.exp(c_inc), bt=beta * e_neg,
                     kt=k2 * e_neg, bg=beta * e_rest, kg=k2 * e_rest, gc=jnp.exp(c_last),
                     ar=[], ls=[], ak=[], rbk=[], gcol=[], bgkg=[])

        def gram(h):
            def run():
                sl = heads[h]
                ar = jnp.concatenate([d["ah"][:, sl], d["rh"][:, sl]], axis=0).astype(BF)
                gm = _mm_nt(ar, jnp.concatenate([d["bt"][:, sl], d["kt"][:, sl]], axis=0))
                d["ar"].append(ar)
                d["ls"].append(jnp.where(strict, gm[0:c, 0:c], 0.0))
                d["ak"].append(jnp.where(ak_mask, gm[0:c, :], 0.0).astype(BF))
                d["rbk"].append(jnp.where(rbk_mask, gm[c:2 * c, :], 0.0).astype(BF))
                d["gcol"].append(jnp.sum(eye_f * d["gc"][:, sl], axis=1, keepdims=True))
                d["bgkg"].append(jnp.concatenate([d["bg"][:, sl], d["kg"][:, sl]], axis=0).astype(BF))
            return run

        def conv():
            x = p_ref[b, :, base:base + GDN_CONV_CH]
            rowc = lax.broadcasted_iota(jnp.int32, (c, GDN_CONV_CH), 0)
            xm1, xm2, xm3 = _shift_rows(x, ccv_ref.at[b], slice(None), GDN_CONV - 1, rowc)
            ccv_ref[b] = x[c - 8:c, :]
            d["qkv"] = _silu(xm3 * cw_ref[0:1, :] + xm2 * cw_ref[1:2, :] + xm1 * cw_ref[2:3, :] + x * cw_ref[3:4, :])

        def gates():
            sc = p_ref[b, :, base + GDN_CONV_CH + GDN_W:base + GDN_IN_PAD]
            g_all = -jnp.exp(alog_ref[...]) * _softplus(sc + dtb_ref[...])
            d["b_all"] = _sigmoid(sc)
            cg = _cmm3(tril, g_all)
            d["cg_all"] = jnp.concatenate(
                [jnp.broadcast_to(cg[:, h:h + 1], (c, 128)) for h in range(GDN_HEADS)], axis=1)
            d["gd"] = []

        def gdn_head(h):
            def run():
                hs = slice(h * 128, (h + 1) * 128)
                qkv = d["qkv"]
                q = qkv[:, hs]
                kd = qkv[:, GDN_W + h * 128:GDN_W + (h + 1) * 128]
                vd = qkv[:, 2 * GDN_W + h * 128:2 * GDN_W + (h + 1) * 128]
                q = q * lax.rsqrt(jnp.sum(q * q, axis=-1, keepdims=True) + EPS) * (GDN_DK ** -0.5)
                kd = kd * lax.rsqrt(jnp.sum(kd * kd, axis=-1, keepdims=True) + EPS)
                b_b = jnp.broadcast_to(d["b_all"][:, GDN_HEADS + h:GDN_HEADS + h + 1], (c, 128))
                cg = d["cg_all"][:, hs]
                cg_row = jnp.sum(jnp.where(eye, cg[:, 0:c], 0.0), axis=0, keepdims=True)
                decay = jnp.exp(jnp.where(incl, cg[:, 0:c] - cg_row, -jnp.inf))
                eg = jnp.exp(cg)
                cl = cg[c - 1:c, :]
                d["gd"].append(dict(
                    l=-jnp.where(strict, b_b[:, 0:c] * _mm_nt(kd, kd) * decay, 0.0),
                    rhs=jnp.concatenate([b_b * vd, b_b * kd * eg], axis=1).astype(BF),
                    qk=(_mm_nt(q, kd) * decay).astype(BF), q_in=(q * eg).astype(BF),
                    k_out=(kd * jnp.exp(cl - cg)).astype(BF), sd=jnp.exp(cl)))
            return run

        return ([shift, lora, keys, decays] + [gram(h) for h in range(RW_HEADS)]
                + [conv, gates] + [gdn_head(h) for h in range(GDN_HEADS)])

    def inverse_pieces(rows, ds, out):
        st = {}

        def start():
            ls = [l for b in rows for l in ds[b]["ls"]] + [g["l"] for b in rows for g in ds[b]["gd"]]
            st["p"] = [eye_f + l for l in ls]
            st["l"] = [l.astype(BF) for l in ls]

        def square():
            st["l"] = [_mm(l, l).astype(BF) for l in st["l"]]

        def extend():
            st["p"] = [p + _mm(l, p) for l, p in zip(st["l"], st["p"])]

        def finish():
            out["tinv"] = st["p"]

        levels = []
        span = 2
        while span < c:
            levels += [square, extend]
            span *= 2
        return [start] + levels + [finish]

    def state_pieces(rows, ds, inv):
        units = [(b, h) for b in rows for h in range(RW_HEADS)]
        gunits = [(b, h) for b in rows for h in range(GDN_HEADS)]
        st = {}

        def rw_read():
            st["h0"] = [hrw_ref[b, h] for b, h in units]
            st["hr"] = [_mm(ds[b]["ar"][h], st["h0"][u]) for u, (b, h) in enumerate(units)]
            st["vs"] = [ds[b]["v"][:, heads[h]].astype(BF) for b, h in units]

        def rw_mix():
            vs = st["vs"]
            st["x"] = [st["hr"][u][0:c, :] + _mm(ds[b]["ak"][h], jnp.concatenate([vs[u], vs[u]], axis=0))
                       for u, (b, h) in enumerate(units)]

        def rw_solve():
            us = [_mm(inv["tinv"][u], st["x"][u]) for u in range(len(units))]
            st["uv"] = [jnp.concatenate([us[u].astype(BF), st["vs"][u]], axis=0) for u in range(len(units))]

        def rw_out():
            st["ys"] = [st["hr"][u][c:2 * c, :] + _mm(ds[b]["rbk"][h], st["uv"][u])
                        for u, (b, h) in enumerate(units)]
            for u, (b, h) in enumerate(units):
                hrw_ref[b, h] = st["h0"][u] * ds[b]["gcol"][h] + _mm_tn(ds[b]["bgkg"][h], st["uv"][u])

        def rw_center():
            st["y"] = [jnp.concatenate(st["ys"][k * RW_HEADS:(k + 1) * RW_HEADS], axis=1) for k in range(len(rows))]
            st["mean"] = [seg_sum(y) * inv_hd for y in st["y"]]
            st["bonus"] = [seg_sum(ds[b]["r"] * ds[b]["k2"] * rk_ref[...]) for b in rows]

        def rw_scale():
            st["yc"] = [y - m for y, m in zip(st["y"], st["mean"])]
            st["var"] = [seg_sum(yc * yc) * inv_hd for yc in st["yc"]]

        def rw_finish():
            for k, b in enumerate(rows):
                d = ds[b]
                yn = st["yc"][k] * lax.rsqrt(st["var"][k] + RW_LNX_EPS) * lnw_ref[...] + lnb_ref[...]
                out = (yn + st["bonus"][k] * d["v"]) * d["gate"]
                y_ref[b, :, 0:RW_W] = out.astype(y_ref.dtype)

        def gd_solve():
            n_rw = len(units)
            sols = [_mm(inv["tinv"][n_rw + u], ds[b]["gd"][h]["rhs"]) for u, (b, h) in enumerate(gunits)]
            st["s0"] = [sgd_ref[b, h] for b, h in gunits]
            st["v_new"] = [sols[u][:, 0:128] - _mm(sols[u][:, 128:256], st["s0"][u]) for u in range(len(gunits))]

        def gd_out(u, b, h):
            def run():
                g = ds[b]["gd"][h]
                s0, v_new = st["s0"][u], st["v_new"][u]
                o = _mm(g["q_in"], s0) + _mm(g["qk"], v_new)
                sgd_ref[b, h] = s0 * g["sd"] + _mm_tn(g["k_out"], v_new)
                og = p_ref[b, :, base + GDN_CONV_CH + h * 128:base + GDN_CONV_CH + (h + 1) * 128]
                out = _rms(o, ggain_ref[...]) * _silu(og)
                y_ref[b, :, RW_W + h * 128:RW_W + (h + 1) * 128] = out.astype(y_ref.dtype)
            return run

        gd = [gd_solve] + [gd_out(u, b, h) for u, (b, h) in enumerate(gunits)]
        cut = len(gd) // 2
        return [rw_read, rw_mix, rw_solve, rw_out, rw_center] + gd[:cut] + [rw_scale] + gd[cut:] + [rw_finish]

    group = min(ODD_GROUP, nb)
    groups = [list(range(g, g + group)) for g in range(0, nb, group)]
    ds = [dict() for _ in range(nb)]
    invs = [dict() for _ in groups]
    stage1 = [_merge_pieces(*[prep_pieces(b, ds[b]) for b in rows]) for rows in groups]
    stage2 = [inverse_pieces(rows, ds, invs[g]) for g, rows in enumerate(groups)]
    stage3 = [state_pieces(rows, ds, invs[g]) for g, rows in enumerate(groups)]
    for slot in range(len(groups) + 2):
        active = []
        if slot < len(groups):
            active.append(stage1[slot])
        if 0 <= slot - 1 < len(groups):
            active.append(stage2[slot - 1])
        if 0 <= slot - 2 < len(groups):
            active.append(stage3[slot - 2])
        for piece in _merge_pieces(*active):
            piece()


def _odd_mixer(p, mu, w0, a0, wa2, g2, kks, kas, rk, lnw, lnb, conv_w, alog, dtb, ggain):
    b, lp, n = p.shape
    nc = lp // CHUNK
    nb = ODD_NB if b % ODD_NB == 0 else 1
    t = np.arange(CHUNK)
    tril = jnp.asarray(t[None, :] <= t[:, None], BF)
    l = np.arange(128)
    bones = jnp.asarray(l[:, None] // RW_HD == l[None, :] // RW_HD, BF)
    row = lambda a: a.reshape(1, -1)
    consts = [row(mu), row(w0), row(a0), wa2, g2, row(kks), row(kas), row(rk), row(lnw), row(lnb),
              conv_w, row(alog), row(dtb), row(ggain), tril, bones]
    const2 = lambda i, j: (0, 0)
    return pl.pallas_call(
        _odd_body,
        grid=(b // nb, nc),
        in_specs=[pl.BlockSpec((nb, CHUNK, n), lambda i, j: (i, j, 0))]
        + [pl.BlockSpec(a.shape, const2) for a in consts],
        out_specs=pl.BlockSpec((nb, CHUNK, RW_W + GDN_W), lambda i, j: (i, j, 0)),
        out_shape=jax.ShapeDtypeStruct((b, lp, RW_W + GDN_W), BF),
        scratch_shapes=[
            pltpu.VMEM((nb, RW_HEADS, RW_HD, RW_HD), F32),
            pltpu.VMEM((nb, GDN_HEADS, GDN_DK, GDN_DK), F32),
            pltpu.VMEM((nb, 8, RW_IN), F32),
            pltpu.VMEM((nb, 8, GDN_CONV_CH), F32),
        ],
        compiler_params=pltpu.CompilerParams(
            dimension_semantics=("arbitrary", "arbitrary"), vmem_limit_bytes=VMEM_LIMIT),
        name="odd_mixer",
    )(p, *consts)


def _row_tile(lp):
    best = None
    for tm in range(16, min(lp, 704) + 1, 16):
        if lp % tm == 0:
            best = tm
    assert best is not None, "sequence length must be a multiple of 16"
    return best


def kernel(x, meta_tokens, norm_gains, w_in_even, w_out_even, hg_lb_logits, hg_norm_gain, w_in_odd, w_out_odd, rw_mu, rw_w0, rw_w2, rw_a0, rw_a2, rw_g2, rw_kk_scale, rw_ka_scale, rw_rk, rw_lnx_w, rw_lnx_b, gdn_conv_w, gdn_a_log, gdn_dt_bias, gdn_norm_gain, ffn_w_up, ffn_conv_w, ffn_conv_b, ffn_w_down):
    bsz, seq, d = x.shape
    depth = norm_gains.shape[0]
    l = N_META + seq
    pad = (-l) % CHUNK
    lp = l + pad
    tm = _row_tile(lp)
    lead = pad + N_META

    half = RET_DK // 2
    pos = (jnp.arange(lp, dtype=jnp.int32) - pad).astype(F32)
    inv = ROPE_BASE ** (-jnp.arange(half, dtype=F32) / half)
    ang = pos[:, None] * inv[None, :]
    cos_t = jnp.concatenate([jnp.cos(ang), jnp.cos(ang)], axis=1)
    sin_t = jnp.concatenate([-jnp.sin(ang), jnp.sin(ang)], axis=1)

    h = None
    for layer in range(depth):
        g = norm_gains[layer]
        i = layer // 2
        if layer % 2 == 0:
            w_in = w_in_even[i].astype(BF)
        else:
            w_in = jnp.pad(w_in_odd[i], ((0, 0), (0, ODD_IN_PAD - w_in_odd.shape[2]))).astype(BF)
        if layer == 0:
            p, h = _embed_proj(x, meta_tokens.astype(x.dtype), g[0], w_in, tm, lead)
        else:
            p = _norm_proj(h, g[0], w_in, tm)
        if layer % 2 == 0:
            y = _even_mixer(p, cos_t, sin_t, hg_lb_logits, hg_norm_gain[i], layer)
            w_out = w_out_even[i]
        else:
            wa2 = jnp.zeros((RW_DECAY_LORA + RW_AAA_LORA, 2 * RW_W), F32)
            wa2 = wa2.at[:RW_DECAY_LORA, :RW_W].set(rw_w2[i]).at[RW_DECAY_LORA:, RW_W:].set(rw_a2[i])
            lane_pad = lambda a: jnp.pad(a, (0, LANES - a.shape[0]))
            y = _odd_mixer(p, rw_mu[i], rw_w0[i], rw_a0[i], wa2.astype(BF), rw_g2[i].astype(BF),
                           rw_kk_scale[i], rw_ka_scale[i], rw_rk[i].reshape(-1), rw_lnx_w[i], rw_lnx_b[i],
                           gdn_conv_w[i], lane_pad(gdn_a_log[i]), lane_pad(gdn_dt_bias[i]), gdn_norm_gain[i])
            w_out = w_out_odd[i]
        ffn_args = (y, w_out.astype(BF), g[1], h, g[2], ffn_w_up[layer].astype(BF), ffn_conv_w[layer],
                    ffn_conv_b[layer], ffn_w_down[layer].astype(BF), g[3])
        if layer + 1 < depth:
            h = _out_ffn(*ffn_args, tm, pad)
        else:
            h = _out_ffn_tail(*ffn_args, _row_tile(seq), lead)
    return h
```

```python
import functools
import math

import numpy as np
import jax
import jax.numpy as jnp
from jax import lax
from jax.experimental import pallas as pl
from jax.experimental.pallas import tpu as pltpu

F32 = jnp.float32
BF = jnp.bfloat16

D_MODEL = 1024
CHUNK = 64
N_META = 16
EPS = 1e-6
ROPE_BASE = 10000.0

RET_HEADS = D_MODEL // 256
RET_DK = 128
HG_HEADS = D_MODEL // 256
HG_DK = 128
RW_HEADS = D_MODEL // 128
RW_HD = 64
RW_DECAY_LORA = 64
RW_AAA_LORA = 64
RW_GATE_LORA = 128
RW_LNX_EPS = 64e-5
GDN_HEADS = D_MODEL // 256
GDN_DK = 128
GDN_CONV = 4
D_FF = 128 * ((8 * D_MODEL // 3 + 127) // 128)
FFN_CONV = 3

RET_W = RET_HEADS * RET_DK
HG_W = HG_HEADS * HG_DK
RW_W = RW_HEADS * RW_HD
RW_IN = 3 * RW_W + RW_DECAY_LORA + RW_AAA_LORA + RW_GATE_LORA
GDN_W = GDN_HEADS * GDN_DK
GDN_CONV_CH = 3 * GDN_W
GDN_IN = GDN_CONV_CH + GDN_W + 2 * GDN_HEADS
LANES = 128
GDN_IN_PAD = LANES * ((GDN_IN + LANES - 1) // LANES)
ODD_IN_PAD = RW_IN + GDN_IN_PAD
HG_LEVELS = (32, 16, 8, 4, 2, 1)
HG_ROW_SPLIT = 8
VMEM_LIMIT = 56 * 1024 * 1024
FFN_TILE = 256
EVEN_NB = 8
ODD_NB = 4
ODD_GROUP = 2


def _dot(a, b):
    return jnp.dot(a, b, preferred_element_type=F32)


def _mm(a, b):
    return _dot(a.astype(BF), b.astype(BF))


def _mm_nt(a, b):
    return lax.dot_general(a.astype(BF), b.astype(BF), (((1,), (1,)), ((), ())), preferred_element_type=F32)


def _mm_tn(a, b):
    return lax.dot_general(a.astype(BF), b.astype(BF), (((0,), (0,)), ((), ())), preferred_element_type=F32)


def _split2(x):
    hi = x.astype(BF)
    lo = (x - hi.astype(F32)).astype(BF)
    return hi, lo


def _split3(x):
    hi = x.astype(BF)
    r1 = x - hi.astype(F32)
    mid = r1.astype(BF)
    lo = (r1 - mid.astype(F32)).astype(BF)
    return hi, mid, lo


def _cmm3(c, x):
    hi, mid, lo = _split3(x)
    return _dot(c, hi) + _dot(c, mid) + _dot(c, lo)


def _cmm2(c, x):
    hi, lo = _split2(x)
    return _dot(c, hi) + _dot(c, lo)


def _sigmoid(x):
    return 1.0 / (1.0 + jnp.exp(-x))


def _silu(x):
    return x * _sigmoid(x)


def _softplus(x):
    return jnp.maximum(x, 0.0) + jnp.log1p(jnp.exp(-jnp.abs(x)))


def _rms(x, g):
    return x * lax.rsqrt(jnp.mean(x * x, axis=-1, keepdims=True) + EPS) * g


def _norm_proj_body(h_ref, g_ref, w_ref, o_ref):
    u = _rms(h_ref[...], g_ref[...])
    o_ref[...] = _dot(u.astype(BF), w_ref[...])


def _norm_proj(h, gain, w, tm):
    b, lp, d = h.shape
    n = w.shape[1]
    return pl.pallas_call(
        _norm_proj_body,
        grid=(b, lp // tm),
        in_specs=[
            pl.BlockSpec((None, tm, d), lambda i, j: (i, j, 0)),
            pl.BlockSpec((1, d), lambda i, j: (0, 0)),
            pl.BlockSpec((d, n), lambda i, j: (0, 0), pipeline_mode=pl.Buffered(1)),
        ],
        out_specs=pl.BlockSpec((None, tm, n), lambda i, j: (i, j, 0)),
        out_shape=jax.ShapeDtypeStruct((b, lp, n), F32),
        compiler_params=pltpu.CompilerParams(
            dimension_semantics=("arbitrary", "arbitrary"), vmem_limit_bytes=VMEM_LIMIT),
        name="norm_proj",
    )(h, gain.reshape(1, d), w)


def _embed_proj_body(x_ref, meta_ref, g_ref, w_ref, p_ref, h_ref, *, lead):
    xb = x_ref[0]
    tm, d = xb.shape
    n_zero = lead - meta_ref.shape[0]
    first = jnp.concatenate([jnp.zeros((n_zero, d), F32), meta_ref[...], xb[0:tm - lead, :]], axis=0)
    blk = jnp.where(pl.program_id(1) == 0, first, xb)
    h_ref[...] = blk
    p_ref[...] = _dot(_rms(blk, g_ref[...]).astype(BF), w_ref[...])


def _embed_proj(x, meta, gain, w, tm, lead):
    b, s, d = x.shape
    lp = s + lead
    n = w.shape[1]
    assert lead % 8 == 0 and meta.shape[0] % 8 == 0 and lead < tm and lp % tm == 0
    x_rows = lambda i, j: (i, pl.multiple_of(jnp.maximum(tm * j - lead, 0), 8), 0)
    return pl.pallas_call(
        functools.partial(_embed_proj_body, lead=lead),
        grid=(b, lp // tm),
        in_specs=[
            pl.BlockSpec((pl.Element(1), pl.Element(tm), pl.Element(d)), x_rows),
            pl.BlockSpec(meta.shape, lambda i, j: (0, 0)),
            pl.BlockSpec((1, d), lambda i, j: (0, 0)),
            pl.BlockSpec((d, n), lambda i, j: (0, 0), pipeline_mode=pl.Buffered(1)),
        ],
        out_specs=[pl.BlockSpec((None, tm, n), lambda i, j: (i, j, 0)),
                   pl.BlockSpec((None, tm, d), lambda i, j: (i, j, 0))],
        out_shape=[jax.ShapeDtypeStruct((b, lp, n), F32), jax.ShapeDtypeStruct((b, lp, d), F32)],
        compiler_params=pltpu.CompilerParams(
            dimension_semantics=("arbitrary", "arbitrary"), vmem_limit_bytes=VMEM_LIMIT),
        name="embed_proj",
    )(x, meta, gain.reshape(1, d), w)


def _shift_rows(a, carry_ref, cols, n_back, row):
    out = []
    for s in range(1, n_back + 1):
        sh = pltpu.roll(a, s, 0)
        for r in range(s):
            sh = jnp.where(row == r, carry_ref[8 - s + r:9 - s + r, cols], sh)
        out.append(sh)
    return out


def _merge_pieces(*lists):
    total = max(len(l) for l in lists)
    keyed = []
    for li, l in enumerate(lists):
        for k, piece in enumerate(l):
            keyed.append(((k + 0.5) * total / len(l), li, k, piece))
    keyed.sort(key=lambda t: t[:3])
    return [t[3] for t in keyed]


def _ffn_pieces(make_x, shift, g2_ref, wup_ref, cw_ref, cb_ref, wdn_ref, g3_ref, emit):
    n_tiles = D_FF // FFN_TILE
    st = {}

    def up(c):
        return [_dot(st["u"], wup_ref[:, part * D_FF + c * FFN_TILE:part * D_FF + (c + 1) * FFN_TILE])
                for part in range(2)]

    def head():
        st["x"] = make_x()
        st["u"] = _rms(st["x"], g2_ref[...]).astype(BF)
        st["acts"] = []

    def lead_in():
        st["pending"] = [up(c) for c in range(min(2, n_tiles))]

    def tile(c):
        def run():
            cur = st["pending"].pop(0)
            if c + 2 < n_tiles:
                st["pending"].append(up(c + 2))
            z = []
            for part in range(2):
                cols = slice(part * D_FF + c * FFN_TILE, part * D_FF + (c + 1) * FFN_TILE)
                a = cur[part]
                am1, am2 = shift(a, cols)
                z.append(am2 * cw_ref[0:1, cols] + am1 * cw_ref[1:2, cols] + a * cw_ref[2:3, cols] + cb_ref[:, cols])
            st["acts"].append((_silu(z[0]) * z[1]).astype(BF))
        return run

    def tail():
        acc = _dot(jnp.concatenate(st["acts"], axis=1), wdn_ref[...])
        emit(st["x"] + _rms(acc, g3_ref[...]))

    return [head, lead_in] + [tile(c) for c in range(n_tiles)] + [tail]


def _ffn_body(y_ref, wout_ref, g1_ref, h_ref, g2_ref, wup_ref, cw_ref, cb_ref, wdn_ref, g3_ref, o_ref, carry_ref,
              *, tm, pad):
    @pl.when(pl.program_id(1) == 0)
    def _():
        carry_ref[...] = jnp.zeros_like(carry_ref)

    half = tm // 2
    lists = []
    for r0 in (0, half):
        rows = slice(r0, r0 + half)
        t = pl.program_id(1) * tm + r0 + lax.broadcasted_iota(jnp.int32, (half, D_MODEL), 0)
        row = lax.broadcasted_iota(jnp.int32, (half, FFN_TILE), 0)

        def make_x(rows=rows, t=t):
            return jnp.where(t >= pad, h_ref[rows, :] + _rms(_dot(y_ref[rows, :], wout_ref[...]), g1_ref[...]), 0.0)

        def shift(a, cols, row=row):
            am1, am2 = _shift_rows(a, carry_ref, cols, FFN_CONV - 1, row)
            carry_ref[:, cols] = a[half - 8:half, :]
            return am1, am2

        def emit(out, rows=rows, t=t):
            o_ref[rows, :] = jnp.where(t >= pad, out, 0.0)

        lists.append(_ffn_pieces(make_x, shift, g2_ref, wup_ref, cw_ref, cb_ref, wdn_ref, g3_ref, emit))
    for piece in _merge_pieces(*lists):
        piece()


def _ffn_tail_body(y_ref, wout_ref, g1_ref, h_ref, g2_ref, wup_ref, cw_ref, cb_ref, wdn_ref, g3_ref, o_ref, *,
                   halo):
    half = o_ref.shape[0] // 2
    shift = lambda a, cols: (pltpu.roll(a, 1, 0), pltpu.roll(a, 2, 0))
    lists = []
    for r0 in (0, half):
        def make_x(r0=r0):
            rows = slice(r0, r0 + half + halo)
            return h_ref[0, rows, :] + _rms(_dot(y_ref[0, rows, :], wout_ref[...]), g1_ref[...])

        def emit(out, r0=r0):
            o_ref[r0:r0 + half, :] = out[halo:, :]

        lists.append(_ffn_pieces(make_x, shift, g2_ref, wup_ref, cw_ref, cb_ref, wdn_ref, g3_ref, emit))
    for piece in _merge_pieces(*lists):
        piece()


def _ffn_weight_specs(k, d, ff2):
    const = lambda i, j: (0, 0)
    return dict(
        w_out=pl.BlockSpec((k, d), const, pipeline_mode=pl.Buffered(1)), gain=pl.BlockSpec((1, d), const),
        w_up=pl.BlockSpec((d, ff2), const, pipeline_mode=pl.Buffered(1)),
        conv_w=pl.BlockSpec((FFN_CONV, ff2), const), conv_b=pl.BlockSpec((1, ff2), const),
        w_down=pl.BlockSpec((ff2 // 2, d), const, pipeline_mode=pl.Buffered(1)))


def _out_ffn(y, w_out, g1, h, g2, w_up, conv_w, conv_b, w_down, g3, tm, pad):
    b, lp, d = h.shape
    k = y.shape[2]
    ff2 = w_up.shape[1]
    assert lp % tm == 0 and tm % 16 == 0
    ws = _ffn_weight_specs(k, d, ff2)
    rows = lambda width: pl.BlockSpec((None, tm, width), lambda i, j: (i, j, 0))
    return pl.pallas_call(
        functools.partial(_ffn_body, tm=tm, pad=pad),
        grid=(b, lp // tm),
        in_specs=[rows(k), ws["w_out"], ws["gain"], rows(d), ws["gain"], ws["w_up"], ws["conv_w"], ws["conv_b"],
                  ws["w_down"], ws["gain"]],
        out_specs=rows(d),
        out_shape=jax.ShapeDtypeStruct((b, lp, d), F32),
        scratch_shapes=[pltpu.VMEM((8, ff2), F32)],
        compiler_params=pltpu.CompilerParams(
            dimension_semantics=("arbitrary", "arbitrary"), vmem_limit_bytes=VMEM_LIMIT),
        name="conv_ffn",
    )(y, w_out, g1.reshape(1, d), h, g2.reshape(1, d), w_up, conv_w, conv_b.reshape(1, ff2), w_down,
      g3.reshape(1, d))


def _out_ffn_tail(y, w_out, g1, h, g2, w_up, conv_w, conv_b, w_down, g3, tm, lead):
    b, lp, d = h.shape
    seq = lp - lead
    k = y.shape[2]
    ff2 = w_up.shape[1]
    halo = 8
    assert seq % tm == 0 and tm % 16 == 0 and lead >= halo and lead % 8 == 0
    ws = _ffn_weight_specs(k, d, ff2)
    rows = lambda width: pl.BlockSpec(
        (pl.Element(1), pl.Element(tm + halo), pl.Element(width)),
        lambda i, j: (i, pl.multiple_of(lead - halo + tm * j, 8), 0))
    return pl.pallas_call(
        functools.partial(_ffn_tail_body, halo=halo),
        grid=(b, seq // tm),
        in_specs=[rows(k), ws["w_out"], ws["gain"], rows(d), ws["gain"], ws["w_up"], ws["conv_w"], ws["conv_b"],
                  ws["w_down"], ws["gain"]],
        out_specs=pl.BlockSpec((None, tm, d), lambda i, j: (i, j, 0)),
        out_shape=jax.ShapeDtypeStruct((b, seq, d), F32),
        compiler_params=pltpu.CompilerParams(
            dimension_semantics=("arbitrary", "arbitrary"), vmem_limit_bytes=VMEM_LIMIT),
        name="conv_ffn_tail",
    )(y, w_out, g1.reshape(1, d), h, g2.reshape(1, d), w_up, conv_w, conv_b.reshape(1, ff2), w_down,
      g3.reshape(1, d))


def _hg_consts():
    c = CHUNK
    t = np.arange(c)
    small = [s for s in HG_LEVELS if s < HG_ROW_SPLIT]
    m = np.zeros((1 + len(small), c, c), np.float32)
    m[0] = t[None, :] <= t[:, None]
    masks = np.zeros((len(HG_LEVELS) + 1, c, c), np.float32)
    ii, jj = t[:, None], t[None, :]
    for li, s in enumerate(HG_LEVELS):
        if s < HG_ROW_SPLIT:
            for i in range(c):
                mid = (i // (2 * s)) * (2 * s) + s - 1
                if i % (2 * s) >= s:
                    m[1 + small.index(s), i, mid + 1:i + 1] = 1.0
                else:
                    m[1 + small.index(s), i, i + 1:mid + 1] = 1.0
        masks[li] = (ii // (2 * s) == jj // (2 * s)) & (ii % (2 * s) >= s) & (jj % (2 * s) < s)
    masks[-1] = ii == jj
    return m.reshape(-1, c), masks


def _even_body(p_ref, cos_ref, sin_ref, lbl_ref, hgain_ref, mall_ref, masks_ref, y_ref, sret_ref, shg_ref, *,
               layer):
    @pl.when(pl.program_id(1) == 0)
    def _():
        sret_ref[...] = jnp.zeros_like(sret_ref)
        shg_ref[...] = jnp.zeros_like(shg_ref)

    c = CHUNK
    cosv = cos_ref[...]
    sinv = sin_ref[...]
    ii = lax.broadcasted_iota(jnp.int32, (c, c), 0)
    jj = lax.broadcasted_iota(jnp.int32, (c, c), 1)
    diff = (ii - jj).astype(F32)
    rowi = lax.broadcasted_iota(jnp.int32, (c, RET_DK), 0).astype(F32)

    nb = p_ref.shape[0]
    col = lambda base, h: slice(base + h * 128, base + (h + 1) * 128)
    units = [(b, h) for b in range(nb) for h in range(RET_HEADS)]
    nu = len(units)
    lgs = [math.log1p(-(2.0 ** (-5.0 - h))) for _, h in units]
    qs = [p_ref[b, :, col(0, h)] for b, h in units]
    ks = [p_ref[b, :, col(RET_W, h)] for b, h in units]
    vs = [p_ref[b, :, col(2 * RET_W, h)] for b, h in units]
    qs = [q * cosv + pltpu.roll(q, RET_DK // 2, 1) * sinv for q in qs]
    ks = [(k * cosv + pltpu.roll(k, RET_DK // 2, 1) * sinv) * (RET_DK ** -0.5) for k in ks]
    ss = [sret_ref[b, h] for b, h in units]
    att = [_mm_nt(qs[u], ks[u]) * jnp.where(diff >= 0, jnp.exp(lgs[u] * jnp.maximum(diff, 0.0)), 0.0)
           for u in range(nu)]
    inter = [_mm(qs[u] * jnp.exp(lgs[u] * (rowi + 1.0)), ss[u]) for u in range(nu)]
    kv = [_mm_tn(ks[u] * jnp.exp(lgs[u] * (c - 1.0 - rowi)), vs[u]) for u in range(nu)]
    outs = [_mm(att[u], vs[u]) + inter[u] for u in range(nu)]
    for u, (b, h) in enumerate(units):
        sret_ref[b, h] = ss[u] * math.exp(lgs[u] * c) + kv[u]
        o = outs[u]
        xc = o - jnp.mean(o, axis=-1, keepdims=True)
        oa = xc * lax.rsqrt(jnp.mean(xc * xc, axis=-1, keepdims=True) + EPS)
        y_ref[b, :, col(0, h)] = (oa * _silu(p_ref[b, :, col(3 * RET_W, h)])).astype(y_ref.dtype)

    lgt = lbl_ref[...]
    mx = jnp.max(lgt, axis=0, keepdims=True)
    ex = jnp.exp(lgt - mx)
    sm = ex / jnp.sum(ex, axis=0, keepdims=True)
    lb_all = jnp.sum(sm[0:layer + 1, :], axis=0, keepdims=True)
    base = 4 * RET_W
    mall = mall_ref[...]
    nl = len(HG_LEVELS)
    units = [(b, h) for b in range(nb) for h in range(HG_HEADS)]
    nu = len(units)
    lbs = [lb_all[:, col(0, h)] for _, h in units]
    qs = [p_ref[b, :, col(base, h)] for b, h in units]
    fbs = [p_ref[b, :, col(base + HG_W, h)] for b, h in units]
    vs = [p_ref[b, :, col(base + 2 * HG_W, h)] for b, h in units]
    ks = [(1.0 - lbs[u]) * _sigmoid(-fbs[u]) for u in range(nu)]
    es = [_cmm2(mall, jnp.log(lbs[u] + (1.0 - lbs[u]) * _sigmoid(fbs[u]))) for u in range(nu)]
    xs = [jnp.exp(e) for e in es]
    rest = [jnp.exp(e[c - 1:c, :] - e[0:c, :]) for e in es]

    def level_factor(u, li):
        s = HG_LEVELS[li]
        if s < HG_ROW_SPLIT:
            k = 1 + [t for t in HG_LEVELS if t < HG_ROW_SPLIT].index(s)
            return xs[u][k * c:(k + 1) * c, :]
        cb = es[u][0:c, :]
        parts = []
        for p0 in range(0, c, 2 * s):
            mid = cb[p0 + s - 1:p0 + s, :]
            parts += [mid - cb[p0:p0 + s, :], cb[p0 + s:p0 + 2 * s, :] - mid]
        return jnp.exp(jnp.concatenate(parts, axis=0))

    sts = [shg_ref[b, h] for b, h in units]
    att = [masks_ref[nl] * _mm_nt(qs[u], ks[u]) for u in range(nu)]
    for li in range(nl):
        fac = [level_factor(u, li) for u in range(nu)]
        att = [att[u] + masks_ref[li] * _mm_nt(qs[u] * fac[u], ks[u] * fac[u]) for u in range(nu)]
    inter = [_mm_nt(qs[u] * xs[u][0:c, :], sts[u]) for u in range(nu)]
    kv = [_mm_tn(vs[u], ks[u] * rest[u]) for u in range(nu)]
    outs = [_mm(att[u], vs[u]) + inter[u] for u in range(nu)]
    for u, (b, h) in enumerate(units):
        shg_ref[b, h] = sts[u] * xs[u][c - 1:c, :] + kv[u]
        ob = _rms(outs[u], hgain_ref[:, col(0, h)])
        y_ref[b, :, col(RET_W, h)] = (ob * _silu(p_ref[b, :, col(base + 3 * HG_W, h)])).astype(y_ref.dtype)


def _even_mixer(p, cos_t, sin_t, lb_logits, hg_gain, layer):
    b, lp, n = p.shape
    nc = lp // CHUNK
    nb = EVEN_NB if b % EVEN_NB == 0 else 1
    mall, masks = _hg_consts()
    const2 = lambda i, j: (0, 0)
    return pl.pallas_call(
        functools.partial(_even_body, layer=layer),
        grid=(b // nb, nc),
        in_specs=[
            pl.BlockSpec((nb, CHUNK, n), lambda i, j: (i, j, 0)),
            pl.BlockSpec((CHUNK, RET_DK), lambda i, j: (j, 0)),
            pl.BlockSpec((CHUNK, RET_DK), lambda i, j: (j, 0)),
            pl.BlockSpec(lb_logits.shape, const2),
            pl.BlockSpec((1, HG_W), const2),
            pl.BlockSpec(mall.shape, const2),
            pl.BlockSpec(masks.shape, lambda i, j: (0, 0, 0)),
        ],
        out_specs=pl.BlockSpec((nb, CHUNK, RET_W + HG_W), lambda i, j: (i, j, 0)),
        out_shape=jax.ShapeDtypeStruct((b, lp, RET_W + HG_W), BF),
        scratch_shapes=[pltpu.VMEM((nb, RET_HEADS, RET_DK, RET_DK), F32),
                        pltpu.VMEM((nb, HG_HEADS, HG_DK, HG_DK), F32)],
        compiler_params=pltpu.CompilerParams(
            dimension_semantics=("arbitrary", "arbitrary"), vmem_limit_bytes=VMEM_LIMIT),
        name="even_mixer",
    )(p, cos_t, sin_t, lb_logits, hg_gain.reshape(1, HG_W), jnp.asarray(mall, BF), jnp.asarray(masks))


def _odd_body(p_ref, mu_ref, w0_ref, a0_ref, wa2_ref, g2_ref, kks_ref, kas_ref, rk_ref, lnw_ref, lnb_ref,
              cw_ref, alog_ref, dtb_ref, ggain_ref, tril_ref, bones_ref, y_ref,
              hrw_ref, sgd_ref, cpc_ref, ccv_ref, *, mixer):
    @pl.when(pl.program_id(1) == 0)
    def _():
        hrw_ref[...] = jnp.zeros_like(hrw_ref)
        sgd_ref[...] = jnp.zeros_like(sgd_ref)
        cpc_ref[...] = jnp.zeros_like(cpc_ref)
        ccv_ref[...] = jnp.zeros_like(ccv_ref)

    nb = p_ref.shape[0]
    c = CHUNK
    tril = tril_ref[...]
    bones = bones_ref[...]
    ii = lax.broadcasted_iota(jnp.int32, (c, c), 0)
    jj = lax.broadcasted_iota(jnp.int32, (c, c), 1)
    strict = ii > jj
    incl = ii >= jj
    eye = ii == jj
    eye_f = eye.astype(F32)
    ii2 = lax.broadcasted_iota(jnp.int32, (c, 2 * c), 0)
    lane2 = lax.broadcasted_iota(jnp.int32, (c, 2 * c), 1)
    jj2 = jnp.bitwise_and(lane2, c - 1)
    ak_mask = (ii2 > jj2) & (lane2 >= c)
    rbk_mask = ii2 >= jj2
    lane = lax.broadcasted_iota(jnp.int32, (c, 128), 1)
    heads = [slice(h * RW_HD, (h + 1) * RW_HD) for h in range(RW_HEADS)]
    base = RW_IN
    inv_hd = 1.0 / RW_HD

    def seg_sum(x):
        return jnp.concatenate(
            [_mm(x[:, p * 128:(p + 1) * 128], bones) for p in range(x.shape[1] // 128)], axis=1)

    def prep_pieces(b, d):
        def shift():
            pc = p_ref[b, :, 0:RW_IN]
            row = lax.broadcasted_iota(jnp.int32, (c, RW_IN), 0)
            prev = jnp.where(row == 0, cpc_ref[b, 7:8, :], pltpu.roll(pc, 1, 0))
            cpc_ref[b] = pc[c - 8:c, :]
            pcs = pc + (prev - pc) * mu_ref[...]
            d.update(r=pcs[:, 0:RW_W], k=pcs[:, RW_W:2 * RW_W], v=pcs[:, 2 * RW_W:3 * RW_W],
                     lo=pcs[:, 3 * RW_W:3 * RW_W + 128], glo=pcs[:, 3 * RW_W + 128:RW_IN])

        def lora():
            lo = d["lo"]
            wa = _mm(jnp.where(lane < RW_DECAY_LORA, jnp.tanh(lo), lo), wa2_ref[...])
            log_w = -_softplus(-(w0_ref[...] + wa[:, 0:RW_W])) - 0.5
            d["lw"] = -jnp.exp(log_w)
            d["a"] = _sigmoid(a0_ref[...] + wa[:, RW_W:2 * RW_W])
            d["gate"] = _mm(_sigmoid(d["glo"]), g2_ref[...])

        def keys():
            kks = d["k"] * kks_ref[...]
            kkn = kks * lax.rsqrt(seg_sum(kks * kks) + EPS)
            d["k2"] = d["k"] * (1.0 + (d["a"] - 1.0) * kas_ref[...])
            d["kkn"] = kkn
            d["beta"] = kkn * d["a"]
            d["c_inc"] = _cmm2(tril, d["lw"])

        def decays():
            c_inc, k2, beta = d["c_inc"], d["k2"], d["beta"]
            c_last = c_inc[c - 1:c, :]
            e_neg = jnp.exp(-c_inc)
            e_rest = jnp.exp(c_last - c_inc)
            d.update(ah=-d["kkn"] * jnp.exp(c_inc - d["lw"]), rh=d["r"] * jnp.exp(c_inc), bt=beta * e_neg,
                     kt=k2 * e_neg, bg=beta * e_rest, kg=k2 * e_rest, gc=jnp.exp(c_last),
                     ar=[], ls=[], ak=[], rbk=[], gcol=[], bgkg=[])

        def gram(h):
            def run():
                sl = heads[h]
                ar = jnp.concatenate([d["ah"][:, sl], d["rh"][:, sl]], axis=0).astype(BF)
                gm = _mm_nt(ar, jnp.concatenate([d["bt"][:, sl], d["kt"][:, sl]], axis=0))
                d["ar"].append(ar)
                d["ls"].append(jnp.where(strict, gm[0:c, 0:c], 0.0))
                d["ak"].append(jnp.where(ak_mask, gm[0:c, :], 0.0).astype(BF))
                d["rbk"].append(jnp.where(rbk_mask, gm[c:2 * c, :], 0.0).astype(BF))
                d["gcol"].append(jnp.sum(eye_f * d["gc"][:, sl], axis=1, keepdims=True))
                d["bgkg"].append(jnp.concatenate([d["bg"][:, sl], d["kg"][:, sl]], axis=0).astype(BF))
            return run

        def conv():
            x = p_ref[b, :, base:base + GDN_CONV_CH]
            rowc = lax.broadcasted_iota(jnp.int32, (c, GDN_CONV_CH), 0)
            xm1, xm2, xm3 = _shift_rows(x, ccv_ref.at[b], slice(None), GDN_CONV - 1, rowc)
            ccv_ref[b] = x[c - 8:c, :]
            d["qkv"] = _silu(xm3 * cw_ref[0:1, :] + xm2 * cw_ref[1:2, :] + xm1 * cw_ref[2:3, :] + x * cw_ref[3:4, :])

        def gates():
            sc = p_ref[b, :, base + GDN_CONV_CH + GDN_W:base + GDN_IN_PAD]
            g_all = -jnp.exp(alog_ref[...]) * _softplus(sc + dtb_ref[...])
            d["b_all"] = _sigmoid(sc)
            cg = _cmm3(tril, g_all)
            d["cg_all"] = jnp.concatenate(
                [jnp.broadcast_to(cg[:, h:h + 1], (c, 128)) for h in range(GDN_HEADS)], axis=1)
            d["gd"] = []

        def gdn_head(h):
            def run():
                hs = slice(h * 128, (h + 1) * 128)
                qkv = d["qkv"]
                q = qkv[:, hs]
                kd = qkv[:, GDN_W + h * 128:GDN_W + (h + 1) * 128]
                vd = qkv[:, 2 * GDN_W + h * 128:2 * GDN_W + (h + 1) * 128]
                q = q * lax.rsqrt(jnp.sum(q * q, axis=-1, keepdims=True) + EPS) * (GDN_DK ** -0.5)
                kd = kd * lax.rsqrt(jnp.sum(kd * kd, axis=-1, keepdims=True) + EPS)
                b_b = jnp.broadcast_to(d["b_all"][:, GDN_HEADS + h:GDN_HEADS + h + 1], (c, 128))
                cg = d["cg_all"][:, hs]
                cg_row = jnp.sum(jnp.where(eye, cg[:, 0:c], 0.0), axis=0, keepdims=True)
                decay = jnp.exp(jnp.where(incl, cg[:, 0:c] - cg_row, -jnp.inf))
                eg = jnp.exp(cg)
                cl = cg[c - 1:c, :]
                d["gd"].append(dict(
                    l=-jnp.where(strict, b_b[:, 0:c] * _mm_nt(kd, kd) * decay, 0.0),
                    rhs=jnp.concatenate([b_b * vd, b_b * kd * eg], axis=1).astype(BF),
                    qk=(_mm_nt(q, kd) * decay).astype(BF), q_in=(q * eg).astype(BF),
                    k_out=(kd * jnp.exp(cl - cg)).astype(BF), sd=jnp.exp(cl)))
            return run

        if mixer == 0:
            return [shift, lora, keys, decays] + [gram(h) for h in range(RW_HEADS)]
        return [conv, gates] + [gdn_head(h) for h in range(GDN_HEADS)]

    def inverse_pieces(rows, ds, out):
        st = {}

        def start():
            if mixer == 0:
                ls = [l for b in rows for l in ds[b]["ls"]]
            else:
                ls = [g["l"] for b in rows for g in ds[b]["gd"]]
            st["p"] = [eye_f + l for l in ls]
            st["l"] = [l.astype(BF) for l in ls]

        def square():
            st["l"] = [_mm(l, l).astype(BF) for l in st["l"]]

        def extend():
            st["p"] = [p + _mm(l, p) for l, p in zip(st["l"], st["p"])]

        def finish():
            out["tinv"] = st["p"]

        levels = []
        span = 2
        while span < c:
            levels += [square, extend]
            span *= 2
        return [start] + levels + [finish]

    def state_pieces(rows, ds, inv):
        units = [(b, h) for b in rows for h in range(RW_HEADS)]
        gunits = [(b, h) for b in rows for h in range(GDN_HEADS)]
        st = {}

        def rw_read():
            st["h0"] = [hrw_ref[b, h] for b, h in units]
            st["hr"] = [_mm(ds[b]["ar"][h], st["h0"][u]) for u, (b, h) in enumerate(units)]
            st["vs"] = [ds[b]["v"][:, heads[h]].astype(BF) for b, h in units]

        def rw_mix():
            vs = st["vs"]
            st["x"] = [st["hr"][u][0:c, :] + _mm(ds[b]["ak"][h], jnp.concatenate([vs[u], vs[u]], axis=0))
                       for u, (b, h) in enumerate(units)]

        def rw_solve():
            us = [_mm(inv["tinv"][u], st["x"][u]) for u in range(len(units))]
            st["uv"] = [jnp.concatenate([us[u].astype(BF), st["vs"][u]], axis=0) for u in range(len(units))]

        def rw_out():
            st["ys"] = [st["hr"][u][c:2 * c, :] + _mm(ds[b]["rbk"][h], st["uv"][u])
                        for u, (b, h) in enumerate(units)]
            for u, (b, h) in enumerate(units):
                hrw_ref[b, h] = st["h0"][u] * ds[b]["gcol"][h] + _mm_tn(ds[b]["bgkg"][h], st["uv"][u])

        def rw_center():
            st["y"] = [jnp.concatenate(st["ys"][k * RW_HEADS:(k + 1) * RW_HEADS], axis=1) for k in range(len(rows))]
            st["mean"] = [seg_sum(y) * inv_hd for y in st["y"]]
            st["bonus"] = [seg_sum(ds[b]["r"] * ds[b]["k2"] * rk_ref[...]) for b in rows]

        def rw_scale():
            st["yc"] = [y - m for y, m in zip(st["y"], st["mean"])]
            st["var"] = [seg_sum(yc * yc) * inv_hd for yc in st["yc"]]

        def rw_finish():
            for k, b in enumerate(rows):
                d = ds[b]
                yn = st["yc"][k] * lax.rsqrt(st["var"][k] + RW_LNX_EPS) * lnw_ref[...] + lnb_ref[...]
                out = (yn + st["bonus"][k] * d["v"]) * d["gate"]
                y_ref[b, :, 0:RW_W] = out.astype(y_ref.dtype)

        def gd_solve():
            n_rw = 0
            sols = [_mm(inv["tinv"][n_rw + u], ds[b]["gd"][h]["rhs"]) for u, (b, h) in enumerate(gunits)]
            st["s0"] = [sgd_ref[b, h] for b, h in gunits]
            st["v_new"] = [sols[u][:, 0:128] - _mm(sols[u][:, 128:256], st["s0"][u]) for u in range(len(gunits))]

        def gd_out(u, b, h):
            def run():
                g = ds[b]["gd"][h]
                s0, v_new = st["s0"][u], st["v_new"][u]
                o = _mm(g["q_in"], s0) + _mm(g["qk"], v_new)
                sgd_ref[b, h] = s0 * g["sd"] + _mm_tn(g["k_out"], v_new)
                og = p_ref[b, :, base + GDN_CONV_CH + h * 128:base + GDN_CONV_CH + (h + 1) * 128]
                out = _rms(o, ggain_ref[...]) * _silu(og)
                y_ref[b, :, h * 128:(h + 1) * 128] = out.astype(y_ref.dtype)
            return run

        if mixer == 0:
            return [rw_read, rw_mix, rw_solve, rw_out, rw_center, rw_scale, rw_finish]
        return [gd_solve] + [gd_out(u, b, h) for u, (b, h) in enumerate(gunits)]

    group = min(ODD_GROUP, nb)
    groups = [list(range(g, g + group)) for g in range(0, nb, group)]
    ds = [dict() for _ in range(nb)]
    invs = [dict() for _ in groups]
    stage1 = [_merge_pieces(*[prep_pieces(b, ds[b]) for b in rows]) for rows in groups]
    stage2 = [inverse_pieces(rows, ds, invs[g]) for g, rows in enumerate(groups)]
    stage3 = [state_pieces(rows, ds, invs[g]) for g, rows in enumerate(groups)]
    for slot in range(len(groups) + 2):
        active = []
        if slot < len(groups):
            active.append(stage1[slot])
        if 0 <= slot - 1 < len(groups):
            active.append(stage2[slot - 1])
        if 0 <= slot - 2 < len(groups):
            active.append(stage3[slot - 2])
        for piece in _merge_pieces(*active):
            piece()


def _odd_mixer(p, mu, w0, a0, wa2, g2, kks, kas, rk, lnw, lnb, conv_w, alog, dtb, ggain):
    b, lp, n = p.shape
    nc = lp // CHUNK
    nb = ODD_NB if b % ODD_NB == 0 else 1
    t = np.arange(CHUNK)
    tril = jnp.asarray(t[None, :] <= t[:, None], BF)
    l = np.arange(128)
    bones = jnp.asarray(l[:, None] // RW_HD == l[None, :] // RW_HD, BF)
    row = lambda a: a.reshape(1, -1)
    consts = [row(mu), row(w0), row(a0), wa2, g2, row(kks), row(kas), row(rk), row(lnw), row(lnb),
              conv_w, row(alog), row(dtb), row(ggain), tril, bones]
    const2 = lambda i, j: (0, 0)
    calls = [pl.pallas_call(
        functools.partial(_odd_body, mixer=mixer),
        grid=(b // nb, nc),
        in_specs=[pl.BlockSpec((nb, CHUNK, n), lambda i, j: (i, j, 0))]
        + [pl.BlockSpec(a.shape, const2) for a in consts],
        out_specs=pl.BlockSpec((nb, CHUNK, RW_W), lambda i, j: (i, j, 0)),
        out_shape=jax.ShapeDtypeStruct((b, lp, RW_W), BF),
        scratch_shapes=[
            pltpu.VMEM((nb, RW_HEADS, RW_HD, RW_HD), F32),
            pltpu.VMEM((nb, GDN_HEADS, GDN_DK, GDN_DK), F32),
            pltpu.VMEM((nb, 8, RW_IN), F32),
            pltpu.VMEM((nb, 8, GDN_CONV_CH), F32),
        ],
        compiler_params=pltpu.CompilerParams(
            dimension_semantics=("arbitrary", "arbitrary"), vmem_limit_bytes=VMEM_LIMIT),
        name=("rwkv_mixer", "deltanet_mixer")[mixer],
    )(p, *consts) for mixer in range(2)]
    return jnp.concatenate(calls, axis=-1)


def _row_tile(lp):
    best = None
    for tm in range(16, min(lp, 704) + 1, 16):
        if lp % tm == 0:
            best = tm
    assert best is not None, "sequence length must be a multiple of 16"
    return best


def kernel(x, meta_tokens, norm_gains, w_in_even, w_out_even, hg_lb_logits, hg_norm_gain, w_in_odd, w_out_odd, rw_mu, rw_w0, rw_w2, rw_a0, rw_a2, rw_g2, rw_kk_scale, rw_ka_scale, rw_rk, rw_lnx_w, rw_lnx_b, gdn_conv_w, gdn_a_log, gdn_dt_bias, gdn_norm_gain, ffn_w_up, ffn_conv_w, ffn_conv_b, ffn_w_down):
    bsz, seq, d = x.shape
    depth = norm_gains.shape[0]
    l = N_META + seq
    pad = (-l) % CHUNK
    lp = l + pad
    tm = _row_tile(lp)
    lead = pad + N_META

    half = RET_DK // 2
    pos = (jnp.arange(lp, dtype=jnp.int32) - pad).astype(F32)
    inv = ROPE_BASE ** (-jnp.arange(half, dtype=F32) / half)
    ang = pos[:, None] * inv[None, :]
    cos_t = jnp.concatenate([jnp.cos(ang), jnp.cos(ang)], axis=1)
    sin_t = jnp.concatenate([-jnp.sin(ang), jnp.sin(ang)], axis=1)

    h = None
    for layer in range(depth):
        g = norm_gains[layer]
        i = layer // 2
        if layer % 2 == 0:
            w_in = w_in_even[i].astype(BF)
        else:
            w_in = jnp.pad(w_in_odd[i], ((0, 0), (0, ODD_IN_PAD - w_in_odd.shape[2]))).astype(BF)
        if layer == 0:
            p, h = _embed_proj(x, meta_tokens.astype(x.dtype), g[0], w_in, tm, lead)
        else:
            p = _norm_proj(h, g[0], w_in, tm)
        if layer % 2 == 0:
            y = _even_mixer(p, cos_t, sin_t, hg_lb_logits, hg_norm_gain[i], layer)
            w_out = w_out_even[i]
        else:
            wa2 = jnp.zeros((RW_DECAY_LORA + RW_AAA_LORA, 2 * RW_W), F32)
            wa2 = wa2.at[:RW_DECAY_LORA, :RW_W].set(rw_w2[i]).at[RW_DECAY_LORA:, RW_W:].set(rw_a2[i])
            lane_pad = lambda a: jnp.pad(a, (0, LANES - a.shape[0]))
            y = _odd_mixer(p, rw_mu[i], rw_w0[i], rw_a0[i], wa2.astype(BF), rw_g2[i].astype(BF),
                           rw_kk_scale[i], rw_ka_scale[i], rw_rk[i].reshape(-1), rw_lnx_w[i], rw_lnx_b[i],
                           gdn_conv_w[i], lane_pad(gdn_a_log[i]), lane_pad(gdn_dt_bias[i]), gdn_norm_gain[i])
            w_out = w_out_odd[i]
        ffn_args = (y, w_out.astype(BF), g[1], h, g[2], ffn_w_up[layer].astype(BF), ffn_conv_w[layer],
                    ffn_conv_b[layer], ffn_w_down[layer].astype(BF), g[3])
        if layer + 1 < depth:
            h = _out_ffn(*ffn_args, tm, pad)
        else:
            h = _out_ffn_tail(*ffn_args, _row_tile(seq), lead)
    return h
```

```python
import functools
import math

import numpy as np
import jax
import jax.numpy as jnp
from jax import lax
from jax.experimental import pallas as pl
from jax.experimental.pallas import tpu as pltpu

F32 = jnp.float32
BF = jnp.bfloat16

D_MODEL = 1024
CHUNK = 64
N_META = 16
EPS = 1e-6
ROPE_BASE = 10000.0

RET_HEADS = D_MODEL // 256
RET_DK = 128
HG_HEADS = D_MODEL // 256
HG_DK = 128
RW_HEADS = D_MODEL // 128
RW_HD = 64
RW_DECAY_LORA = 64
RW_AAA_LORA = 64
RW_GATE_LORA = 128
RW_LNX_EPS = 64e-5
GDN_HEADS = D_MODEL // 256
GDN_DK = 128
GDN_CONV = 4
D_FF = 128 * ((8 * D_MODEL // 3 + 127) // 128)
FFN_CONV = 3

RET_W = RET_HEADS * RET_DK
HG_W = HG_HEADS * HG_DK
RW_W = RW_HEADS * RW_HD
RW_IN = 3 * RW_W + RW_DECAY_LORA + RW_AAA_LORA + RW_GATE_LORA
GDN_W = GDN_HEADS * GDN_DK
GDN_CONV_CH = 3 * GDN_W
GDN_IN = GDN_CONV_CH + GDN_W + 2 * GDN_HEADS
LANES = 128
GDN_IN_PAD = LANES * ((GDN_IN + LANES - 1) // LANES)
ODD_IN_PAD = RW_IN + GDN_IN_PAD
HG_LEVELS = (32, 16, 8, 4, 2, 1)
HG_ROW_SPLIT = 8
VMEM_LIMIT = 56 * 1024 * 1024
FFN_TILE = 256
EVEN_NB = 8
ODD_NB = 4
ODD_GROUP = 2


def _dot(a, b):
    return jnp.dot(a, b, preferred_element_type=F32)


def _mm(a, b):
    return _dot(a.astype(BF), b.astype(BF))


def _mm_nt(a, b):
    return lax.dot_general(a.astype(BF), b.astype(BF), (((1,), (1,)), ((), ())), preferred_element_type=F32)


def _mm_tn(a, b):
    return lax.dot_general(a.astype(BF), b.astype(BF), (((0,), (0,)), ((), ())), preferred_element_type=F32)


def _split2(x):
    hi = x.astype(BF)
    lo = (x - hi.astype(F32)).astype(BF)
    return hi, lo


def _split3(x):
    hi = x.astype(BF)
    r1 = x - hi.astype(F32)
    mid = r1.astype(BF)
    lo = (r1 - mid.astype(F32)).astype(BF)
    return hi, mid, lo


def _cmm3(c, x):
    hi, mid, lo = _split3(x)
    return _dot(c, hi) + _dot(c, mid) + _dot(c, lo)


def _cmm2(c, x):
    hi, lo = _split2(x)
    return _dot(c, hi) + _dot(c, lo)


def _sigmoid(x):
    return 1.0 / (1.0 + jnp.exp(-x))


def _silu(x):
    return x * _sigmoid(x)


def _softplus(x):
    return jnp.maximum(x, 0.0) + jnp.log1p(jnp.exp(-jnp.abs(x)))


def _rms(x, g):
    return x * lax.rsqrt(jnp.mean(x * x, axis=-1, keepdims=True) + EPS) * g


def _norm_proj_body(h_ref, g_ref, w_ref, o_ref):
    u = _rms(h_ref[...], g_ref[...])
    o_ref[...] = _dot(u.astype(BF), w_ref[...])


def _norm_proj(h, gain, w, tm):
    b, lp, d = h.shape
    n = w.shape[1]
    return pl.pallas_call(
        _norm_proj_body,
        grid=(b, lp // tm),
        in_specs=[
            pl.BlockSpec((None, tm, d), lambda i, j: (i, j, 0)),
            pl.BlockSpec((1, d), lambda i, j: (0, 0)),
            pl.BlockSpec((d, n), lambda i, j: (0, 0), pipeline_mode=pl.Buffered(1)),
        ],
        out_specs=pl.BlockSpec((None, tm, n), lambda i, j: (i, j, 0)),
        out_shape=jax.ShapeDtypeStruct((b, lp, n), F32),
        compiler_params=pltpu.CompilerParams(
            dimension_semantics=("parallel", "arbitrary"), vmem_limit_bytes=VMEM_LIMIT),
        name="norm_proj",
    )(h, gain.reshape(1, d), w)


def _embed_proj_body(x_ref, meta_ref, g_ref, w_ref, p_ref, h_ref, *, lead):
    xb = x_ref[0]
    tm, d = xb.shape
    n_zero = lead - meta_ref.shape[0]
    first = jnp.concatenate([jnp.zeros((n_zero, d), F32), meta_ref[...], xb[0:tm - lead, :]], axis=0)
    blk = jnp.where(pl.program_id(1) == 0, first, xb)
    h_ref[...] = blk
    p_ref[...] = _dot(_rms(blk, g_ref[...]).astype(BF), w_ref[...])


def _embed_proj(x, meta, gain, w, tm, lead):
    b, s, d = x.shape
    lp = s + lead
    n = w.shape[1]
    assert lead % 8 == 0 and meta.shape[0] % 8 == 0 and lead < tm and lp % tm == 0
    x_rows = lambda i, j: (i, pl.multiple_of(jnp.maximum(tm * j - lead, 0), 8), 0)
    return pl.pallas_call(
        functools.partial(_embed_proj_body, lead=lead),
        grid=(b, lp // tm),
        in_specs=[
            pl.BlockSpec((pl.Element(1), pl.Element(tm), pl.Element(d)), x_rows),
            pl.BlockSpec(meta.shape, lambda i, j: (0, 0)),
            pl.BlockSpec((1, d), lambda i, j: (0, 0)),
            pl.BlockSpec((d, n), lambda i, j: (0, 0), pipeline_mode=pl.Buffered(1)),
        ],
        out_specs=[pl.BlockSpec((None, tm, n), lambda i, j: (i, j, 0)),
                   pl.BlockSpec((None, tm, d), lambda i, j: (i, j, 0))],
        out_shape=[jax.ShapeDtypeStruct((b, lp, n), F32), jax.ShapeDtypeStruct((b, lp, d), F32)],
        compiler_params=pltpu.CompilerParams(
            dimension_semantics=("parallel", "arbitrary"), vmem_limit_bytes=VMEM_LIMIT),
        name="embed_proj",
    )(x, meta, gain.reshape(1, d), w)


def _shift_rows(a, carry_ref, cols, n_back, row):
    out = []
    for s in range(1, n_back + 1):
        sh = pltpu.roll(a, s, 0)
        for r in range(s):
            sh = jnp.where(row == r, carry_ref[8 - s + r:9 - s + r, cols], sh)
        out.append(sh)
    return out


def _merge_pieces(*lists):
    total = max(len(l) for l in lists)
    keyed = []
    for li, l in enumerate(lists):
        for k, piece in enumerate(l):
            keyed.append(((k + 0.5) * total / len(l), li, k, piece))
    keyed.sort(key=lambda t: t[:3])
    return [t[3] for t in keyed]


def _ffn_pieces(make_x, shift, g2_ref, wup_ref, cw_ref, cb_ref, wdn_ref, g3_ref, emit):
    n_tiles = D_FF // FFN_TILE
    st = {}

    def up(c):
        return [_dot(st["u"], wup_ref[:, part * D_FF + c * FFN_TILE:part * D_FF + (c + 1) * FFN_TILE])
                for part in range(2)]

    def head():
        st["x"] = make_x()
        st["u"] = _rms(st["x"], g2_ref[...]).astype(BF)
        st["acts"] = []

    def lead_in():
        st["pending"] = [up(c) for c in range(min(2, n_tiles))]

    def tile(c):
        def run():
            cur = st["pending"].pop(0)
            if c + 2 < n_tiles:
                st["pending"].append(up(c + 2))
            z = []
            for part in range(2):
                cols = slice(part * D_FF + c * FFN_TILE, part * D_FF + (c + 1) * FFN_TILE)
                a = cur[part]
                am1, am2 = shift(a, cols)
                z.append(am2 * cw_ref[0:1, cols] + am1 * cw_ref[1:2, cols] + a * cw_ref[2:3, cols] + cb_ref[:, cols])
            st["acts"].append((_silu(z[0]) * z[1]).astype(BF))
        return run

    def tail():
        acc = _dot(jnp.concatenate(st["acts"], axis=1), wdn_ref[...])
        emit(st["x"] + _rms(acc, g3_ref[...]))

    return [head, lead_in] + [tile(c) for c in range(n_tiles)] + [tail]


def _ffn_body(y_ref, wout_ref, g1_ref, h_ref, g2_ref, wup_ref, cw_ref, cb_ref, wdn_ref, g3_ref, o_ref, carry_ref,
              *, tm, pad):
    @pl.when(pl.program_id(1) == 0)
    def _():
        carry_ref[...] = jnp.zeros_like(carry_ref)

    half = tm // 2
    lists = []
    for r0 in (0, half):
        rows = slice(r0, r0 + half)
        t = pl.program_id(1) * tm + r0 + lax.broadcasted_iota(jnp.int32, (half, D_MODEL), 0)
        row = lax.broadcasted_iota(jnp.int32, (half, FFN_TILE), 0)

        def make_x(rows=rows, t=t):
            return jnp.where(t >= pad, h_ref[rows, :] + _rms(_dot(y_ref[rows, :], wout_ref[...]), g1_ref[...]), 0.0)

        def shift(a, cols, row=row):
            am1, am2 = _shift_rows(a, carry_ref, cols, FFN_CONV - 1, row)
            carry_ref[:, cols] = a[half - 8:half, :]
            return am1, am2

        def emit(out, rows=rows, t=t):
            o_ref[rows, :] = jnp.where(t >= pad, out, 0.0)

        lists.append(_ffn_pieces(make_x, shift, g2_ref, wup_ref, cw_ref, cb_ref, wdn_ref, g3_ref, emit))
    for piece in _merge_pieces(*lists):
        piece()


def _ffn_tail_body(y_ref, wout_ref, g1_ref, h_ref, g2_ref, wup_ref, cw_ref, cb_ref, wdn_ref, g3_ref, o_ref, *,
                   halo):
    half = o_ref.shape[0] // 2
    shift = lambda a, cols: (pltpu.roll(a, 1, 0), pltpu.roll(a, 2, 0))
    lists = []
    for r0 in (0, half):
        def make_x(r0=r0):
            rows = slice(r0, r0 + half + halo)
            return h_ref[0, rows, :] + _rms(_dot(y_ref[0, rows, :], wout_ref[...]), g1_ref[...])

        def emit(out, r0=r0):
            o_ref[r0:r0 + half, :] = out[halo:, :]

        lists.append(_ffn_pieces(make_x, shift, g2_ref, wup_ref, cw_ref, cb_ref, wdn_ref, g3_ref, emit))
    for piece in _merge_pieces(*lists):
        piece()


def _ffn_weight_specs(k, d, ff2):
    const = lambda i, j: (0, 0)
    return dict(
        w_out=pl.BlockSpec((k, d), const, pipeline_mode=pl.Buffered(1)), gain=pl.BlockSpec((1, d), const),
        w_up=pl.BlockSpec((d, ff2), const, pipeline_mode=pl.Buffered(1)),
        conv_w=pl.BlockSpec((FFN_CONV, ff2), const), conv_b=pl.BlockSpec((1, ff2), const),
        w_down=pl.BlockSpec((ff2 // 2, d), const, pipeline_mode=pl.Buffered(1)))


def _out_ffn(y, w_out, g1, h, g2, w_up, conv_w, conv_b, w_down, g3, tm, pad):
    b, lp, d = h.shape
    k = y.shape[2]
    ff2 = w_up.shape[1]
    assert lp % tm == 0 and tm % 16 == 0
    ws = _ffn_weight_specs(k, d, ff2)
    rows = lambda width: pl.BlockSpec((None, tm, width), lambda i, j: (i, j, 0))
    return pl.pallas_call(
        functools.partial(_ffn_body, tm=tm, pad=pad),
        grid=(b, lp // tm),
        in_specs=[rows(k), ws["w_out"], ws["gain"], rows(d), ws["gain"], ws["w_up"], ws["conv_w"], ws["conv_b"],
                  ws["w_down"], ws["gain"]],
        out_specs=rows(d),
        out_shape=jax.ShapeDtypeStruct((b, lp, d), F32),
        scratch_shapes=[pltpu.VMEM((8, ff2), F32)],
        compiler_params=pltpu.CompilerParams(
            dimension_semantics=("parallel", "arbitrary"), vmem_limit_bytes=VMEM_LIMIT),
        name="conv_ffn",
    )(y, w_out, g1.reshape(1, d), h, g2.reshape(1, d), w_up, conv_w, conv_b.reshape(1, ff2), w_down,
      g3.reshape(1, d))


def _out_ffn_tail(y, w_out, g1, h, g2, w_up, conv_w, conv_b, w_down, g3, tm, lead):
    b, lp, d = h.shape
    seq = lp - lead
    k = y.shape[2]
    ff2 = w_up.shape[1]
    halo = 8
    assert seq % tm == 0 and tm % 16 == 0 and lead >= halo and lead % 8 == 0
    ws = _ffn_weight_specs(k, d, ff2)
    rows = lambda width: pl.BlockSpec(
        (pl.Element(1), pl.Element(tm + halo), pl.Element(width)),
        lambda i, j: (i, pl.multiple_of(lead - halo + tm * j, 8), 0))
    return pl.pallas_call(
        functools.partial(_ffn_tail_body, halo=halo),
        grid=(b, seq // tm),
        in_specs=[rows(k), ws["w_out"], ws["gain"], rows(d), ws["gain"], ws["w_up"], ws["conv_w"], ws["conv_b"],
                  ws["w_down"], ws["gain"]],
        out_specs=pl.BlockSpec((None, tm, d), lambda i, j: (i, j, 0)),
        out_shape=jax.ShapeDtypeStruct((b, seq, d), F32),
        compiler_params=pltpu.CompilerParams(
            dimension_semantics=("parallel", "arbitrary"), vmem_limit_bytes=VMEM_LIMIT),
        name="conv_ffn_tail",
    )(y, w_out, g1.reshape(1, d), h, g2.reshape(1, d), w_up, conv_w, conv_b.reshape(1, ff2), w_down,
      g3.reshape(1, d))


def _hg_consts():
    c = CHUNK
    t = np.arange(c)
    small = [s for s in HG_LEVELS if s < HG_ROW_SPLIT]
    m = np.zeros((1 + len(small), c, c), np.float32)
    m[0] = t[None, :] <= t[:, None]
    masks = np.zeros((len(HG_LEVELS) + 1, c, c), np.float32)
    ii, jj = t[:, None], t[None, :]
    for li, s in enumerate(HG_LEVELS):
        if s < HG_ROW_SPLIT:
            for i in range(c):
                mid = (i // (2 * s)) * (2 * s) + s - 1
                if i % (2 * s) >= s:
                    m[1 + small.index(s), i, mid + 1:i + 1] = 1.0
                else:
                    m[1 + small.index(s), i, i + 1:mid + 1] = 1.0
        masks[li] = (ii // (2 * s) == jj // (2 * s)) & (ii % (2 * s) >= s) & (jj % (2 * s) < s)
    masks[-1] = ii == jj
    return m.reshape(-1, c), masks


def _even_body(p_ref, cos_ref, sin_ref, lbl_ref, hgain_ref, mall_ref, masks_ref, y_ref, sret_ref, shg_ref, *,
               layer):
    @pl.when(pl.program_id(1) == 0)
    def _():
        sret_ref[...] = jnp.zeros_like(sret_ref)
        shg_ref[...] = jnp.zeros_like(shg_ref)

    c = CHUNK
    cosv = cos_ref[...]
    sinv = sin_ref[...]
    ii = lax.broadcasted_iota(jnp.int32, (c, c), 0)
    jj = lax.broadcasted_iota(jnp.int32, (c, c), 1)
    diff = (ii - jj).astype(F32)
    rowi = lax.broadcasted_iota(jnp.int32, (c, RET_DK), 0).astype(F32)

    nb = p_ref.shape[0]
    col = lambda base, h: slice(base + h * 128, base + (h + 1) * 128)
    units = [(b, h) for b in range(nb) for h in range(RET_HEADS)]
    nu = len(units)
    lgs = [math.log1p(-(2.0 ** (-5.0 - h))) for _, h in units]
    qs = [p_ref[b, :, col(0, h)] for b, h in units]
    ks = [p_ref[b, :, col(RET_W, h)] for b, h in units]
    vs = [p_ref[b, :, col(2 * RET_W, h)] for b, h in units]
    qs = [q * cosv + pltpu.roll(q, RET_DK // 2, 1) * sinv for q in qs]
    ks = [(k * cosv + pltpu.roll(k, RET_DK // 2, 1) * sinv) * (RET_DK ** -0.5) for k in ks]
    ss = [sret_ref[b, h] for b, h in units]
    att = [_mm_nt(qs[u], ks[u]) * jnp.where(diff >= 0, jnp.exp(lgs[u] * jnp.maximum(diff, 0.0)), 0.0)
           for u in range(nu)]
    inter = [_mm(qs[u] * jnp.exp(lgs[u] * (rowi + 1.0)), ss[u]) for u in range(nu)]
    kv = [_mm_tn(ks[u] * jnp.exp(lgs[u] * (c - 1.0 - rowi)), vs[u]) for u in range(nu)]
    outs = [_mm(att[u], vs[u]) + inter[u] for u in range(nu)]
    for u, (b, h) in enumerate(units):
        sret_ref[b, h] = ss[u] * math.exp(lgs[u] * c) + kv[u]
        o = outs[u]
        xc = o - jnp.mean(o, axis=-1, keepdims=True)
        oa = xc * lax.rsqrt(jnp.mean(xc * xc, axis=-1, keepdims=True) + EPS)
        y_ref[b, :, col(0, h)] = (oa * _silu(p_ref[b, :, col(3 * RET_W, h)])).astype(y_ref.dtype)

    lgt = lbl_ref[...]
    mx = jnp.max(lgt, axis=0, keepdims=True)
    ex = jnp.exp(lgt - mx)
    sm = ex / jnp.sum(ex, axis=0, keepdims=True)
    lb_all = jnp.sum(sm[0:layer + 1, :], axis=0, keepdims=True)
    base = 4 * RET_W
    mall = mall_ref[...]
    nl = len(HG_LEVELS)
    units = [(b, h) for b in range(nb) for h in range(HG_HEADS)]
    nu = len(units)
    lbs = [lb_all[:, col(0, h)] for _, h in units]
    qs = [p_ref[b, :, col(base, h)] for b, h in units]
    fbs = [p_ref[b, :, col(base + HG_W, h)] for b, h in units]
    vs = [p_ref[b, :, col(base + 2 * HG_W, h)] for b, h in units]
    ks = [(1.0 - lbs[u]) * _sigmoid(-fbs[u]) for u in range(nu)]
    es = [_cmm2(mall, jnp.log(lbs[u] + (1.0 - lbs[u]) * _sigmoid(fbs[u]))) for u in range(nu)]
    xs = [jnp.exp(e) for e in es]
    rest = [jnp.exp(e[c - 1:c, :] - e[0:c, :]) for e in es]

    def level_factor(u, li):
        s = HG_LEVELS[li]
        if s < HG_ROW_SPLIT:
            k = 1 + [t for t in HG_LEVELS if t < HG_ROW_SPLIT].index(s)
            return xs[u][k * c:(k + 1) * c, :]
        cb = es[u][0:c, :]
        parts = []
        for p0 in range(0, c, 2 * s):
            mid = cb[p0 + s - 1:p0 + s, :]
            parts += [mid - cb[p0:p0 + s, :], cb[p0 + s:p0 + 2 * s, :] - mid]
        return jnp.exp(jnp.concatenate(parts, axis=0))

    sts = [shg_ref[b, h] for b, h in units]
    att = [masks_ref[nl] * _mm_nt(qs[u], ks[u]) for u in range(nu)]
    for li in range(nl):
        fac = [level_factor(u, li) for u in range(nu)]
        att = [att[u] + masks_ref[li] * _mm_nt(qs[u] * fac[u], ks[u] * fac[u]) for u in range(nu)]
    inter = [_mm_nt(qs[u] * xs[u][0:c, :], sts[u]) for u in range(nu)]
    kv = [_mm_tn(vs[u], ks[u] * rest[u]) for u in range(nu)]
    outs = [_mm(att[u], vs[u]) + inter[u] for u in range(nu)]
    for u, (b, h) in enumerate(units):
        shg_ref[b, h] = sts[u] * xs[u][c - 1:c, :] + kv[u]
        ob = _rms(outs[u], hgain_ref[:, col(0, h)])
        y_ref[b, :, col(RET_W, h)] = (ob * _silu(p_ref[b, :, col(base + 3 * HG_W, h)])).astype(y_ref.dtype)


def _even_mixer(p, cos_t, sin_t, lb_logits, hg_gain, layer):
    b, lp, n = p.shape
    nc = lp // CHUNK
    nb = EVEN_NB if b % EVEN_NB == 0 else 1
    mall, masks = _hg_consts()
    const2 = lambda i, j: (0, 0)
    return pl.pallas_call(
        functools.partial(_even_body, layer=layer),
        grid=(b // nb, nc),
        in_specs=[
            pl.BlockSpec((nb, CHUNK, n), lambda i, j: (i, j, 0)),
            pl.BlockSpec((CHUNK, RET_DK), lambda i, j: (j, 0)),
            pl.BlockSpec((CHUNK, RET_DK), lambda i, j: (j, 0)),
            pl.BlockSpec(lb_logits.shape, const2),
            pl.BlockSpec((1, HG_W), const2),
            pl.BlockSpec(mall.shape, const2),
            pl.BlockSpec(masks.shape, lambda i, j: (0, 0, 0)),
        ],
        out_specs=pl.BlockSpec((nb, CHUNK, RET_W + HG_W), lambda i, j: (i, j, 0)),
        out_shape=jax.ShapeDtypeStruct((b, lp, RET_W + HG_W), BF),
        scratch_shapes=[pltpu.VMEM((nb, RET_HEADS, RET_DK, RET_DK), F32),
                        pltpu.VMEM((nb, HG_HEADS, HG_DK, HG_DK), F32)],
        compiler_params=pltpu.CompilerParams(
            dimension_semantics=("parallel", "arbitrary"), vmem_limit_bytes=VMEM_LIMIT),
        name="even_mixer",
    )(p, cos_t, sin_t, lb_logits, hg_gain.reshape(1, HG_W), jnp.asarray(mall, BF), jnp.asarray(masks))


def _odd_body(p_ref, mu_ref, w0_ref, a0_ref, wa2_ref, g2_ref, kks_ref, kas_ref, rk_ref, lnw_ref, lnb_ref,
              cw_ref, alog_ref, dtb_ref, ggain_ref, tril_ref, bones_ref, y_ref,
              hrw_ref, sgd_ref, cpc_ref, ccv_ref):
    @pl.when(pl.program_id(1) == 0)
    def _():
        hrw_ref[...] = jnp.zeros_like(hrw_ref)
        sgd_ref[...] = jnp.zeros_like(sgd_ref)
        cpc_ref[...] = jnp.zeros_like(cpc_ref)
        ccv_ref[...] = jnp.zeros_like(ccv_ref)

    nb = p_ref.shape[0]
    c = CHUNK
    tril = tril_ref[...]
    bones = bones_ref[...]
    ii = lax.broadcasted_iota(jnp.int32, (c, c), 0)
    jj = lax.broadcasted_iota(jnp.int32, (c, c), 1)
    strict = ii > jj
    incl = ii >= jj
    eye = ii == jj
    eye_f = eye.astype(F32)
    ii2 = lax.broadcasted_iota(jnp.int32, (c, 2 * c), 0)
    lane2 = lax.broadcasted_iota(jnp.int32, (c, 2 * c), 1)
    jj2 = jnp.bitwise_and(lane2, c - 1)
    ak_mask = (ii2 > jj2) & (lane2 >= c)
    rbk_mask = ii2 >= jj2
    lane = lax.broadcasted_iota(jnp.int32, (c, 128), 1)
    heads = [slice(h * RW_HD, (h + 1) * RW_HD) for h in range(RW_HEADS)]
    base = RW_IN
    inv_hd = 1.0 / RW_HD

    def seg_sum(x):
        return jnp.concatenate(
            [_mm(x[:, p * 128:(p + 1) * 128], bones) for p in range(x.shape[1] // 128)], axis=1)

    def prep_pieces(b, d):
        def shift():
            pc = p_ref[b, :, 0:RW_IN]
            row = lax.broadcasted_iota(jnp.int32, (c, RW_IN), 0)
            prev = jnp.where(row == 0, cpc_ref[b, 7:8, :], pltpu.roll(pc, 1, 0))
            cpc_ref[b] = pc[c - 8:c, :]
            pcs = pc + (prev - pc) * mu_ref[...]
            d.update(r=pcs[:, 0:RW_W], k=pcs[:, RW_W:2 * RW_W], v=pcs[:, 2 * RW_W:3 * RW_W],
                     lo=pcs[:, 3 * RW_W:3 * RW_W + 128], glo=pcs[:, 3 * RW_W + 128:RW_IN])

        def lora():
            lo = d["lo"]
            wa = _mm(jnp.where(lane < RW_DECAY_LORA, jnp.tanh(lo), lo), wa2_ref[...])
            log_w = -_softplus(-(w0_ref[...] + wa[:, 0:RW_W])) - 0.5
            d["lw"] = -jnp.exp(log_w)
            d["a"] = _sigmoid(a0_ref[...] + wa[:, RW_W:2 * RW_W])
            d["gate"] = _mm(_sigmoid(d["glo"]), g2_ref[...])

        def keys():
            kks = d["k"] * kks_ref[...]
            kkn = kks * lax.rsqrt(seg_sum(kks * kks) + EPS)
            d["k2"] = d["k"] * (1.0 + (d["a"] - 1.0) * kas_ref[...])
            d["kkn"] = kkn
            d["beta"] = kkn * d["a"]
            d["c_inc"] = _cmm2(tril, d["lw"])

        def decays():
            c_inc, k2, beta = d["c_inc"], d["k2"], d["beta"]
            c_last = c_inc[c - 1:c, :]
            e_neg = jnp.exp(-c_inc)
            e_rest = jnp.exp(c_last - c_inc)
            d.update(ah=-d["kkn"] * jnp.exp(c_inc - d["lw"]), rh=d["r"] * jnp.exp(c_inc), bt=beta * e_neg,
                     kt=k2 * e_neg, bg=beta * e_rest, kg=k2 * e_rest, gc=jnp.exp(c_last),
                     ar=[], ls=[], ak=[], rbk=[], gcol=[], bgkg=[])

        def gram(h):
            def run():
                sl = heads[h]
                ar = jnp.concatenate([d["ah"][:, sl], d["rh"][:, sl]], axis=0).astype(BF)
                gm = _mm_nt(ar, jnp.concatenate([d["bt"][:, sl], d["kt"][:, sl]], axis=0))
                d["ar"].append(ar)
                d["ls"].append(jnp.where(strict, gm[0:c, 0:c], 0.0))
                d["ak"].append(jnp.where(ak_mask, gm[0:c, :], 0.0).astype(BF))
                d["rbk"].append(jnp.where(rbk_mask, gm[c:2 * c, :], 0.0).astype(BF))
                d["gcol"].append(jnp.sum(eye_f * d["gc"][:, sl], axis=1, keepdims=True))
                d["bgkg"].append(jnp.concatenate([d["bg"][:, sl], d["kg"][:, sl]], axis=0).astype(BF))
            return run

        def conv():
            x = p_ref[b, :, base:base + GDN_CONV_CH]
            rowc = lax.broadcasted_iota(jnp.int32, (c, GDN_CONV_CH), 0)
            xm1, xm2, xm3 = _shift_rows(x, ccv_ref.at[b], slice(None), GDN_CONV - 1, rowc)
            ccv_ref[b] = x[c - 8:c, :]
            d["qkv"] = _silu(xm3 * cw_ref[0:1, :] + xm2 * cw_ref[1:2, :] + xm1 * cw_ref[2:3, :] + x * cw_ref[3:4, :])

        def gates():
            sc = p_ref[b, :, base + GDN_CONV_CH + GDN_W:base + GDN_IN_PAD]
            g_all = -jnp.exp(alog_ref[...]) * _softplus(sc + dtb_ref[...])
            d["b_all"] = _sigmoid(sc)
            cg = _cmm3(tril, g_all)
            d["cg_all"] = jnp.concatenate(
                [jnp.broadcast_to(cg[:, h:h + 1], (c, 128)) for h in range(GDN_HEADS)], axis=1)
            d["gd"] = []

        def gdn_head(h):
            def run():
                hs = slice(h * 128, (h + 1) * 128)
                qkv = d["qkv"]
                q = qkv[:, hs]
                kd = qkv[:, GDN_W + h * 128:GDN_W + (h + 1) * 128]
                vd = qkv[:, 2 * GDN_W + h * 128:2 * GDN_W + (h + 1) * 128]
                q = q * lax.rsqrt(jnp.sum(q * q, axis=-1, keepdims=True) + EPS) * (GDN_DK ** -0.5)
                kd = kd * lax.rsqrt(jnp.sum(kd * kd, axis=-1, keepdims=True) + EPS)
                b_b = jnp.broadcast_to(d["b_all"][:, GDN_HEADS + h:GDN_HEADS + h + 1], (c, 128))
                cg = d["cg_all"][:, hs]
                cg_row = jnp.sum(jnp.where(eye, cg[:, 0:c], 0.0), axis=0, keepdims=True)
                decay = jnp.exp(jnp.where(incl, cg[:, 0:c] - cg_row, -jnp.inf))
                eg = jnp.exp(cg)
                cl = cg[c - 1:c, :]
                d["gd"].append(dict(
                    l=-jnp.where(strict, b_b[:, 0:c] * _mm_nt(kd, kd) * decay, 0.0),
                    rhs=jnp.concatenate([b_b * vd, b_b * kd * eg], axis=1).astype(BF),
                    qk=(_mm_nt(q, kd) * decay).astype(BF), q_in=(q * eg).astype(BF),
                    k_out=(kd * jnp.exp(cl - cg)).astype(BF), sd=jnp.exp(cl)))
            return run

        return ([shift, lora, keys, decays] + [gram(h) for h in range(RW_HEADS)]
                + [conv, gates] + [gdn_head(h) for h in range(GDN_HEADS)])

    def inverse_pieces(rows, ds, out):
        st = {}

        def start():
            ls = [l for b in rows for l in ds[b]["ls"]] + [g["l"] for b in rows for g in ds[b]["gd"]]
            st["p"] = [eye_f + l for l in ls]
            st["l"] = [l.astype(BF) for l in ls]

        def square():
            st["l"] = [_mm(l, l).astype(BF) for l in st["l"]]

        def extend():
            st["p"] = [p + _mm(l, p) for l, p in zip(st["l"], st["p"])]

        def finish():
            out["tinv"] = st["p"]

        levels = []
        span = 2
        while span < c:
            levels += [square, extend]
            span *= 2
        return [start] + levels + [finish]

    def state_pieces(rows, ds, inv):
        units = [(b, h) for b in rows for h in range(RW_HEADS)]
        gunits = [(b, h) for b in rows for h in range(GDN_HEADS)]
        st = {}

        def rw_read():
            st["h0"] = [hrw_ref[b, h] for b, h in units]
            st["hr"] = [_mm(ds[b]["ar"][h], st["h0"][u]) for u, (b, h) in enumerate(units)]
            st["vs"] = [ds[b]["v"][:, heads[h]].astype(BF) for b, h in units]

        def rw_mix():
            vs = st["vs"]
            st["x"] = [st["hr"][u][0:c, :] + _mm(ds[b]["ak"][h], jnp.concatenate([vs[u], vs[u]], axis=0))
                       for u, (b, h) in enumerate(units)]

        def rw_solve():
            us = [_mm(inv["tinv"][u], st["x"][u]) for u in range(len(units))]
            st["uv"] = [jnp.concatenate([us[u].astype(BF), st["vs"][u]], axis=0) for u in range(len(units))]

        def rw_out():
            st["ys"] = [st["hr"][u][c:2 * c, :] + _mm(ds[b]["rbk"][h], st["uv"][u])
                        for u, (b, h) in enumerate(units)]
            for u, (b, h) in enumerate(units):
                hrw_ref[b, h] = st["h0"][u] * ds[b]["gcol"][h] + _mm_tn(ds[b]["bgkg"][h], st["uv"][u])

        def rw_center():
            st["y"] = [jnp.concatenate(st["ys"][k * RW_HEADS:(k + 1) * RW_HEADS], axis=1) for k in range(len(rows))]
            st["mean"] = [seg_sum(y) * inv_hd for y in st["y"]]
            st["bonus"] = [seg_sum(ds[b]["r"] * ds[b]["k2"] * rk_ref[...]) for b in rows]

        def rw_scale():
            st["yc"] = [y - m for y, m in zip(st["y"], st["mean"])]
            st["var"] = [seg_sum(yc * yc) * inv_hd for yc in st["yc"]]

        def rw_finish():
            for k, b in enumerate(rows):
                d = ds[b]
                yn = st["yc"][k] * lax.rsqrt(st["var"][k] + RW_LNX_EPS) * lnw_ref[...] + lnb_ref[...]
                out = (yn + st["bonus"][k] * d["v"]) * d["gate"]
                y_ref[b, :, 0:RW_W] = out.astype(y_ref.dtype)

        def gd_solve():
            n_rw = len(units)
            sols = [_mm(inv["tinv"][n_rw + u], ds[b]["gd"][h]["rhs"]) for u, (b, h) in enumerate(gunits)]
            st["s0"] = [sgd_ref[b, h] for b, h in gunits]
            st["v_new"] = [sols[u][:, 0:128] - _mm(sols[u][:, 128:256], st["s0"][u]) for u in range(len(gunits))]

        def gd_out(u, b, h):
            def run():
                g = ds[b]["gd"][h]
                s0, v_new = st["s0"][u], st["v_new"][u]
                o = _mm(g["q_in"], s0) + _mm(g["qk"], v_new)
                sgd_ref[b, h] = s0 * g["sd"] + _mm_tn(g["k_out"], v_new)
                og = p_ref[b, :, base + GDN_CONV_CH + h * 128:base + GDN_CONV_CH + (h + 1) * 128]
                out = _rms(o, ggain_ref[...]) * _silu(og)
                y_ref[b, :, RW_W + h * 128:RW_W + (h + 1) * 128] = out.astype(y_ref.dtype)
            return run

        gd = [gd_solve] + [gd_out(u, b, h) for u, (b, h) in enumerate(gunits)]
        cut = len(gd) // 2
        return [rw_read, rw_mix, rw_solve, rw_out, rw_center] + gd[:cut] + [rw_scale] + gd[cut:] + [rw_finish]

    group = min(ODD_GROUP, nb)
    groups = [list(range(g, g + group)) for g in range(0, nb, group)]
    ds = [dict() for _ in range(nb)]
    invs = [dict() for _ in groups]
    stage1 = [_merge_pieces(*[prep_pieces(b, ds[b]) for b in rows]) for rows in groups]
    stage2 = [inverse_pieces(rows, ds, invs[g]) for g, rows in enumerate(groups)]
    stage3 = [state_pieces(rows, ds, invs[g]) for g, rows in enumerate(groups)]
    for slot in range(len(groups) + 2):
        active = []
        if slot < len(groups):
            active.append(stage1[slot])
        if 0 <= slot - 1 < len(groups):
            active.append(stage2[slot - 1])
        if 0 <= slot - 2 < len(groups):
            active.append(stage3[slot - 2])
        for piece in _merge_pieces(*active):
            piece()


def _odd_mixer(p, mu, w0, a0, wa2, g2, kks, kas, rk, lnw, lnb, conv_w, alog, dtb, ggain):
    b, lp, n = p.shape
    nc = lp // CHUNK
    nb = ODD_NB if b % ODD_NB == 0 else 1
    t = np.arange(CHUNK)
    tril = jnp.asarray(t[None, :] <= t[:, None], BF)
    l = np.arange(128)
    bones = jnp.asarray(l[:, None] // RW_HD == l[None, :] // RW_HD, BF)
    row = lambda a: a.reshape(1, -1)
    consts = [row(mu), row(w0), row(a0), wa2, g2, row(kks), row(kas), row(rk), row(lnw), row(lnb),
              conv_w, row(alog), row(dtb), row(ggain), tril, bones]
    const2 = lambda i, j: (0, 0)
    return pl.pallas_call(
        _odd_body,
        grid=(b // nb, nc),
        in_specs=[pl.BlockSpec((nb, CHUNK, n), lambda i, j: (i, j, 0))]
        + [pl.BlockSpec(a.shape, const2) for a in consts],
        out_specs=pl.BlockSpec((nb, CHUNK, RW_W + GDN_W), lambda i, j: (i, j, 0)),
        out_shape=jax.ShapeDtypeStruct((b, lp, RW_W + GDN_W), BF),
        scratch_shapes=[
            pltpu.VMEM((nb, RW_HEADS, RW_HD, RW_HD), F32),
            pltpu.VMEM((nb, GDN_HEADS, GDN_DK, GDN_DK), F32),
            pltpu.VMEM((nb, 8, RW_IN), F32),
            pltpu.VMEM((nb, 8, GDN_CONV_CH), F32),
        ],
        compiler_params=pltpu.CompilerParams(
            dimension_semantics=("parallel", "arbitrary"), vmem_limit_bytes=VMEM_LIMIT),
        name="odd_mixer",
    )(p, *consts)


def _row_tile(lp):
    best = None
    for tm in range(16, min(lp, 704) + 1, 16):
        if lp % tm == 0:
            best = tm
    assert best is not None, "sequence length must be a multiple of 16"
    return best


def kernel(x, meta_tokens, norm_gains, w_in_even, w_out_even, hg_lb_logits, hg_norm_gain, w_in_odd, w_out_odd, rw_mu, rw_w0, rw_w2, rw_a0, rw_a2, rw_g2, rw_kk_scale, rw_ka_scale, rw_rk, rw_lnx_w, rw_lnx_b, gdn_conv_w, gdn_a_log, gdn_dt_bias, gdn_norm_gain, ffn_w_up, ffn_conv_w, ffn_conv_b, ffn_w_down):
    bsz, seq, d = x.shape
    depth = norm_gains.shape[0]
    l = N_META + seq
    pad = (-l) % CHUNK
    lp = l + pad
    tm = _row_tile(lp)
    lead = pad + N_META

    half = RET_DK // 2
    pos = (jnp.arange(lp, dtype=jnp.int32) - pad).astype(F32)
    inv = ROPE_BASE ** (-jnp.arange(half, dtype=F32) / half)
    ang = pos[:, None] * inv[None, :]
    cos_t = jnp.concatenate([jnp.cos(ang), jnp.cos(ang)], axis=1)
    sin_t = jnp.concatenate([-jnp.sin(ang), jnp.sin(ang)], axis=1)

    h = None
    for layer in range(depth):
        g = norm_gains[layer]
        i = layer // 2
        if layer % 2 == 0:
            w_in = w_in_even[i].astype(BF)
        else:
            w_in = jnp.pad(w_in_odd[i], ((0, 0), (0, ODD_IN_PAD - w_in_odd.shape[2]))).astype(BF)
        if layer == 0:
            p, h = _embed_proj(x, meta_tokens.astype(x.dtype), g[0], w_in, tm, lead)
        else:
            p = _norm_proj(h, g[0], w_in, tm)
        if layer % 2 == 0:
            y = _even_mixer(p, cos_t, sin_t, hg_lb_logits, hg_norm_gain[i], layer)
            w_out = w_out_even[i]
        else:
            wa2 = jnp.zeros((RW_DECAY_LORA + RW_AAA_LORA, 2 * RW_W), F32)
            wa2 = wa2.at[:RW_DECAY_LORA, :RW_W].set(rw_w2[i]).at[RW_DECAY_LORA:, RW_W:].set(rw_a2[i])
            lane_pad = lambda a: jnp.pad(a, (0, LANES - a.shape[0]))
            y = _odd_mixer(p, rw_mu[i], rw_w0[i], rw_a0[i], wa2.astype(BF), rw_g2[i].astype(BF),
                           rw_kk_scale[i], rw_ka_scale[i], rw_rk[i].reshape(-1), rw_lnx_w[i], rw_lnx_b[i],
                           gdn_conv_w[i], lane_pad(gdn_a_log[i]), lane_pad(gdn_dt_bias[i]), gdn_norm_gain[i])
            w_out = w_out_odd[i]
        ffn_args = (y, w_out.astype(BF), g[1], h, g[2], ffn_w_up[layer].astype(BF), ffn_conv_w[layer],
                    ffn_conv_b[layer], ffn_w_down[layer].astype(BF), g[3])
        if layer + 1 < depth:
            h = _out_ffn(*ffn_args, tm, pad)
        else:
            h = _out_ffn_tail(*ffn_args, _row_tile(seq), lead)
    return h
```
